```python
import jax, jax.numpy as jnp
from jax import lax
import numpy as np

D_MODEL = 4096
BATCH = 16
SEQ = 2048
DEPTH = 1

SG_WIDTH = 2048
SG_GROUPS = 8
SG_GROUP_DIM = SG_WIDTH // SG_GROUPS
CHUNK = 128
LRU_WIDTH = 4096
LRU_HEADS = 16
LRU_HEAD_DIM = LRU_WIDTH // LRU_HEADS
LRU_CONV = 4
LRU_C = 8.0
FFN_WIDTH = 3 * D_MODEL
FFN_CONV = 3
EPS = 1e-6
IN_SPLITS = [SG_WIDTH, 2 * SG_WIDTH, 2 * SG_WIDTH + LRU_WIDTH, 2 * SG_WIDTH + 2 * LRU_WIDTH,
             2 * SG_WIDTH + 2 * LRU_WIDTH + D_MODEL]
IN_COLS = 2 * SG_WIDTH + 2 * LRU_WIDTH + 2 * D_MODEL

kernel_name = "hybrid_gmlp_rglru_gated_merge"


def rmsnorm(x, g):
    xf = x.astype(jnp.float32)
    var = jnp.mean(xf * xf, axis=-1, keepdims=True)
    return (xf * lax.rsqrt(var + EPS) * g.astype(jnp.float32)).astype(x.dtype)


def layernorm(x, g, b):
    xf = x.astype(jnp.float32)
    mu = jnp.mean(xf, axis=-1, keepdims=True)
    xc = xf - mu
    var = jnp.mean(xc * xc, axis=-1, keepdims=True)
    return (xc * lax.rsqrt(var + EPS) * g.astype(jnp.float32) + b.astype(jnp.float32)).astype(x.dtype)


def causal_dwconv(x, w, b):
    K = w.shape[0]
    S = x.shape[1]
    xp = jnp.pad(x, ((0, 0), (K - 1, 0), (0, 0)))
    w = w.astype(x.dtype)
    out = xp[:, 0:S] * w[0]
    for k in range(1, K):
        out = out + xp[:, k:k + S] * w[k]
    return out + b.astype(x.dtype)


def spatial_gating(u, v, ln_g, ln_b, w_s, b_s):
    B, S, _ = u.shape
    vn = layernorm(v, ln_g, ln_b).reshape(B, S // CHUNK, CHUNK, SG_GROUPS, SG_GROUP_DIM)
    mask = jnp.tril(jnp.ones((CHUNK, CHUNK), dtype=bool))
    w = jnp.where(mask, w_s, 0.0).astype(vn.dtype)
    mixed = jnp.einsum('gts,bcsgd->bctgd', w, vn) + b_s.T.astype(vn.dtype)[:, :, None]
    return u * mixed.reshape(B, S, SG_WIDTH)


def rg_lru(x, w_a, b_a, w_x, b_x, lam):
    B, S, W = x.shape
    xf = x.astype(jnp.float32)
    xh = xf.reshape(B, S, LRU_HEADS, LRU_HEAD_DIM)
    rec_gate = jax.nn.sigmoid(jnp.einsum('bshi,hij->bshj', xh, w_a.astype(jnp.float32)).reshape(B, S, W)
                              + b_a.astype(jnp.float32))
    in_gate = jax.nn.sigmoid(jnp.einsum('bshi,hij->bshj', xh, w_x.astype(jnp.float32)).reshape(B, S, W)
                             + b_x.astype(jnp.float32))
    log_a = -LRU_C * rec_gate * jax.nn.softplus(-lam.astype(jnp.float32))
    a = jnp.exp(log_a)
    gated_x = jnp.sqrt(-jnp.expm1(2.0 * log_a)) * (in_gate * xf)

    def combine(left, right):
        a1, b1 = left
        a2, b2 = right
        return a1 * a2, a2 * b1 + b2

    _, h = lax.associative_scan(combine, (a, gated_x), axis=1)
    return h.astype(x.dtype)


def hybrid_layer(x, g_mix, w_in, sg_ln_g, sg_ln_b, sg_w, sg_b, lru_conv_w, lru_conv_b,
                 lru_wa, lru_ba, lru_wx, lru_bx, lru_lam, p_sg, p_lru, w_out,
                 g_ffn, w_up, ffn_conv_w, ffn_conv_b, w_down):
    h = rmsnorm(x, g_mix)
    proj = h @ w_in.astype(h.dtype)
    z_u, z_v, x_r, y_r, gate_a, gate_b = jnp.split(proj, IN_SPLITS, axis=-1)
    y_a = spatial_gating(jax.nn.gelu(z_u), jax.nn.gelu(z_v), sg_ln_g, sg_ln_b, sg_w, sg_b)
    x_r = causal_dwconv(x_r, lru_conv_w, lru_conv_b)
    y_b = rg_lru(x_r, lru_wa, lru_ba, lru_wx, lru_bx, lru_lam) * jax.nn.gelu(y_r)
    merged = (jax.nn.sigmoid(gate_a) * (y_a @ p_sg.astype(y_a.dtype))
              + jax.nn.sigmoid(gate_b) * (y_b @ p_lru.astype(y_b.dtype)))
    x = x + merged @ w_out.astype(merged.dtype)
    h = rmsnorm(x, g_ffn)
    up = causal_dwconv(h @ w_up.astype(h.dtype), ffn_conv_w, ffn_conv_b)
    c_gate, c_val = jnp.split(up, 2, axis=-1)
    x = x + (jax.nn.gelu(c_gate) * c_val) @ w_down.astype(up.dtype)
    return x


def _fwd_setup_inputs(seed: int = 0) -> dict:
    key = jax.random.key(seed)
    ks = jax.random.split(key, 24)
    f32 = jnp.float32
    L = DEPTH

    def nrm(k, shape, scale):
        return jax.random.normal(k, shape, f32) * scale

    a0 = jax.random.uniform(ks[14], (L, LRU_WIDTH), f32, 0.9, 0.999)
    p = a0 ** (1.0 / LRU_C)
    lru_lam = jnp.log(p) - jnp.log1p(-p)
    return {
        "x": nrm(ks[0], (BATCH, SEQ, D_MODEL), 1.0),
        "g_mix": 1.0 + nrm(ks[1], (L, D_MODEL), 0.02),
        "w_in": nrm(ks[2], (L, D_MODEL, IN_COLS), D_MODEL ** -0.5),
        "sg_ln_g": 1.0 + nrm(ks[3], (L, SG_WIDTH), 0.02),
        "sg_ln_b": nrm(ks[4], (L, SG_WIDTH), 0.02),
        "sg_w": nrm(ks[5], (L, SG_GROUPS, CHUNK, CHUNK), CHUNK ** -0.5),
        "sg_b": 1.0 + nrm(ks[6], (L, SG_GROUPS, CHUNK), 0.02),
        "lru_conv_w": nrm(ks[7], (L, LRU_CONV, LRU_WIDTH), LRU_CONV ** -0.5),
        "lru_conv_b": nrm(ks[8], (L, LRU_WIDTH), 0.02),
        "lru_wa": nrm(ks[9], (L, LRU_HEADS, LRU_HEAD_DIM, LRU_HEAD_DIM), LRU_HEAD_DIM ** -0.5),
        "lru_ba": nrm(ks[10], (L, LRU_WIDTH), 0.02),
        "lru_wx": nrm(ks[11], (L, LRU_HEADS, LRU_HEAD_DIM, LRU_HEAD_DIM), LRU_HEAD_DIM ** -0.5),
        "lru_bx": nrm(ks[12], (L, LRU_WIDTH), 0.02),
        "lru_lam": lru_lam,
        "p_sg": nrm(ks[15], (L, SG_WIDTH, D_MODEL), SG_WIDTH ** -0.5),
        "p_lru": nrm(ks[16], (L, LRU_WIDTH, D_MODEL), LRU_WIDTH ** -0.5),
        "w_out": nrm(ks[17], (L, D_MODEL, D_MODEL), D_MODEL ** -0.5),
        "g_ffn": 1.0 + nrm(ks[18], (L, D_MODEL), 0.02),
        "w_up": nrm(ks[19], (L, D_MODEL, 2 * FFN_WIDTH), D_MODEL ** -0.5),
        "ffn_conv_w": nrm(ks[20], (L, FFN_CONV, 2 * FFN_WIDTH), FFN_CONV ** -0.5),
        "ffn_conv_b": nrm(ks[21], (L, 2 * FFN_WIDTH), 0.02),
        "w_down": nrm(ks[22], (L, FFN_WIDTH, D_MODEL), FFN_WIDTH ** -0.5),
        "g_final": 1.0 + nrm(ks[23], (D_MODEL,), 0.02),
    }


def _fwd_reference(x, g_mix, w_in, sg_ln_g, sg_ln_b, sg_w, sg_b, lru_conv_w, lru_conv_b,
              lru_wa, lru_ba, lru_wx, lru_bx, lru_lam, p_sg, p_lru, w_out,
              g_ffn, w_up, ffn_conv_w, ffn_conv_b, w_down, g_final):
    for l in range(DEPTH):
        x = hybrid_layer(x, g_mix[l], w_in[l], sg_ln_g[l], sg_ln_b[l], sg_w[l], sg_b[l],
                         lru_conv_w[l], lru_conv_b[l], lru_wa[l], lru_ba[l], lru_wx[l], lru_bx[l],
                         lru_lam[l], p_sg[l], p_lru[l], w_out[l], g_ffn[l], w_up[l],
                         ffn_conv_w[l], ffn_conv_b[l], w_down[l])
    return rmsnorm(x, g_final)


import jax as _jax
import jax.numpy as _jnp

TWIN_FORMAT = 'train_step'
FWD_PARAMS = ['x', 'g_mix', 'w_in', 'sg_ln_g', 'sg_ln_b', 'sg_w', 'sg_b', 'lru_conv_w', 'lru_conv_b', 'lru_wa', 'lru_ba', 'lru_wx', 'lru_bx', 'lru_lam', 'p_sg', 'p_lru', 'w_out', 'g_ffn', 'w_up', 'ffn_conv_w', 'ffn_conv_b', 'w_down', 'g_final']
TWIN_WEIGHTS = ['g_mix', 'w_in', 'sg_ln_g', 'sg_ln_b', 'sg_w', 'sg_b', 'lru_conv_w', 'lru_conv_b', 'lru_wa', 'lru_ba', 'lru_wx', 'lru_bx', 'lru_lam', 'p_sg', 'p_lru', 'w_out', 'g_ffn', 'w_up', 'ffn_conv_w', 'ffn_conv_b', 'w_down', 'g_final']
TWIN_DIFF_INPUT = 'x'
TWIN_INPUTS = ['x', 'g_mix', 'w_in', 'sg_ln_g', 'sg_ln_b', 'sg_w', 'sg_b', 'lru_conv_w', 'lru_conv_b', 'lru_wa', 'lru_ba', 'lru_wx', 'lru_bx', 'lru_lam', 'p_sg', 'p_lru', 'w_out', 'g_ffn', 'w_up', 'ffn_conv_w', 'ffn_conv_b', 'w_down', 'g_final', 'loss_target', 'm_g_mix', 'm_w_in', 'm_sg_ln_g', 'm_sg_ln_b', 'm_sg_w', 'm_sg_b', 'm_lru_conv_w', 'm_lru_conv_b', 'm_lru_wa', 'm_lru_ba', 'm_lru_wx', 'm_lru_bx', 'm_lru_lam', 'm_p_sg', 'm_p_lru', 'm_w_out', 'm_g_ffn', 'm_w_up', 'm_ffn_conv_w', 'm_ffn_conv_b', 'm_w_down', 'm_g_final', 'v_g_mix', 'v_w_in', 'v_sg_ln_g', 'v_sg_ln_b', 'v_sg_w', 'v_sg_b', 'v_lru_conv_w', 'v_lru_conv_b', 'v_lru_wa', 'v_lru_ba', 'v_lru_wx', 'v_lru_bx', 'v_lru_lam', 'v_p_sg', 'v_p_lru', 'v_w_out', 'v_g_ffn', 'v_w_up', 'v_ffn_conv_w', 'v_ffn_conv_b', 'v_w_down', 'v_g_final']
TWIN_OUTPUTS = ['loss', 'grad_x', 'grad_g_mix', 'grad_w_in', 'grad_sg_ln_g', 'grad_sg_ln_b', 'grad_sg_w', 'grad_sg_b', 'grad_lru_conv_w', 'grad_lru_conv_b', 'grad_lru_wa', 'grad_lru_ba', 'grad_lru_wx', 'grad_lru_bx', 'grad_lru_lam', 'grad_p_sg', 'grad_p_lru', 'grad_w_out', 'grad_g_ffn', 'grad_w_up', 'grad_ffn_conv_w', 'grad_ffn_conv_b', 'grad_w_down', 'grad_g_final', 'delta_g_mix', 'delta_w_in', 'delta_sg_ln_g', 'delta_sg_ln_b', 'delta_sg_w', 'delta_sg_b', 'delta_lru_conv_w', 'delta_lru_conv_b', 'delta_lru_wa', 'delta_lru_ba', 'delta_lru_wx', 'delta_lru_bx', 'delta_lru_lam', 'delta_p_sg', 'delta_p_lru', 'delta_w_out', 'delta_g_ffn', 'delta_w_up', 'delta_ffn_conv_w', 'delta_ffn_conv_b', 'delta_w_down', 'delta_g_final', 'new_m_g_mix', 'new_m_w_in', 'new_m_sg_ln_g', 'new_m_sg_ln_b', 'new_m_sg_w', 'new_m_sg_b', 'new_m_lru_conv_w', 'new_m_lru_conv_b', 'new_m_lru_wa', 'new_m_lru_ba', 'new_m_lru_wx', 'new_m_lru_bx', 'new_m_lru_lam', 'new_m_p_sg', 'new_m_p_lru', 'new_m_w_out', 'new_m_g_ffn', 'new_m_w_up', 'new_m_ffn_conv_w', 'new_m_ffn_conv_b', 'new_m_w_down', 'new_m_g_final', 'new_v_g_mix', 'new_v_w_in', 'new_v_sg_ln_g', 'new_v_sg_ln_b', 'new_v_sg_w', 'new_v_sg_b', 'new_v_lru_conv_w', 'new_v_lru_conv_b', 'new_v_lru_wa', 'new_v_lru_ba', 'new_v_lru_wx', 'new_v_lru_bx', 'new_v_lru_lam', 'new_v_p_sg', 'new_v_p_lru', 'new_v_w_out', 'new_v_g_ffn', 'new_v_w_up', 'new_v_ffn_conv_w', 'new_v_ffn_conv_b', 'new_v_w_down', 'new_v_g_final']
TWIN_LEAF_KINDS = {'loss': 'loss', 'grad_x': 'grad_x', 'grad_g_mix': 'grad_w', 'grad_w_in': 'grad_w', 'grad_sg_ln_g': 'grad_w', 'grad_sg_ln_b': 'grad_w', 'grad_sg_w': 'grad_w', 'grad_sg_b': 'grad_w', 'grad_lru_conv_w': 'grad_w', 'grad_lru_conv_b': 'grad_w', 'grad_lru_wa': 'grad_w', 'grad_lru_ba': 'grad_w', 'grad_lru_wx': 'grad_w', 'grad_lru_bx': 'grad_w', 'grad_lru_lam': 'grad_w', 'grad_p_sg': 'grad_w', 'grad_p_lru': 'grad_w', 'grad_w_out': 'grad_w', 'grad_g_ffn': 'grad_w', 'grad_w_up': 'grad_w', 'grad_ffn_conv_w': 'grad_w', 'grad_ffn_conv_b': 'grad_w', 'grad_w_down': 'grad_w', 'grad_g_final': 'grad_w', 'delta_g_mix': 'delta_w', 'delta_w_in': 'delta_w', 'delta_sg_ln_g': 'delta_w', 'delta_sg_ln_b': 'delta_w', 'delta_sg_w': 'delta_w', 'delta_sg_b': 'delta_w', 'delta_lru_conv_w': 'delta_w', 'delta_lru_conv_b': 'delta_w', 'delta_lru_wa': 'delta_w', 'delta_lru_ba': 'delta_w', 'delta_lru_wx': 'delta_w', 'delta_lru_bx': 'delta_w', 'delta_lru_lam': 'delta_w', 'delta_p_sg': 'delta_w', 'delta_p_lru': 'delta_w', 'delta_w_out': 'delta_w', 'delta_g_ffn': 'delta_w', 'delta_w_up': 'delta_w', 'delta_ffn_conv_w': 'delta_w', 'delta_ffn_conv_b': 'delta_w', 'delta_w_down': 'delta_w', 'delta_g_final': 'delta_w', 'new_m_g_mix': 'new_m', 'new_m_w_in': 'new_m', 'new_m_sg_ln_g': 'new_m', 'new_m_sg_ln_b': 'new_m', 'new_m_sg_w': 'new_m', 'new_m_sg_b': 'new_m', 'new_m_lru_conv_w': 'new_m', 'new_m_lru_conv_b': 'new_m', 'new_m_lru_wa': 'new_m', 'new_m_lru_ba': 'new_m', 'new_m_lru_wx': 'new_m', 'new_m_lru_bx': 'new_m', 'new_m_lru_lam': 'new_m', 'new_m_p_sg': 'new_m', 'new_m_p_lru': 'new_m', 'new_m_w_out': 'new_m', 'new_m_g_ffn': 'new_m', 'new_m_w_up': 'new_m', 'new_m_ffn_conv_w': 'new_m', 'new_m_ffn_conv_b': 'new_m', 'new_m_w_down': 'new_m', 'new_m_g_final': 'new_m', 'new_v_g_mix': 'new_v', 'new_v_w_in': 'new_v', 'new_v_sg_ln_g': 'new_v', 'new_v_sg_ln_b': 'new_v', 'new_v_sg_w': 'new_v', 'new_v_sg_b': 'new_v', 'new_v_lru_conv_w': 'new_v', 'new_v_lru_conv_b': 'new_v', 'new_v_lru_wa': 'new_v', 'new_v_lru_ba': 'new_v', 'new_v_lru_wx': 'new_v', 'new_v_lru_bx': 'new_v', 'new_v_lru_lam': 'new_v', 'new_v_p_sg': 'new_v', 'new_v_p_lru': 'new_v', 'new_v_w_out': 'new_v', 'new_v_g_ffn': 'new_v', 'new_v_w_up': 'new_v', 'new_v_ffn_conv_w': 'new_v', 'new_v_ffn_conv_b': 'new_v', 'new_v_w_down': 'new_v', 'new_v_g_final': 'new_v'}


def _forward(args):
    return _fwd_reference(*[args[k] for k in FWD_PARAMS])


def _output_shape():
    def fwd():
        inp = _fwd_setup_inputs(0)
        return _fwd_reference(*[inp[k] for k in FWD_PARAMS])
    out = _jax.eval_shape(fwd)
    return out.shape, out.dtype

N_MICROBATCH = 1
ADAM_LR = 0.001
ADAM_B1 = 0.9
ADAM_B2 = 0.999
ADAM_EPS = 1e-08
ADAM_WD = 0.01
ADAM_STEP = 10
PER_EXAMPLE_BATCH_AXIS = {'x': 0, 'loss_target': 0}
SHARED_INPUTS = []
_WEIGHT_DTYPES = {'g_mix': _jnp.float32, 'w_in': _jnp.float32, 'sg_ln_g': _jnp.float32, 'sg_ln_b': _jnp.float32, 'sg_w': _jnp.float32, 'sg_b': _jnp.float32, 'lru_conv_w': _jnp.float32, 'lru_conv_b': _jnp.float32, 'lru_wa': _jnp.float32, 'lru_ba': _jnp.float32, 'lru_wx': _jnp.float32, 'lru_bx': _jnp.float32, 'lru_lam': _jnp.float32, 'p_sg': _jnp.float32, 'p_lru': _jnp.float32, 'w_out': _jnp.float32, 'g_ffn': _jnp.float32, 'w_up': _jnp.float32, 'ffn_conv_w': _jnp.float32, 'ffn_conv_b': _jnp.float32, 'w_down': _jnp.float32, 'g_final': _jnp.float32}
MOMENT_SCALE = {'g_mix': 2.912539e-02, 'w_in': 1.300010e-02, 'sg_ln_g': 1.616018e-02, 'sg_ln_b': 1.521020e-02, 'sg_w': 2.255740e-02, 'sg_b': 3.298574e-02, 'lru_conv_w': 1.016352e-02, 'lru_conv_b': 1.145595e-01, 'lru_wa': 2.802288e-03, 'lru_ba': 2.443819e-03, 'lru_wx': 4.939274e-03, 'lru_bx': 3.716052e-03, 'lru_lam': 5.172152e-03, 'p_sg': 1.962510e-02, 'p_lru': 9.907286e-03, 'w_out': 2.196392e-02, 'g_ffn': 3.292436e-02, 'w_up': 1.336338e-02, 'ffn_conv_w': 1.331443e-02, 'ffn_conv_b': 1.313066e-02, 'w_down': 2.282030e-02, 'g_final': 7.997697e+00}


def _to_microbatches(a, axis):
    t = _jnp.moveaxis(a, axis, 0)
    t = t.reshape((N_MICROBATCH, t.shape[0] // N_MICROBATCH) + t.shape[1:])
    return _jnp.moveaxis(t, 1, axis + 1)


def setup_inputs(seed: int = 0) -> dict:
    inp = _fwd_setup_inputs(seed)
    key = _jax.random.fold_in(_jax.random.key(seed), 7919)
    shape, _ = _output_shape()
    out = dict(inp)
    out["loss_target"] = _jax.random.normal(_jax.random.fold_in(key, 0), shape, _jnp.float32)
    for i, name in enumerate(TWIN_WEIGHTS):
        w = inp[name].astype(_jnp.float32)
        if MOMENT_SCALE is None:
            s = _jnp.sqrt(_jnp.mean(_jnp.square(w)) + 1e-30)
        else:
            s = MOMENT_SCALE[name]
        km, kv = _jax.random.split(_jax.random.fold_in(key, i + 1))
        out[name] = w
        out["m_" + name] = s * _jax.random.normal(km, w.shape, _jnp.float32)
        out["v_" + name] = (s * s) * _jax.random.uniform(kv, w.shape, _jnp.float32, 0.5, 1.5)
    if N_MICROBATCH > 1:
        for name, axis in PER_EXAMPLE_BATCH_AXIS.items():
            out[name] = _to_microbatches(out[name], axis)
    return {'x': out['x'], 'g_mix': out['g_mix'], 'w_in': out['w_in'], 'sg_ln_g': out['sg_ln_g'], 'sg_ln_b': out['sg_ln_b'], 'sg_w': out['sg_w'], 'sg_b': out['sg_b'], 'lru_conv_w': out['lru_conv_w'], 'lru_conv_b': out['lru_conv_b'], 'lru_wa': out['lru_wa'], 'lru_ba': out['lru_ba'], 'lru_wx': out['lru_wx'], 'lru_bx': out['lru_bx'], 'lru_lam': out['lru_lam'], 'p_sg': out['p_sg'], 'p_lru': out['p_lru'], 'w_out': out['w_out'], 'g_ffn': out['g_ffn'], 'w_up': out['w_up'], 'ffn_conv_w': out['ffn_conv_w'], 'ffn_conv_b': out['ffn_conv_b'], 'w_down': out['w_down'], 'g_final': out['g_final'], 'loss_target': out['loss_target'], 'm_g_mix': out['m_g_mix'], 'm_w_in': out['m_w_in'], 'm_sg_ln_g': out['m_sg_ln_g'], 'm_sg_ln_b': out['m_sg_ln_b'], 'm_sg_w': out['m_sg_w'], 'm_sg_b': out['m_sg_b'], 'm_lru_conv_w': out['m_lru_conv_w'], 'm_lru_conv_b': out['m_lru_conv_b'], 'm_lru_wa': out['m_lru_wa'], 'm_lru_ba': out['m_lru_ba'], 'm_lru_wx': out['m_lru_wx'], 'm_lru_bx': out['m_lru_bx'], 'm_lru_lam': out['m_lru_lam'], 'm_p_sg': out['m_p_sg'], 'm_p_lru': out['m_p_lru'], 'm_w_out': out['m_w_out'], 'm_g_ffn': out['m_g_ffn'], 'm_w_up': out['m_w_up'], 'm_ffn_conv_w': out['m_ffn_conv_w'], 'm_ffn_conv_b': out['m_ffn_conv_b'], 'm_w_down': out['m_w_down'], 'm_g_final': out['m_g_final'], 'v_g_mix': out['v_g_mix'], 'v_w_in': out['v_w_in'], 'v_sg_ln_g': out['v_sg_ln_g'], 'v_sg_ln_b': out['v_sg_ln_b'], 'v_sg_w': out['v_sg_w'], 'v_sg_b': out['v_sg_b'], 'v_lru_conv_w': out['v_lru_conv_w'], 'v_lru_conv_b': out['v_lru_conv_b'], 'v_lru_wa': out['v_lru_wa'], 'v_lru_ba': out['v_lru_ba'], 'v_lru_wx': out['v_lru_wx'], 'v_lru_bx': out['v_lru_bx'], 'v_lru_lam': out['v_lru_lam'], 'v_p_sg': out['v_p_sg'], 'v_p_lru': out['v_p_lru'], 'v_w_out': out['v_w_out'], 'v_g_ffn': out['v_g_ffn'], 'v_w_up': out['v_w_up'], 'v_ffn_conv_w': out['v_ffn_conv_w'], 'v_ffn_conv_b': out['v_ffn_conv_b'], 'v_w_down': out['v_w_down'], 'v_g_final': out['v_g_final']}


def _loss(weights, diff, rest, loss_target):
    with _jax.named_scope("forward"):
        args = {**rest, TWIN_DIFF_INPUT: diff, **{k: w.astype(_WEIGHT_DTYPES[k]) for k, w in weights.items()}}
        y = _forward(args)
    with _jax.named_scope("loss_head"):
        err = _jnp.square(y.astype(_jnp.float32) - loss_target)
        return 0.5 * _jnp.sum(_jnp.mean(err, axis=-1)) if err.ndim else 0.5 * err


def _adamw(w, g, m, v):
    m = ADAM_B1 * m + (1.0 - ADAM_B1) * g
    v = ADAM_B2 * v + (1.0 - ADAM_B2) * _jnp.square(g)
    m_hat = m / (1.0 - ADAM_B1 ** ADAM_STEP)
    v_hat = v / (1.0 - ADAM_B2 ** ADAM_STEP)
    delta = -ADAM_LR * (m_hat / (_jnp.sqrt(v_hat) + ADAM_EPS) + ADAM_WD * w)
    return delta, m, v


def reference(x, g_mix, w_in, sg_ln_g, sg_ln_b, sg_w, sg_b, lru_conv_w, lru_conv_b, lru_wa, lru_ba, lru_wx, lru_bx, lru_lam, p_sg, p_lru, w_out, g_ffn, w_up, ffn_conv_w, ffn_conv_b, w_down, g_final, loss_target, m_g_mix, m_w_in, m_sg_ln_g, m_sg_ln_b, m_sg_w, m_sg_b, m_lru_conv_w, m_lru_conv_b, m_lru_wa, m_lru_ba, m_lru_wx, m_lru_bx, m_lru_lam, m_p_sg, m_p_lru, m_w_out, m_g_ffn, m_w_up, m_ffn_conv_w, m_ffn_conv_b, m_w_down, m_g_final, v_g_mix, v_w_in, v_sg_ln_g, v_sg_ln_b, v_sg_w, v_sg_b, v_lru_conv_w, v_lru_conv_b, v_lru_wa, v_lru_ba, v_lru_wx, v_lru_bx, v_lru_lam, v_p_sg, v_p_lru, v_w_out, v_g_ffn, v_w_up, v_ffn_conv_w, v_ffn_conv_b, v_w_down, v_g_final):
    given = dict(x=x, g_mix=g_mix, w_in=w_in, sg_ln_g=sg_ln_g, sg_ln_b=sg_ln_b, sg_w=sg_w, sg_b=sg_b, lru_conv_w=lru_conv_w, lru_conv_b=lru_conv_b, lru_wa=lru_wa, lru_ba=lru_ba, lru_wx=lru_wx, lru_bx=lru_bx, lru_lam=lru_lam, p_sg=p_sg, p_lru=p_lru, w_out=w_out, g_ffn=g_ffn, w_up=w_up, ffn_conv_w=ffn_conv_w, ffn_conv_b=ffn_conv_b, w_down=w_down, g_final=g_final, loss_target=loss_target, m_g_mix=m_g_mix, m_w_in=m_w_in, m_sg_ln_g=m_sg_ln_g, m_sg_ln_b=m_sg_ln_b, m_sg_w=m_sg_w, m_sg_b=m_sg_b, m_lru_conv_w=m_lru_conv_w, m_lru_conv_b=m_lru_conv_b, m_lru_wa=m_lru_wa, m_lru_ba=m_lru_ba, m_lru_wx=m_lru_wx, m_lru_bx=m_lru_bx, m_lru_lam=m_lru_lam, m_p_sg=m_p_sg, m_p_lru=m_p_lru, m_w_out=m_w_out, m_g_ffn=m_g_ffn, m_w_up=m_w_up, m_ffn_conv_w=m_ffn_conv_w, m_ffn_conv_b=m_ffn_conv_b, m_w_down=m_w_down, m_g_final=m_g_final, v_g_mix=v_g_mix, v_w_in=v_w_in, v_sg_ln_g=v_sg_ln_g, v_sg_ln_b=v_sg_ln_b, v_sg_w=v_sg_w, v_sg_b=v_sg_b, v_lru_conv_w=v_lru_conv_w, v_lru_conv_b=v_lru_conv_b, v_lru_wa=v_lru_wa, v_lru_ba=v_lru_ba, v_lru_wx=v_lru_wx, v_lru_bx=v_lru_bx, v_lru_lam=v_lru_lam, v_p_sg=v_p_sg, v_p_lru=v_p_lru, v_w_out=v_w_out, v_g_ffn=v_g_ffn, v_w_up=v_w_up, v_ffn_conv_w=v_ffn_conv_w, v_ffn_conv_b=v_ffn_conv_b, v_w_down=v_w_down, v_g_final=v_g_final)
    weights = {n: given[n] for n in TWIN_WEIGHTS}
    shared = {n: given[n] for n in SHARED_INPUTS}
    per_example = {n: given[n] for n in ['x']}
    grad_fn = _jax.value_and_grad(_loss, argnums=(0, 1))

    def one_microbatch(ex, loss_target):
        ex = dict(ex)
        diff = ex.pop(TWIN_DIFF_INPUT)
        return grad_fn(weights, diff, {**shared, **ex}, loss_target)

    if N_MICROBATCH == 1:
        loss, (grad_w, grad_x) = one_microbatch(per_example, given["loss_target"])
    else:
        def body(carry, xs):
            loss_sum, grad_sum = carry
            l_k, (gw_k, gx_k) = one_microbatch(xs[0], xs[1])
            with _jax.named_scope("update"):
                return (loss_sum + l_k, _jax.tree.map(_jnp.add, grad_sum, gw_k)), gx_k

        init = (_jnp.zeros((), _jnp.float32), _jax.tree.map(_jnp.zeros_like, weights))
        (loss, grad_w), grad_x = _jax.lax.scan(body, init, (per_example, given["loss_target"]))
    with _jax.named_scope("update"):
        delta_w, new_m, new_v = {}, {}, {}
        for n in TWIN_WEIGHTS:
            delta_w[n], new_m[n], new_v[n] = _adamw(weights[n], grad_w[n], given["m_" + n], given["v_" + n])
    return (loss, grad_x, *[grad_w[n] for n in TWIN_WEIGHTS], *[delta_w[n] for n in TWIN_WEIGHTS],
            *[new_m[n] for n in TWIN_WEIGHTS], *[new_v[n] for n in TWIN_WEIGHTS])
```

```python
import math

import jax
import jax.numpy as jnp
from jax import lax
from jax.experimental import pallas as pl
from jax.experimental.pallas import tpu as pltpu

F32 = jnp.float32
BF16 = jnp.bfloat16
MESH = pl.DeviceIdType.MESH
ANY = pl.BlockSpec(memory_space=pl.ANY)

EPS = 1e-6
LRU_C = 8.0
ADAM_LR = 0.001
ADAM_B1 = 0.9
ADAM_B2 = 0.999
ADAM_EPS = 1e-08
ADAM_WD = 0.01
ADAM_STEP = 10

N_CHIPS = 4
SUBLANES = 8
BF16_ROWS = 16
LANES = 128
VMEM_LIMIT = 56 * 1024 * 1024
GELU_C = math.sqrt(2.0 / math.pi)
GELU_K = 0.044715


def _cp(*sem):
    return pltpu.CompilerParams(dimension_semantics=sem, vmem_limit_bytes=VMEM_LIMIT)


def _blk(dim, pref):
    if dim <= pref:
        return dim
    b = pref
    while dim % b:
        b //= 2
    return b


def _gelu(x):
    t = jnp.tanh(GELU_C * (x + GELU_K * x * x * x))
    return 0.5 * x * (1.0 + t)


def _gelu_and_grad(x):
    x2 = x * x
    t = jnp.tanh(GELU_C * (x + GELU_K * x * x2))
    g = 0.5 * x * (1.0 + t)
    dg = 0.5 * (1.0 + t) + 0.5 * x * (1.0 - t * t) * (GELU_C * (1.0 + 3.0 * GELU_K * x2))
    return g, dg


def _sigmoid(x):
    return 1.0 / (1.0 + jnp.exp(-x))


def _softplus(x):
    e = jnp.exp(-jnp.abs(x))
    series = e * (1.0 - e * (0.5 - e * (1.0 / 3.0 - e * (0.25 - e * 0.2))))
    return jnp.where(e < 0.01, series, jnp.log(1.0 + e)) + jnp.maximum(x, 0.0)


def _neg_expm1(x):
    series = -(x * (1.0 + x * (0.5 + x * (1.0 / 6.0 + x * (1.0 / 24.0)))))
    return jnp.where(x > -0.01, series, 1.0 - jnp.exp(x))


def _mesh_pos():
    return lax.axis_index("x"), lax.axis_index("y"), lax.axis_index("c")


def _mm_nn(a, b, *, out_dtype, name, res=None):
    M, K = a.shape
    cs = b.shape[-1]
    N = cs * (b.shape[0] if b.ndim == 3 else 1)
    tm, tn, tk = _blk(M, 1024), _blk(cs, 1024), _blk(K, 2048)
    nbs, nk = cs // tn, K // tk
    if b.ndim == 3:
        b_spec = pl.BlockSpec((None, tk, tn), lambda i, j, k: (j // nbs, k, j % nbs))
    else:
        b_spec = pl.BlockSpec((tk, tn), lambda i, j, k: (k, j))
    in_specs = [pl.BlockSpec((tm, tk), lambda i, j, k: (i, k)), b_spec]
    args = [a, b]
    if res is not None:
        in_specs.append(pl.BlockSpec((tm, tn), lambda i, j, k: (i, j)))
        args.append(res)

    def body(*refs):
        a_ref, b_ref = refs[0], refs[1]
        r_ref = refs[2] if res is not None else None
        o_ref, acc = refs[-2], refs[-1]
        k = pl.program_id(2)
        p = jnp.dot(a_ref[...], b_ref[...], preferred_element_type=F32)

        @pl.when(k == 0)
        def _():
            acc[...] = p

        @pl.when(k > 0)
        def _():
            acc[...] += p

        @pl.when(k == nk - 1)
        def _():
            r = acc[...]
            if r_ref is not None:
                r = r + r_ref[...]
            o_ref[...] = r.astype(out_dtype)

    return pl.pallas_call(
        body, name=name, grid=(M // tm, N // tn, nk), in_specs=in_specs,
        out_specs=pl.BlockSpec((tm, tn), lambda i, j, k: (i, j)),
        out_shape=jax.ShapeDtypeStruct((M, N), out_dtype),
        scratch_shapes=[pltpu.VMEM((tm, tn), F32)],
        compiler_params=_cp("parallel", "parallel", "arbitrary"),
    )(*args)


def _mm_nt(a, b, *, out_dtype, name):
    M, Kc = a.shape
    cs = b.shape[-1]
    N = b.shape[-2]
    tm, tn, tk = _blk(M, 1024), _blk(N, 1024), _blk(cs, 2048)
    nks, nk = cs // tk, Kc // tk
    if b.ndim == 3:
        b_spec = pl.BlockSpec((None, tn, tk), lambda i, j, k: (k // nks, j, k % nks))
    else:
        b_spec = pl.BlockSpec((tn, tk), lambda i, j, k: (j, k))

    def body(a_ref, b_ref, o_ref, acc):
        k = pl.program_id(2)
        p = lax.dot_general(a_ref[...], b_ref[...], (((1,), (1,)), ((), ())), preferred_element_type=F32)

        @pl.when(k == 0)
        def _():
            acc[...] = p

        @pl.when(k > 0)
        def _():
            acc[...] += p

        @pl.when(k == nk - 1)
        def _():
            o_ref[...] = acc[...].astype(out_dtype)

    return pl.pallas_call(
        body, name=name, grid=(M // tm, N // tn, nk),
        in_specs=[pl.BlockSpec((tm, tk), lambda i, j, k: (i, k)), b_spec],
        out_specs=pl.BlockSpec((tm, tn), lambda i, j, k: (i, j)),
        out_shape=jax.ShapeDtypeStruct((M, N), out_dtype),
        scratch_shapes=[pltpu.VMEM((tm, tn), F32)],
        compiler_params=_cp("parallel", "parallel", "arbitrary"),
    )(a, b)


def _mm_tn(a, b, *, out_dtype, name, col_shards=None):
    T, K1 = a.shape
    N = b.shape[1]
    cs = N // col_shards if col_shards else N
    tm, tn, tk = _blk(K1, 1024), _blk(cs, 1024), _blk(T, 2048)
    nbs, nk = cs // tn, T // tk
    if col_shards:
        o_spec = pl.BlockSpec((None, tm, tn), lambda i, j, k: (j // nbs, i, j % nbs))
        o_shape = jax.ShapeDtypeStruct((col_shards, K1, cs), out_dtype)
    else:
        o_spec = pl.BlockSpec((tm, tn), lambda i, j, k: (i, j))
        o_shape = jax.ShapeDtypeStruct((K1, N), out_dtype)

    def body(a_ref, b_ref, o_ref, acc):
        k = pl.program_id(2)
        p = lax.dot_general(a_ref[...], b_ref[...], (((0,), (0,)), ((), ())), preferred_element_type=F32)

        @pl.when(k == 0)
        def _():
            acc[...] = p

        @pl.when(k > 0)
        def _():
            acc[...] += p

        @pl.when(k == nk - 1)
        def _():
            o_ref[...] = acc[...].astype(out_dtype)

    return pl.pallas_call(
        body, name=name, grid=(K1 // tm, N // tn, nk),
        in_specs=[pl.BlockSpec((tk, tm), lambda i, j, k: (k, i)), pl.BlockSpec((tk, tn), lambda i, j, k: (k, j))],
        out_specs=o_spec, out_shape=o_shape,
        scratch_shapes=[pltpu.VMEM((tm, tn), F32)],
        compiler_params=_cp("parallel", "parallel", "arbitrary"),
    )(a, b)


def _rms_fwd(x, g, *, name):
    T, D = x.shape
    tm = _blk(T, 256)

    def body(x_ref, g_ref, o_ref):
        xv = x_ref[...]
        r = lax.rsqrt(jnp.mean(xv * xv, axis=-1, keepdims=True) + EPS)
        o_ref[...] = (xv * r * g_ref[...]).astype(BF16)

    return pl.pallas_call(
        body, name=name, grid=(T // tm,),
        in_specs=[pl.BlockSpec((tm, D), lambda i: (i, 0)), pl.BlockSpec((1, D), lambda i: (0, 0))],
        out_specs=pl.BlockSpec((tm, D), lambda i: (i, 0)),
        out_shape=jax.ShapeDtypeStruct((T, D), BF16),
        compiler_params=_cp("parallel"),
    )(x, g)


def _rms_bwd(x, g, dh, dres, *, name):
    T, D = x.shape
    tm = _blk(T, 256)

    def body(x_ref, g_ref, dh_ref, dres_ref, dx_ref, dxb_ref, dg_ref):
        i = pl.program_id(0)
        xv = x_ref[...]
        r = lax.rsqrt(jnp.mean(xv * xv, axis=-1, keepdims=True) + EPS)
        n = xv * r
        dh_v = dh_ref[...]
        dn = dh_v * g_ref[...]
        dx = dres_ref[...] + r * (dn - n * jnp.mean(dn * n, axis=-1, keepdims=True))
        dx_ref[...] = dx
        dxb_ref[...] = dx.astype(BF16)
        part = jnp.sum(dh_v * n, axis=0, keepdims=True)

        @pl.when(i == 0)
        def _():
            dg_ref[...] = part

        @pl.when(i > 0)
        def _():
            dg_ref[...] += part

    row = pl.BlockSpec((tm, D), lambda i: (i, 0))
    vec = pl.BlockSpec((1, D), lambda i: (0, 0))
    return pl.pallas_call(
        body, name=name, grid=(T // tm,),
        in_specs=[row, vec, row, row], out_specs=[row, row, vec],
        out_shape=[jax.ShapeDtypeStruct((T, D), F32), jax.ShapeDtypeStruct((T, D), BF16),
                   jax.ShapeDtypeStruct((1, D), F32)],
        compiler_params=_cp("arbitrary"),
    )(x, g, dh, dres)


def _loss_head(x2, tgt, g, *, name):
    T, D = x2.shape
    tm = _blk(T, 256)

    def body(x_ref, t_ref, g_ref, l_ref, dx_ref, dxb_ref, dg_ref):
        i = pl.program_id(0)
        xv = x_ref[...]
        gv = g_ref[...]
        r = lax.rsqrt(jnp.mean(xv * xv, axis=-1, keepdims=True) + EPS)
        n = xv * r
        err = n * gv - t_ref[...]
        dy = err * (1.0 / D)
        dn = dy * gv
        dx = r * (dn - n * jnp.mean(dn * n, axis=-1, keepdims=True))
        dx_ref[...] = dx
        dxb_ref[...] = dx.astype(BF16)
        lpart = jnp.sum(err * err, axis=0, keepdims=True)
        gpart = jnp.sum(dy * n, axis=0, keepdims=True)

        @pl.when(i == 0)
        def _():
            l_ref[...] = lpart
            dg_ref[...] = gpart

        @pl.when(i > 0)
        def _():
            l_ref[...] += lpart
            dg_ref[...] += gpart

    row = pl.BlockSpec((tm, D), lambda i: (i, 0))
    vec = pl.BlockSpec((1, D), lambda i: (0, 0))
    return pl.pallas_call(
        body, name=name, grid=(T // tm,),
        in_specs=[row, row, vec], out_specs=[vec, row, row, vec],
        out_shape=[jax.ShapeDtypeStruct((1, D), F32), jax.ShapeDtypeStruct((T, D), F32),
                   jax.ShapeDtypeStruct((T, D), BF16), jax.ShapeDtypeStruct((1, D), F32)],
        compiler_params=_cp("arbitrary"),
    )(x2, tgt, g)


def _merge_fwd(proj, pa, pb, *, off_a, name):
    T, D = pa.shape
    tm, tn = _blk(T, 256), _blk(D, 1024)
    oa, ob = off_a // tn, (off_a + D) // tn

    def body(ga_ref, gb_ref, pa_ref, pb_ref, o_ref):
        o_ref[...] = (_sigmoid(ga_ref[...]) * pa_ref[...] + _sigmoid(gb_ref[...]) * pb_ref[...]).astype(BF16)

    blk = pl.BlockSpec((tm, tn), lambda i, j: (i, j))
    return pl.pallas_call(
        body, name=name, grid=(T // tm, D // tn),
        in_specs=[pl.BlockSpec((tm, tn), lambda i, j: (i, oa + j)),
                  pl.BlockSpec((tm, tn), lambda i, j: (i, ob + j)), blk, blk],
        out_specs=blk, out_shape=jax.ShapeDtypeStruct((T, D), BF16),
        compiler_params=_cp("parallel", "parallel"),
    )(proj, proj, pa, pb)


def _merge_bwd(proj, pa, pb, dm, *, off_a, name):
    T, D = pa.shape
    tm, tn = _blk(T, 256), _blk(D, 1024)
    oa, ob = off_a // tn, (off_a + D) // tn

    def body(ga_ref, gb_ref, pa_ref, pb_ref, dm_ref, dga_ref, dgb_ref, dpa_ref, dpb_ref):
        dmv = dm_ref[...]
        sa = _sigmoid(ga_ref[...])
        sb = _sigmoid(gb_ref[...])
        dga_ref[...] = (dmv * pa_ref[...] * sa * (1.0 - sa)).astype(BF16)
        dgb_ref[...] = (dmv * pb_ref[...] * sb * (1.0 - sb)).astype(BF16)
        dpa_ref[...] = (dmv * sa).astype(BF16)
        dpb_ref[...] = (dmv * sb).astype(BF16)

    blk = pl.BlockSpec((tm, tn), lambda i, j: (i, j))
    out = jax.ShapeDtypeStruct((T, D), BF16)
    return pl.pallas_call(
        body, name=name, grid=(T // tm, D // tn),
        in_specs=[pl.BlockSpec((tm, tn), lambda i, j: (i, oa + j)),
                  pl.BlockSpec((tm, tn), lambda i, j: (i, ob + j)), blk, blk, blk],
        out_specs=[blk] * 4, out_shape=[out] * 4,
        compiler_params=_cp("parallel", "parallel"),
    )(proj, proj, pa, pb, dm)


def _sg_mask(sg_w, *, name):
    G, C, _ = sg_w.shape

    def body(w_ref, m_ref, mt_ref):
        row = lax.broadcasted_iota(jnp.int32, (C, C), 0)
        col = lax.broadcasted_iota(jnp.int32, (C, C), 1)
        for g in range(G):
            w = jnp.where(row >= col, w_ref[g], 0.0)
            m_ref[g] = w.astype(BF16)
            mt_ref[g] = w.T.astype(BF16)

    out = jax.ShapeDtypeStruct((G, C, C), BF16)
    return pl.pallas_call(body, name=name, out_shape=[out, out])(sg_w)


def _sg_layernorm(zv, lg, lb):
    v = _gelu(zv)
    mu = jnp.mean(v, axis=-1, keepdims=True)
    xc = v - mu
    rstd = lax.rsqrt(jnp.mean(xc * xc, axis=-1, keepdims=True) + EPS)
    vhat = xc * rstd
    return vhat, rstd, vhat * lg + lb


def _sg_fwd(proj, lg, lb, wm, bt, *, name):
    T = proj.shape[0]
    G, C, _ = wm.shape
    W = lg.shape[-1]
    gd = W // G

    def body(zu_ref, zv_ref, lg_ref, lb_ref, wm_ref, bt_ref, ya_ref, vn_scr):
        _, _, vn = _sg_layernorm(zv_ref[...], lg_ref[...], lb_ref[...])
        vn_scr[...] = vn.astype(BF16)
        for g in range(G):
            cols = slice(g * gd, (g + 1) * gd)
            mixed = jnp.dot(wm_ref[g], vn_scr[:, cols], preferred_element_type=F32) + bt_ref[:, g:g + 1]
            ya_ref[:, cols] = (_gelu(zu_ref[:, cols]) * mixed).astype(BF16)

    vec = pl.BlockSpec((1, W), lambda i: (0, 0))
    return pl.pallas_call(
        body, name=name, grid=(T // C,),
        in_specs=[pl.BlockSpec((C, W), lambda i: (i, 0)), pl.BlockSpec((C, W), lambda i: (i, 1)), vec, vec,
                  pl.BlockSpec((G, C, C), lambda i: (0, 0, 0)), pl.BlockSpec((C, G), lambda i: (0, 0))],
        out_specs=pl.BlockSpec((C, W), lambda i: (i, 0)),
        out_shape=jax.ShapeDtypeStruct((T, W), BF16),
        scratch_shapes=[pltpu.VMEM((C, W), BF16)],
        compiler_params=_cp("parallel"),
    )(proj, proj, lg, lb, wm, bt)


def _sg_bwd(proj, dya, lg, lb, wm, wmt, bt, *, name):
    T = proj.shape[0]
    G, C, _ = wm.shape
    W = lg.shape[-1]
    gd = W // G
    n_steps = T // C

    def body(zu_ref, zv_ref, dya_ref, lg_ref, lb_ref, wm_ref, wmt_ref, bt_ref,
             dz_ref, dwm_ref, dbt_ref, dlg_ref, dlb_ref, vn_scr, dvn_scr):
        i = pl.program_id(0)

        @pl.when(i == 0)
        def _():
            dwm_ref[...] = jnp.zeros_like(dwm_ref)
            dbt_ref[...] = jnp.zeros_like(dbt_ref)
            dlg_ref[...] = jnp.zeros_like(dlg_ref)
            dlb_ref[...] = jnp.zeros_like(dlb_ref)

        lgv = lg_ref[...]
        vhat, rstd, vn = _sg_layernorm(zv_ref[...], lgv, lb_ref[...])
        vn_scr[...] = vn.astype(BF16)
        for g in range(G):
            cols = slice(g * gd, (g + 1) * gd)
            vnb = vn_scr[:, cols]
            mixed = jnp.dot(wm_ref[g], vnb, preferred_element_type=F32) + bt_ref[:, g:g + 1]
            u, du = _gelu_and_grad(zu_ref[:, cols])
            dy = dya_ref[:, cols]
            dz_ref[:, cols] = (dy * mixed * du).astype(BF16)
            dmix = dy * u
            dmb = dmix.astype(BF16)
            dbt_ref[:, g:g + 1] += jnp.sum(dmix, axis=1, keepdims=True)
            dwm_ref[g] += lax.dot_general(dmb, vnb, (((1,), (1,)), ((), ())), preferred_element_type=F32)
            dvn_scr[:, cols] = jnp.dot(wmt_ref[g], dmb, preferred_element_type=F32)
        dvn = dvn_scr[...]
        dlg_ref[...] += jnp.sum(dvn * vhat, axis=0, keepdims=True)
        dlb_ref[...] += jnp.sum(dvn, axis=0, keepdims=True)
        dvh = dvn * lgv
        dv = rstd * (dvh - jnp.mean(dvh, axis=-1, keepdims=True)
                     - vhat * jnp.mean(dvh * vhat, axis=-1, keepdims=True))
        _, dgv = _gelu_and_grad(zv_ref[...])
        dz_ref[:, W:] = (dv * dgv).astype(BF16)

        @pl.when(i == n_steps - 1)
        def _():
            row = lax.broadcasted_iota(jnp.int32, (C, C), 0)
            col = lax.broadcasted_iota(jnp.int32, (C, C), 1)
            for g in range(G):
                dwm_ref[g] = jnp.where(row >= col, dwm_ref[g], 0.0)

    vec = pl.BlockSpec((1, W), lambda i: (0, 0))
    mat = pl.BlockSpec((G, C, C), lambda i: (0, 0, 0))
    bts = pl.BlockSpec((C, G), lambda i: (0, 0))
    return pl.pallas_call(
        body, name=name, grid=(n_steps,),
        in_specs=[pl.BlockSpec((C, W), lambda i: (i, 0)), pl.BlockSpec((C, W), lambda i: (i, 1)),
                  pl.BlockSpec((C, W), lambda i: (i, 0)), vec, vec, mat, mat, bts],
        out_specs=[pl.BlockSpec((C, 2 * W), lambda i: (i, 0)), mat, bts, vec, vec],
        out_shape=[jax.ShapeDtypeStruct((T, 2 * W), BF16), jax.ShapeDtypeStruct((G, C, C), F32),
                   jax.ShapeDtypeStruct((C, G), F32), jax.ShapeDtypeStruct((1, W), F32),
                   jax.ShapeDtypeStruct((1, W), F32)],
        scratch_shapes=[pltpu.VMEM((C, W), BF16), pltpu.VMEM((C, W), F32)],
        compiler_params=_cp("arbitrary"),
    )(proj, proj, dya, lg, lb, wm, wmt, bt)


def _rows_with_prev(ref, r0, rows, ci):
    p0 = pl.multiple_of(jnp.maximum(r0 - SUBLANES, 0), SUBLANES)
    prev = jnp.where(ci > 0, ref[pl.ds(p0, SUBLANES), :], 0.0)
    return jnp.concatenate([prev, ref[pl.ds(r0, rows), :]], axis=0)


def _rows_with_next(ref, r0, rows, ci, n_chunks, total):
    n0 = pl.multiple_of(jnp.minimum(r0 + rows, total - SUBLANES), SUBLANES)
    nxt = jnp.where(ci < n_chunks - 1, ref[pl.ds(n0, SUBLANES), :], 0.0)
    return jnp.concatenate([ref[pl.ds(r0, rows), :], nxt], axis=0)


def _delayed(xx, k, rows):
    if k == 0:
        return xx[SUBLANES:, :]
    return pltpu.roll(xx, k, 0)[SUBLANES:, :]


def _advanced(xx, k, rows):
    if k == 0:
        return xx[:rows, :]
    return pltpu.roll(xx, rows + SUBLANES - k, 0)[:rows, :]


def _conv_chunk(x_ref, w_ref, b_ref, r0, rows, ci):
    K = w_ref.shape[0]
    xx = _rows_with_prev(x_ref, r0, rows, ci)
    out = _delayed(xx, K - 1, rows) * w_ref[0:1, :]
    for k in range(1, K):
        out = out + _delayed(xx, K - 1 - k, rows) * w_ref[k:k + 1, :]
    return out + b_ref[...]


def _ffn_act_fwd(up0, cw, cb, *, batch, name):
    T, F2 = up0.shape
    F = F2 // 2
    S = T // batch
    K = cw.shape[0]
    cbk = _blk(F, 512)
    nj = F // cbk
    R = min(64, S // 2)
    n_chunks = S // R

    def body(ug_ref, uv_ref, wg_ref, wv_ref, bg_ref, bv_ref, act_ref):
        def chunk(ci, carry):
            r0 = pl.multiple_of(ci * R, R)
            cg = _conv_chunk(ug_ref, wg_ref, bg_ref, r0, R, ci)
            cv = _conv_chunk(uv_ref, wv_ref, bv_ref, r0, R, ci)
            act_ref[pl.ds(r0, R), :] = (_gelu(cg) * cv).astype(BF16)
            return carry

        lax.fori_loop(0, n_chunks, chunk, 0)

    return pl.pallas_call(
        body, name=name, grid=(nj, batch),
        in_specs=[pl.BlockSpec((S, cbk), lambda j, b: (b, j)), pl.BlockSpec((S, cbk), lambda j, b: (b, nj + j)),
                  pl.BlockSpec((K, cbk), lambda j, b: (0, j)), pl.BlockSpec((K, cbk), lambda j, b: (0, nj + j)),
                  pl.BlockSpec((1, cbk), lambda j, b: (0, j)), pl.BlockSpec((1, cbk), lambda j, b: (0, nj + j))],
        out_specs=pl.BlockSpec((S, cbk), lambda j, b: (b, j)),
        out_shape=jax.ShapeDtypeStruct((T, F), BF16),
        compiler_params=_cp("parallel", "parallel"),
    )(up0, up0, cw, cw, cb, cb)


def _ffn_act_bwd(up0, cw, cb, dact, *, batch, name):
    T, F2 = up0.shape
    F = F2 // 2
    S = T // batch
    K = cw.shape[0]
    cbk = _blk(F, 512)
    nj = F // cbk
    R = min(64, S // 2)
    n_chunks = S // R

    def body(ug_ref, uv_ref, wg_ref, wv_ref, bg_ref, bv_ref, da_ref,
             dug_ref, duv_ref, dw_g_ref, dw_v_ref, db_g_ref, db_v_ref, dcg_scr, dcv_scr):
        b = pl.program_id(1)

        def chunk_a(ci, acc):
            r0 = pl.multiple_of(ci * R, R)
            xg = _rows_with_prev(ug_ref, r0, R, ci)
            xv = _rows_with_prev(uv_ref, r0, R, ci)
            dg_taps = [_delayed(xg, K - 1 - k, R) for k in range(K)]
            dv_taps = [_delayed(xv, K - 1 - k, R) for k in range(K)]
            cg = dg_taps[0] * wg_ref[0:1, :]
            cv = dv_taps[0] * wv_ref[0:1, :]
            for k in range(1, K):
                cg = cg + dg_taps[k] * wg_ref[k:k + 1, :]
                cv = cv + dv_taps[k] * wv_ref[k:k + 1, :]
            cg = cg + bg_ref[...]
            cv = cv + bv_ref[...]
            gl, dgl = _gelu_and_grad(cg)
            da = da_ref[pl.ds(r0, R), :]
            dcg = da * cv * dgl
            dcv = da * gl
            dcg_scr[pl.ds(r0, R), :] = dcg
            dcv_scr[pl.ds(r0, R), :] = dcv
            new = []
            for k in range(K):
                new.append(acc[k] + jnp.sum(dcg * dg_taps[k], axis=0, keepdims=True))
            for k in range(K):
                new.append(acc[K + k] + jnp.sum(dcv * dv_taps[k], axis=0, keepdims=True))
            new.append(acc[2 * K] + jnp.sum(dcg, axis=0, keepdims=True))
            new.append(acc[2 * K + 1] + jnp.sum(dcv, axis=0, keepdims=True))
            return tuple(new)

        zero = jnp.zeros((1, cbk), F32)
        acc = lax.fori_loop(0, n_chunks, chunk_a, (zero,) * (2 * K + 2))

        def chunk_b(ci, carry):
            r0 = pl.multiple_of(ci * R, R)
            dg = _rows_with_next(dcg_scr, r0, R, ci, n_chunks, S)
            dv = _rows_with_next(dcv_scr, r0, R, ci, n_chunks, S)
            og = _advanced(dg, 0, R) * wg_ref[K - 1:K, :]
            ov = _advanced(dv, 0, R) * wv_ref[K - 1:K, :]
            for j in range(1, K):
                og = og + _advanced(dg, j, R) * wg_ref[K - 1 - j:K - j, :]
                ov = ov + _advanced(dv, j, R) * wv_ref[K - 1 - j:K - j, :]
            dug_ref[pl.ds(r0, R), :] = og.astype(BF16)
            duv_ref[pl.ds(r0, R), :] = ov.astype(BF16)
            return carry

        lax.fori_loop(0, n_chunks, chunk_b, 0)

        @pl.when(b == 0)
        def _():
            for k in range(K):
                dw_g_ref[k:k + 1, :] = acc[k]
                dw_v_ref[k:k + 1, :] = acc[K + k]
            db_g_ref[...] = acc[2 * K]
            db_v_ref[...] = acc[2 * K + 1]

        @pl.when(b > 0)
        def _():
            for k in range(K):
                dw_g_ref[k:k + 1, :] += acc[k]
                dw_v_ref[k:k + 1, :] += acc[K + k]
            db_g_ref[...] += acc[2 * K]
            db_v_ref[...] += acc[2 * K + 1]

    seq = pl.BlockSpec((S, cbk), lambda j, b: (b, j))
    wk = pl.BlockSpec((K, cbk), lambda j, b: (0, j))
    w1 = pl.BlockSpec((1, cbk), lambda j, b: (0, j))
    outs = pl.pallas_call(
        body, name=name, grid=(nj, batch),
        in_specs=[seq, pl.BlockSpec((S, cbk), lambda j, b: (b, nj + j)),
                  wk, pl.BlockSpec((K, cbk), lambda j, b: (0, nj + j)),
                  w1, pl.BlockSpec((1, cbk), lambda j, b: (0, nj + j)), seq],
        out_specs=[seq, seq, wk, wk, w1, w1],
        out_shape=[jax.ShapeDtypeStruct((T, F), BF16), jax.ShapeDtypeStruct((T, F), BF16),
                   jax.ShapeDtypeStruct((K, F), F32), jax.ShapeDtypeStruct((K, F), F32),
                   jax.ShapeDtypeStruct((1, F), F32), jax.ShapeDtypeStruct((1, F), F32)],
        scratch_shapes=[pltpu.VMEM((S, cbk), F32), pltpu.VMEM((S, cbk), F32)],
        compiler_params=_cp("parallel", "arbitrary"),
    )(up0, up0, cw, cw, cb, cb, dact)
    dug, duv, dwg, dwv, dbg, dbv = outs
    return dug, duv, jnp.concatenate([dwg, dwv], axis=1), jnp.concatenate([dbg, dbv], axis=1)


def _lru_gate_rows(xr_ref, cw_ref, cb_ref, wa_ref, wx_ref, ba_ref, bx_ref, xc_scr, za_scr, zx_scr, S, R):
    def chunk(ci, carry):
        r0 = pl.multiple_of(ci * R, R)
        xc = _conv_chunk(xr_ref, cw_ref, cb_ref, r0, R, ci)
        xc_scr[pl.ds(r0, R), :] = xc
        xb = xc.astype(BF16)
        za_scr[pl.ds(r0, R), :] = jnp.dot(xb, wa_ref[...], preferred_element_type=F32) + ba_ref[...]
        zx_scr[pl.ds(r0, R), :] = jnp.dot(xb, wx_ref[...], preferred_element_type=F32) + bx_ref[...]
        return carry

    lax.fori_loop(0, S // R, chunk, 0)


def _lru_gates(za, zx, sp):
    ra = _sigmoid(za)
    ig = _sigmoid(zx)
    la = -LRU_C * ra * sp
    a = jnp.exp(la)
    s = jnp.sqrt(_neg_expm1(2.0 * la))
    return ra, ig, a, s


def _lru_fwd(proj, cw, cb, wa, wx, ba, bx, lam, *, batch, off_x, name):
    T = proj.shape[0]
    H, Dh, _ = wa.shape
    W = H * Dh
    S = T // batch
    K = cw.shape[0]
    ox, oy = off_x // Dh, (off_x + W) // Dh
    R = min(256, S // 2)
    n16 = S // BF16_ROWS

    def body(xr_ref, yr_ref, cw_ref, cb_ref, wa_ref, wx_ref, ba_ref, bx_ref, lam_ref,
             yb_ref, h_ref, xc_scr, za_scr, zx_scr):
        _lru_gate_rows(xr_ref, cw_ref, cb_ref, wa_ref, wx_ref, ba_ref, bx_ref, xc_scr, za_scr, zx_scr, S, R)
        sp = _softplus(-lam_ref[...])
        row = lax.broadcasted_iota(jnp.int32, (SUBLANES, Dh), 0)

        def tile(r0, carry):
            rows = pl.ds(r0, SUBLANES)
            xc = xc_scr[rows, :]
            _, ig, a, s = _lru_gates(za_scr[rows, :], zx_scr[rows, :], sp)
            A, B = a, s * (ig * xc)
            for d in (1, 2, 4):
                m = row >= d
                Bs = pltpu.roll(B, d, 0)
                As = pltpu.roll(A, d, 0)
                B = jnp.where(m, B + A * Bs, B)
                A = jnp.where(m, A * As, A)
            hh = B + A * carry
            h_ref[rows, :] = hh
            return hh, hh[SUBLANES - 1:SUBLANES, :]

        def step(i, carry):
            r0 = pl.multiple_of(i * BF16_ROWS, BF16_ROWS)
            h0, carry = tile(r0, carry)
            h1, carry = tile(r0 + SUBLANES, carry)
            hh = jnp.concatenate([h0, h1], axis=0)
            yb_ref[pl.ds(r0, BF16_ROWS), :] = (hh * _gelu(yr_ref[pl.ds(r0, BF16_ROWS), :])).astype(BF16)
            return carry

        lax.fori_loop(0, n16, step, jnp.zeros((1, Dh), F32))

    vec = pl.BlockSpec((1, Dh), lambda b, h: (0, h))
    wsp = pl.BlockSpec((None, Dh, Dh), lambda b, h: (h, 0, 0))
    seq = pl.BlockSpec((S, Dh), lambda b, h: (b, h))
    return pl.pallas_call(
        body, name=name, grid=(batch, H),
        in_specs=[pl.BlockSpec((S, Dh), lambda b, h: (b, ox + h)), pl.BlockSpec((S, Dh), lambda b, h: (b, oy + h)),
                  pl.BlockSpec((K, Dh), lambda b, h: (0, h)), vec, wsp, wsp, vec, vec, vec],
        out_specs=[seq, seq],
        out_shape=[jax.ShapeDtypeStruct((T, W), BF16), jax.ShapeDtypeStruct((T, W), F32)],
        scratch_shapes=[pltpu.VMEM((S, Dh), F32)] * 3,
        compiler_params=_cp("parallel", "parallel"),
    )(proj, proj, cw, cb, wa, wx, ba, bx, lam)


def _lru_bwd(proj, hseq, dyb, cw, cb, wa, wx, wat, wxt, ba, bx, lam, *, batch, off_x, name):
    T = proj.shape[0]
    H, Dh, _ = wa.shape
    W = H * Dh
    S = T // batch
    K = cw.shape[0]
    ox, oy = off_x // Dh, (off_x + W) // Dh
    R = min(256, S // 2)
    n_chunks = S // R
    n16 = S // BF16_ROWS

    def body(xr_ref, yr_ref, h_ref, dyb_ref, cw_ref, cb_ref, wa_ref, wx_ref, wat_ref, wxt_ref,
             ba_ref, bx_ref, lam_ref,
             dxr_ref, dyr_ref, dwa_ref, dwx_ref, dcw_ref, dcb_ref, dba_ref, dbx_ref, dlam_ref,
             xc_scr, za_scr, zx_scr, dza_scr, dzx_scr, dxc_scr):
        b = pl.program_id(1)
        _lru_gate_rows(xr_ref, cw_ref, cb_ref, wa_ref, wx_ref, ba_ref, bx_ref, xc_scr, za_scr, zx_scr, S, R)
        lam_v = lam_ref[...]
        sp = _softplus(-lam_v)
        row = lax.broadcasted_iota(jnp.int32, (SUBLANES, Dh), 0)

        def tile(r0, carry):
            a_next, g_next, s_ba, s_bx, s_lam = carry
            rows = pl.ds(r0, SUBLANES)
            xc = xc_scr[rows, :]
            ra, ig, a, s = _lru_gates(za_scr[rows, :], zx_scr[rows, :], sp)
            hh = h_ref[rows, :]
            gy, dgy = _gelu_and_grad(yr_ref[rows, :])
            dy = dyb_ref[rows, :]
            dyr = dy * hh * dgy
            C = jnp.where(row == SUBLANES - 1, a_next, pltpu.roll(a, SUBLANES - 1, 0))
            B = dy * gy
            for d in (1, 2, 4):
                m = row < SUBLANES - d
                Bs = pltpu.roll(B, SUBLANES - d, 0)
                Cs = pltpu.roll(C, SUBLANES - d, 0)
                B = jnp.where(m, B + C * Bs, B)
                C = jnp.where(m, C * Cs, C)
            G = B + C * g_next
            p0 = pl.multiple_of(jnp.maximum(r0 - SUBLANES, 0), SUBLANES)
            h_before = jnp.where(r0 > 0, h_ref[pl.ds(p0, SUBLANES), :][SUBLANES - 1:SUBLANES, :], 0.0)
            h_prev = jnp.where(row == 0, h_before, pltpu.roll(hh, 1, 0))
            da = G * h_prev
            dig = G * s * xc
            ds = G * ig * xc
            dxc_scr[rows, :] = G * s * ig
            dla = da * a - ds * (a * a) / s
            dza = dla * (-LRU_C * sp) * ra * (1.0 - ra)
            dzx = dig * ig * (1.0 - ig)
            dza_scr[rows, :] = dza
            dzx_scr[rows, :] = dzx
            carry = (a[0:1, :], G[0:1, :], s_ba + dza, s_bx + dzx, s_lam + dla * ra)
            return dyr, carry

        def step(it, carry):
            r0 = pl.multiple_of((n16 - 1 - it) * BF16_ROWS, BF16_ROWS)
            d1, carry = tile(r0 + SUBLANES, carry)
            d0, carry = tile(r0, carry)
            dyr_ref[pl.ds(r0, BF16_ROWS), :] = jnp.concatenate([d0, d1], axis=0).astype(BF16)
            return carry

        z1 = jnp.zeros((1, Dh), F32)
        z8 = jnp.zeros((SUBLANES, Dh), F32)
        _, _, s_ba, s_bx, s_lam = lax.fori_loop(0, n16, step, (z1, z1, z8, z8, z8))
        dba = jnp.sum(s_ba, axis=0, keepdims=True)
        dbx = jnp.sum(s_bx, axis=0, keepdims=True)
        dlam = jnp.sum(s_lam, axis=0, keepdims=True) * (LRU_C * _sigmoid(-lam_v))

        @pl.when(b == 0)
        def _():
            dwa_ref[...] = jnp.zeros_like(dwa_ref)
            dwx_ref[...] = jnp.zeros_like(dwx_ref)

        def chunk_c(ci, carry):
            r0 = pl.multiple_of(ci * R, R)
            rows = pl.ds(r0, R)
            xb = xc_scr[rows, :].astype(BF16)
            dzab = dza_scr[rows, :].astype(BF16)
            dzxb = dzx_scr[rows, :].astype(BF16)
            dwa_ref[...] += lax.dot_general(xb, dzab, (((0,), (0,)), ((), ())), preferred_element_type=F32)
            dwx_ref[...] += lax.dot_general(xb, dzxb, (((0,), (0,)), ((), ())), preferred_element_type=F32)
            dxc_scr[rows, :] += (jnp.dot(dzab, wat_ref[...], preferred_element_type=F32)
                                 + jnp.dot(dzxb, wxt_ref[...], preferred_element_type=F32))
            return carry

        lax.fori_loop(0, n_chunks, chunk_c, 0)

        def chunk_d(ci, acc):
            r0 = pl.multiple_of(ci * R, R)
            dd = _rows_with_next(dxc_scr, r0, R, ci, n_chunks, S)
            xx = _rows_with_prev(xr_ref, r0, R, ci)
            dxc = dd[:R, :]
            out = dxc * cw_ref[K - 1:K, :]
            for j in range(1, K):
                out = out + _advanced(dd, j, R) * cw_ref[K - 1 - j:K - j, :]
            dxr_ref[pl.ds(r0, R), :] = out.astype(BF16)
            new = [acc[k] + jnp.sum(dxc * _delayed(xx, K - 1 - k, R), axis=0, keepdims=True) for k in range(K)]
            new.append(acc[K] + jnp.sum(dxc, axis=0, keepdims=True))
            return tuple(new)

        acc = lax.fori_loop(0, n_chunks, chunk_d, (z1,) * (K + 1))

        @pl.when(b == 0)
        def _():
            for k in range(K):
                dcw_ref[k:k + 1, :] = acc[k]
            dcb_ref[...] = acc[K]
            dba_ref[...] = dba
            dbx_ref[...] = dbx
            dlam_ref[...] = dlam

        @pl.when(b > 0)
        def _():
            for k in range(K):
                dcw_ref[k:k + 1, :] += acc[k]
            dcb_ref[...] += acc[K]
            dba_ref[...] += dba
            dbx_ref[...] += dbx
            dlam_ref[...] += dlam

    vec = pl.BlockSpec((1, Dh), lambda h, b: (0, h))
    wsp = pl.BlockSpec((None, Dh, Dh), lambda h, b: (h, 0, 0))
    seq = pl.BlockSpec((S, Dh), lambda h, b: (b, h))
    ck = pl.BlockSpec((K, Dh), lambda h, b: (0, h))
    row_out = jax.ShapeDtypeStruct((1, W), F32)
    return pl.pallas_call(
        body, name=name, grid=(H, batch),
        in_specs=[pl.BlockSpec((S, Dh), lambda h, b: (b, ox + h)), pl.BlockSpec((S, Dh), lambda h, b: (b, oy + h)),
                  seq, seq, ck, vec, wsp, wsp, wsp, wsp, vec, vec, vec],
        out_specs=[seq, seq, wsp, wsp, ck, vec, vec, vec, vec],
        out_shape=[jax.ShapeDtypeStruct((T, W), BF16), jax.ShapeDtypeStruct((T, W), BF16),
                   jax.ShapeDtypeStruct((H, Dh, Dh), F32), jax.ShapeDtypeStruct((H, Dh, Dh), F32),
                   jax.ShapeDtypeStruct((K, W), F32), row_out, row_out, row_out, row_out],
        scratch_shapes=[pltpu.VMEM((S, Dh), F32)] * 6,
        compiler_params=_cp("parallel", "arbitrary"),
    )(proj, proj, hseq, dyb, cw, cb, wa, wx, wat, wxt, ba, bx, lam)


def _adamw(w, g, m, v, *, name):
    R, C = w.shape
    tr, tc = _blk(R, 256), _blk(C, 1024)

    def body(w_ref, g_ref, m_ref, v_ref, d_ref, nm_ref, nv_ref):
        gv = g_ref[...]
        nm = ADAM_B1 * m_ref[...] + (1.0 - ADAM_B1) * gv
        nv = ADAM_B2 * v_ref[...] + (1.0 - ADAM_B2) * (gv * gv)
        m_hat = nm / (1.0 - ADAM_B1 ** ADAM_STEP)
        v_hat = nv / (1.0 - ADAM_B2 ** ADAM_STEP)
        d_ref[...] = -ADAM_LR * (m_hat / (jnp.sqrt(v_hat) + ADAM_EPS) + ADAM_WD * w_ref[...])
        nm_ref[...] = nm
        nv_ref[...] = nv

    blk = pl.BlockSpec((tr, tc), lambda i, j: (i, j))
    out = jax.ShapeDtypeStruct((R, C), F32)
    return pl.pallas_call(
        body, name=name, grid=(R // tr, C // tc), in_specs=[blk] * 4, out_specs=[blk] * 3,
        out_shape=[out] * 3, compiler_params=_cp("parallel", "parallel"),
    )(w, g, m, v)


def _add_pair(a, b, *, name):
    R, C = a.shape
    tr = _blk(R, 1024)

    def body(a_ref, b_ref, o_ref):
        o_ref[...] = a_ref[...] + b_ref[...]

    blk = pl.BlockSpec((tr, C), lambda i: (i, 0))
    return pl.pallas_call(body, name=name, grid=(R // tr,), in_specs=[blk, blk], out_specs=blk,
                          out_shape=jax.ShapeDtypeStruct((R, C), a.dtype), compiler_params=_cp("parallel"))(a, b)


def _sum_chips(q, *, name):
    _, R, C = q.shape
    tr = _blk(R, 1024)

    def body(q_ref, o_ref):
        o_ref[...] = ((q_ref[0] + q_ref[1]) + q_ref[2]) + q_ref[3]

    return pl.pallas_call(body, name=name, grid=(R // tr,),
                          in_specs=[pl.BlockSpec((N_CHIPS, tr, C), lambda i: (0, i, 0))],
                          out_specs=pl.BlockSpec((tr, C), lambda i: (i, 0)),
                          out_shape=jax.ShapeDtypeStruct((R, C), q.dtype), compiler_params=_cp("parallel"))(q)


def _other_chips(x, y):
    return [(1 - x, y), (x, 1 - y), (1 - x, 1 - y)]


def _gather_chips(shard, *, name):
    R, C = shard.shape
    hr = R // 2

    def body(x_ref, o_ref, send_sems, recv_sems, local_sem):
        x, y, c = _mesh_pos()
        me = 2 * x + y
        sibling = (x, y, 1 - c)
        chips = _other_chips(x, y)

        def slab(chip, half):
            return o_ref.at[chip, pl.ds(half * hr, hr), :]

        def copy(k, src, dst, to):
            return pltpu.make_async_remote_copy(src_ref=src, dst_ref=dst, send_sem=send_sems.at[k],
                                                recv_sem=recv_sems.at[k], device_id=to, device_id_type=MESH)

        mine = pltpu.make_async_copy(x_ref, o_ref.at[me], local_sem)
        mine.start()
        first = [copy(j, x_ref.at[pl.ds(c * hr, hr), :], slab(me, c), (px, py, c))
                 for j, (px, py) in enumerate(chips)]
        for cp in first:
            cp.start()
        passed = []
        for j, (px, py) in enumerate(chips):
            src = slab(2 * px + py, c)
            copy(j, src, src, (px, py, c)).wait_recv()
            fw = copy(3 + j, src, src, sibling)
            fw.start()
            passed.append(fw)
        for j, (px, py) in enumerate(chips):
            dst = slab(2 * px + py, 1 - c)
            copy(3 + j, dst, dst, sibling).wait_recv()
        for cp in first + passed:
            cp.wait_send()
        mine.wait()

    return pl.pallas_call(
        body, name=name, in_specs=[ANY], out_specs=ANY,
        out_shape=jax.ShapeDtypeStruct((N_CHIPS, R, C), shard.dtype),
        scratch_shapes=[pltpu.SemaphoreType.DMA((6,)), pltpu.SemaphoreType.DMA((6,)), pltpu.SemaphoreType.DMA],
    )(shard)


def _pair_swap_halves(g, *, name):
    n, R, C = g.shape
    hr = R // 2

    def body(g_ref, o_ref, send_sem, recv_sem):
        x, y, c = _mesh_pos()
        cp = pltpu.make_async_remote_copy(
            src_ref=g_ref.at[:, pl.ds((1 - c) * hr, hr), :], dst_ref=o_ref, send_sem=send_sem, recv_sem=recv_sem,
            device_id=(x, y, 1 - c), device_id_type=MESH)
        cp.start()
        cp.wait()

    return pl.pallas_call(
        body, name=name, in_specs=[ANY], out_specs=ANY,
        out_shape=jax.ShapeDtypeStruct((n, hr, C), g.dtype),
        scratch_shapes=[pltpu.SemaphoreType.DMA, pltpu.SemaphoreType.DMA],
    )(g)


def _pair_add_halves(g, rb, cpos, *, name):
    n, R, C = g.shape
    hr = R // 2
    tr, tc = _blk(hr, 512), _blk(C, 1024)
    nrb = hr // tr

    def body(c_ref, g_ref, r_ref, o_ref):
        o_ref[...] = (g_ref[...].astype(F32) + r_ref[...].astype(F32)).astype(o_ref.dtype)

    return pl.pallas_call(
        body, name=name,
        grid_spec=pltpu.PrefetchScalarGridSpec(
            num_scalar_prefetch=1, grid=(n, nrb, C // tc),
            in_specs=[pl.BlockSpec((None, tr, tc), lambda s, i, j, c_ref: (s, c_ref[0] * nrb + i, j)),
                      pl.BlockSpec((None, tr, tc), lambda s, i, j, c_ref: (s, i, j))],
            out_specs=pl.BlockSpec((None, tr, tc), lambda s, i, j, c_ref: (s, i, j))),
        out_shape=jax.ShapeDtypeStruct((n, hr, C), g.dtype),
        compiler_params=_cp("parallel", "parallel", "parallel"),
    )(cpos, g, rb)


def _chip_exchange(p, *, name):
    _, hr, C = p.shape

    def body(p_ref, q_ref, send_sems, recv_sems):
        x, y, c = _mesh_pos()
        cps = []
        for j, (px, py) in enumerate(_other_chips(x, y)):
            cp = pltpu.make_async_remote_copy(
                src_ref=p_ref.at[2 * px + py], dst_ref=q_ref.at[j], send_sem=send_sems.at[j],
                recv_sem=recv_sems.at[j], device_id=(px, py, c), device_id_type=MESH)
            cp.start()
            cps.append(cp)
        for cp in cps:
            cp.wait()

    return pl.pallas_call(
        body, name=name, in_specs=[ANY], out_specs=ANY,
        out_shape=jax.ShapeDtypeStruct((N_CHIPS - 1, hr, C), p.dtype),
        scratch_shapes=[pltpu.SemaphoreType.DMA((3,)), pltpu.SemaphoreType.DMA((3,))],
    )(p)


def _chip_final_add(p, q, chip, *, name):
    _, hr, C = p.shape
    tr, tc = _blk(hr, 512), _blk(C, 1024)

    def body(k_ref, p_ref, q_ref, o_ref):
        o_ref[...] = ((p_ref[...].astype(F32) + q_ref[0].astype(F32)) + q_ref[1].astype(F32)) + q_ref[2].astype(F32)

    return pl.pallas_call(
        body, name=name,
        grid_spec=pltpu.PrefetchScalarGridSpec(
            num_scalar_prefetch=1, grid=(hr // tr, C // tc),
            in_specs=[pl.BlockSpec((None, tr, tc), lambda i, j, k_ref: (k_ref[0], i, j)),
                      pl.BlockSpec((N_CHIPS - 1, tr, tc), lambda i, j, k_ref: (0, i, j))],
            out_specs=pl.BlockSpec((tr, tc), lambda i, j, k_ref: (i, j))),
        out_shape=jax.ShapeDtypeStruct((hr, C), F32),
        compiler_params=_cp("parallel", "parallel"),
    )(chip, p, q)


def _pair_share_halves(h, *, name):
    hr, C = h.shape

    def body(h_ref, o_ref, send_sem, recv_sem, local_sem):
        x, y, c = _mesh_pos()
        mine = pltpu.make_async_copy(h_ref, o_ref.at[pl.ds(c * hr, hr), :], local_sem)
        mine.start()
        cp = pltpu.make_async_remote_copy(
            src_ref=h_ref, dst_ref=o_ref.at[pl.ds(c * hr, hr), :], send_sem=send_sem, recv_sem=recv_sem,
            device_id=(x, y, 1 - c), device_id_type=MESH)
        cp.start()
        cp.wait_send()
        pltpu.make_async_remote_copy(
            src_ref=h_ref, dst_ref=o_ref.at[pl.ds((1 - c) * hr, hr), :], send_sem=send_sem, recv_sem=recv_sem,
            device_id=(x, y, 1 - c), device_id_type=MESH).wait_recv()
        mine.wait()

    return pl.pallas_call(
        body, name=name, in_specs=[ANY], out_specs=ANY,
        out_shape=jax.ShapeDtypeStruct((2 * hr, C), h.dtype),
        scratch_shapes=[pltpu.SemaphoreType.DMA, pltpu.SemaphoreType.DMA, pltpu.SemaphoreType.DMA],
    )(h)


def _pair_swap(v, *, name):
    def body(v_ref, o_ref, send_sem, recv_sem):
        x, y, c = _mesh_pos()
        cp = pltpu.make_async_remote_copy(src_ref=v_ref, dst_ref=o_ref, send_sem=send_sem, recv_sem=recv_sem,
                                          device_id=(x, y, 1 - c), device_id_type=MESH)
        cp.start()
        cp.wait()

    return pl.pallas_call(
        body, name=name, in_specs=[ANY], out_specs=ANY, out_shape=jax.ShapeDtypeStruct(v.shape, v.dtype),
        scratch_shapes=[pltpu.SemaphoreType.DMA, pltpu.SemaphoreType.DMA],
    )(v)


def _chip_allgather(v, *, name):
    R, C = v.shape

    def body(v_ref, o_ref, send_sems, recv_sems, local_sem):
        x, y, c = _mesh_pos()
        me = 2 * x + y
        mine = pltpu.make_async_copy(v_ref, o_ref.at[me], local_sem)
        mine.start()
        cps = []
        for j, (px, py) in enumerate(_other_chips(x, y)):
            cp = pltpu.make_async_remote_copy(
                src_ref=v_ref, dst_ref=o_ref.at[me], send_sem=send_sems.at[j], recv_sem=recv_sems.at[j],
                device_id=(px, py, c), device_id_type=MESH)
            cp.start()
            cps.append(cp)
        for j, (px, py) in enumerate(_other_chips(x, y)):
            cps[j].wait_send()
            pltpu.make_async_remote_copy(
                src_ref=v_ref, dst_ref=o_ref.at[2 * px + py], send_sem=send_sems.at[j], recv_sem=recv_sems.at[j],
                device_id=(px, py, c), device_id_type=MESH).wait_recv()
        mine.wait()

    return pl.pallas_call(
        body, name=name, in_specs=[ANY], out_specs=ANY,
        out_shape=jax.ShapeDtypeStruct((N_CHIPS, R, C), v.dtype),
        scratch_shapes=[pltpu.SemaphoreType.DMA((3,)), pltpu.SemaphoreType.DMA((3,)), pltpu.SemaphoreType.DMA],
    )(v)


def _reduce_scatter(g, cpos, chip, *, tag):
    rb = _pair_swap_halves(g, name=f"rs_pair_swap_{tag}")
    p = _pair_add_halves(g, rb, cpos, name=f"rs_pair_add_{tag}")
    q = _chip_exchange(p, name=f"rs_chip_exchange_{tag}")
    h = _chip_final_add(p, q, chip, name=f"rs_final_add_{tag}")
    return _pair_share_halves(h, name=f"rs_share_{tag}")


def _all_reduce(v, *, tag):
    other = _pair_swap(v, name=f"ar_pair_swap_{tag}")
    chip_sum = _add_pair(v, other, name=f"ar_pair_add_{tag}")
    return _sum_chips(_chip_allgather(chip_sum, name=f"ar_allgather_{tag}"), name=f"ar_sum_{tag}")


def _pack(arrays, unit, total_unit=None):
    parts, n = [], 0
    for a in arrays:
        flat = a.reshape(-1)
        pad = (-flat.shape[0]) % unit
        parts.append(jnp.pad(flat, (0, pad)) if pad else flat)
        n += flat.shape[0] + pad
    if total_unit and n % total_unit:
        parts.append(jnp.zeros((-n) % total_unit, arrays[0].dtype))
    return jnp.concatenate(parts).reshape(-1, LANES)


def _unpack(packed, shapes, unit):
    lead = packed.shape[:-2]
    flat = packed.reshape(lead + (-1,))
    out, pos = [], 0
    for shp in shapes:
        n = math.prod(shp)
        out.append(flat[..., pos:pos + n].reshape(lead + tuple(shp)))
        pos += n + (-n) % unit
    return out


def kernel(x, g_mix, w_in, sg_ln_g, sg_ln_b, sg_w, sg_b, lru_conv_w, lru_conv_b, lru_wa, lru_ba, lru_wx, lru_bx, lru_lam, p_sg, p_lru, w_out, g_ffn, w_up, ffn_conv_w, ffn_conv_b, w_down, g_final, loss_target, m_g_mix, m_w_in, m_sg_ln_g, m_sg_ln_b, m_sg_w, m_sg_b, m_lru_conv_w, m_lru_conv_b, m_lru_wa, m_lru_ba, m_lru_wx, m_lru_bx, m_lru_lam, m_p_sg, m_p_lru, m_w_out, m_g_ffn, m_w_up, m_ffn_conv_w, m_ffn_conv_b, m_w_down, m_g_final, v_g_mix, v_w_in, v_sg_ln_g, v_sg_ln_b, v_sg_w, v_sg_b, v_lru_conv_w, v_lru_conv_b, v_lru_wa, v_lru_ba, v_lru_wx, v_lru_bx, v_lru_lam, v_p_sg, v_p_lru, v_w_out, v_g_ffn, v_w_up, v_ffn_conv_w, v_ffn_conv_b, v_w_down, v_g_final):
    params = dict(g_mix=g_mix, w_in=w_in, sg_ln_g=sg_ln_g, sg_ln_b=sg_ln_b, sg_w=sg_w, sg_b=sg_b,
                  lru_conv_w=lru_conv_w, lru_conv_b=lru_conv_b, lru_wa=lru_wa, lru_ba=lru_ba, lru_wx=lru_wx,
                  lru_bx=lru_bx, lru_lam=lru_lam, p_sg=p_sg, p_lru=p_lru, w_out=w_out, g_ffn=g_ffn, w_up=w_up,
                  ffn_conv_w=ffn_conv_w, ffn_conv_b=ffn_conv_b, w_down=w_down, g_final=g_final)
    mom1 = dict(g_mix=m_g_mix, w_in=m_w_in, sg_ln_g=m_sg_ln_g, sg_ln_b=m_sg_ln_b, sg_w=m_sg_w, sg_b=m_sg_b,
                lru_conv_w=m_lru_conv_w, lru_conv_b=m_lru_conv_b, lru_wa=m_lru_wa, lru_ba=m_lru_ba,
                lru_wx=m_lru_wx, lru_bx=m_lru_bx, lru_lam=m_lru_lam, p_sg=m_p_sg, p_lru=m_p_lru, w_out=m_w_out,
                g_ffn=m_g_ffn, w_up=m_w_up, ffn_conv_w=m_ffn_conv_w, ffn_conv_b=m_ffn_conv_b, w_down=m_w_down,
                g_final=m_g_final)
    mom2 = dict(g_mix=v_g_mix, w_in=v_w_in, sg_ln_g=v_sg_ln_g, sg_ln_b=v_sg_ln_b, sg_w=v_sg_w, sg_b=v_sg_b,
                lru_conv_w=v_lru_conv_w, lru_conv_b=v_lru_conv_b, lru_wa=v_lru_wa, lru_ba=v_lru_ba,
                lru_wx=v_lru_wx, lru_bx=v_lru_bx, lru_lam=v_lru_lam, p_sg=v_p_sg, p_lru=v_p_lru, w_out=v_w_out,
                g_ffn=v_g_ffn, w_up=v_w_up, ffn_conv_w=v_ffn_conv_w, ffn_conv_b=v_ffn_conv_b, w_down=v_w_down,
                g_final=v_g_final)
    names = list(params)
    big = ["w_in", "p_sg", "p_lru", "w_out", "w_up", "w_down"]
    col_sharded = {"w_in", "p_sg", "w_up"}
    small = [n for n in names if n not in big]

    batch, S, D = x.shape
    T = batch * S
    W_sg = sg_ln_g.shape[-1]
    H, _, Dh = lru_wa.shape[1:]
    W_lru = H * Dh
    K_lru = lru_conv_w.shape[1]
    K_ffn = ffn_conv_w.shape[1]
    F2 = ffn_conv_b.shape[-1]
    off_lru = 2 * W_sg
    off_gate = off_lru + 2 * W_lru

    cx, cy, cc = _mesh_pos()
    chip = 2 * cx + cy
    cpos = jnp.reshape(cc, (1,)).astype(jnp.int32)
    chip1 = jnp.reshape(chip, (1,)).astype(jnp.int32)

    xf = x.reshape(T, D)
    tgt = loss_target.reshape(T, D)

    wg = {n: _gather_chips(params[n][0].astype(BF16), name=f"gather_{n}") for n in big}
    w_in_g, p_sg_g, w_up_g = wg["w_in"], wg["p_sg"], wg["w_up"]
    p_lru_g = wg["p_lru"].reshape(-1, D)
    w_out_g = wg["w_out"].reshape(-1, D)
    w_down_g = wg["w_down"].reshape(-1, D)

    sharded_small = ["lru_conv_w", "ffn_conv_w", "lru_wa", "lru_wx"]
    unit_g = 2 * BF16_ROWS * LANES
    sm_shapes = [params[n][0].shape for n in sharded_small]
    sm = _gather_chips(_pack([params[n][0] for n in sharded_small], unit_g), name="gather_small")
    cwl_s, cwf_s, wa_s, wx_s = _unpack(sm, sm_shapes, unit_g)
    lru_cw = jnp.transpose(cwl_s, (1, 0, 2)).reshape(K_lru, W_lru)
    ffn_cw = jnp.transpose(cwf_s, (1, 0, 2)).reshape(K_ffn, F2)
    wa_full = jnp.transpose(wa_s, (1, 0, 2, 3)).reshape(H, Dh, Dh)
    wx_full = jnp.transpose(wx_s, (1, 0, 2, 3)).reshape(H, Dh, Dh)
    wa_b, wx_b = wa_full.astype(BF16), wx_full.astype(BF16)
    wat_b, wxt_b = jnp.swapaxes(wa_b, 1, 2), jnp.swapaxes(wx_b, 1, 2)

    wm, wmt = _sg_mask(sg_w[0], name="sg_mask")
    bt = sg_b[0].T

    h1 = _rms_fwd(xf, g_mix, name="rms1_fwd")
    proj = _mm_nn(h1, w_in_g, out_dtype=F32, name="mm_proj")
    y_a = _sg_fwd(proj, sg_ln_g, sg_ln_b, wm, bt, name="sg_fwd")
    y_b, hseq = _lru_fwd(proj, lru_cw, lru_conv_b, wa_b, wx_b, lru_ba, lru_bx, lru_lam,
                         batch=batch, off_x=off_lru, name="lru_fwd")
    pa = _mm_nn(y_a, p_sg_g, out_dtype=F32, name="mm_pa")
    pb = _mm_nn(y_b, p_lru_g, out_dtype=F32, name="mm_pb")
    merged = _merge_fwd(proj, pa, pb, off_a=off_gate, name="merge_fwd")
    x1 = _mm_nn(merged, w_out_g, out_dtype=F32, res=xf, name="mm_out")
    h2 = _rms_fwd(x1, g_ffn, name="rms2_fwd")
    up0 = _mm_nn(h2, w_up_g, out_dtype=F32, name="mm_up")
    act = _ffn_act_fwd(up0, ffn_cw, ffn_conv_b, batch=batch, name="ffn_act_fwd")
    x2 = _mm_nn(act, w_down_g, out_dtype=F32, res=x1, name="mm_down")
    lvec, dx2, dx2_b, dg_final = _loss_head(x2, tgt, g_final.reshape(1, D), name="loss_head")
    loss = lax.psum(jnp.sum(lvec) * (0.5 / D), ("x", "y", "c"))

    gw = {}
    dact = _mm_nt(dx2_b, w_down_g, out_dtype=F32, name="mm_dact")
    gw["w_down"] = _mm_tn(act, dx2_b, out_dtype=BF16, name="mm_dw_down").reshape(N_CHIPS, -1, D)
    dug, duv, d_ffn_cw, d_ffn_cb = _ffn_act_bwd(up0, ffn_cw, ffn_conv_b, dact, batch=batch, name="ffn_act_bwd")
    dup0 = jnp.concatenate([dug, duv], axis=1)
    dh2 = _mm_nt(dup0, w_up_g, out_dtype=F32, name="mm_dh2")
    gw["w_up"] = _mm_tn(h2, dup0, out_dtype=BF16, col_shards=N_CHIPS, name="mm_dw_up")
    dx1, dx1_b, dg_ffn = _rms_bwd(x1, g_ffn, dh2, dx2, name="rms2_bwd")
    dmerged = _mm_nt(dx1_b, w_out_g, out_dtype=F32, name="mm_dmerged")
    gw["w_out"] = _mm_tn(merged, dx1_b, out_dtype=BF16, name="mm_dw_out").reshape(N_CHIPS, -1, D)
    dga, dgb, dpa, dpb = _merge_bwd(proj, pa, pb, dmerged, off_a=off_gate, name="merge_bwd")
    dya = _mm_nt(dpa, p_sg_g, out_dtype=F32, name="mm_dya")
    gw["p_sg"] = _mm_tn(y_a, dpa, out_dtype=BF16, col_shards=N_CHIPS, name="mm_dp_sg")
    dyb = _mm_nt(dpb, p_lru_g, out_dtype=F32, name="mm_dyb")
    gw["p_lru"] = _mm_tn(y_b, dpb, out_dtype=BF16, name="mm_dp_lru").reshape(N_CHIPS, -1, D)
    dxr, dyr, d_wa, d_wx, d_lru_cw, d_lru_cb, d_ba, d_bx, d_lam = _lru_bwd(
        proj, hseq, dyb, lru_cw, lru_conv_b, wa_b, wx_b, wat_b, wxt_b, lru_ba, lru_bx, lru_lam,
        batch=batch, off_x=off_lru, name="lru_bwd")
    dzuv, d_wm, d_bt, d_lg, d_lb = _sg_bwd(proj, dya, sg_ln_g, sg_ln_b, wm, wmt, bt, name="sg_bwd")
    dproj = jnp.concatenate([dzuv, dxr, dyr, dga, dgb], axis=1)
    dh1 = _mm_nt(dproj, w_in_g, out_dtype=F32, name="mm_dh1")
    gw["w_in"] = _mm_tn(h1, dproj, out_dtype=BF16, col_shards=N_CHIPS, name="mm_dw_in")
    dx, _, dg_mix = _rms_bwd(xf, g_mix, dh1, dx1, name="rms1_bwd")

    grads = {n: _reduce_scatter(gw[n], cpos, chip1, tag=n) for n in big}
    small_full = dict(g_mix=dg_mix, sg_ln_g=d_lg, sg_ln_b=d_lb, sg_w=d_wm, sg_b=d_bt.T, lru_conv_w=d_lru_cw,
                      lru_conv_b=d_lru_cb, lru_wa=d_wa, lru_ba=d_ba, lru_wx=d_wx, lru_bx=d_bx, lru_lam=d_lam,
                      g_ffn=dg_ffn, ffn_conv_w=d_ffn_cw, ffn_conv_b=d_ffn_cb, g_final=dg_final)
    unit_s = SUBLANES * LANES
    pack_s = 512 * LANES
    red = _all_reduce(_pack([small_full[n] for n in small], unit_s, pack_s), tag="small")
    red = dict(zip(small, _unpack(red, [small_full[n].shape for n in small], unit_s)))
    cs_lru = W_lru // N_CHIPS
    cs_ffn = F2 // N_CHIPS
    rs_wa = Dh // N_CHIPS
    red["lru_conv_w"] = lax.dynamic_slice_in_dim(red["lru_conv_w"], chip * cs_lru, cs_lru, axis=1)
    red["ffn_conv_w"] = lax.dynamic_slice_in_dim(red["ffn_conv_w"], chip * cs_ffn, cs_ffn, axis=1)
    red["lru_wa"] = lax.dynamic_slice_in_dim(red["lru_wa"], chip * rs_wa, rs_wa, axis=1)
    red["lru_wx"] = lax.dynamic_slice_in_dim(red["lru_wx"], chip * rs_wa, rs_wa, axis=1)
    for n in small:
        grads[n] = red[n].reshape(params[n].shape)
    for n in big:
        grads[n] = grads[n].reshape(params[n].shape)

    delta, new_m, new_v = {}, {}, {}
    for n in big:
        shp = params[n].shape
        two_d = (-1, shp[-1])
        d, nm, nv = _adamw(params[n].reshape(two_d), grads[n].reshape(two_d), mom1[n].reshape(two_d),
                           mom2[n].reshape(two_d), name=f"adamw_{n}")
        delta[n], new_m[n], new_v[n] = d.reshape(shp), nm.reshape(shp), nv.reshape(shp)
    packs = [_pack([src[n] for n in small], unit_s, pack_s) for src in (params, grads, mom1, mom2)]
    outs = _adamw(*packs, name="adamw_small")
    shapes = [params[n].shape for n in small]
    for dst, packed in zip((delta, new_m, new_v), outs):
        dst.update(dict(zip(small, _unpack(packed, shapes, unit_s))))

    return (loss, dx.reshape(x.shape), *[grads[n] for n in names], *[delta[n] for n in names],
            *[new_m[n] for n in names], *[new_v[n] for n in names])
```

```python
import math

import jax
import jax.numpy as jnp
from jax import lax
from jax.experimental import pallas as pl
from jax.experimental.pallas import tpu as pltpu

F32 = jnp.float32
BF16 = jnp.bfloat16
MESH = pl.DeviceIdType.MESH
ANY = pl.BlockSpec(memory_space=pl.ANY)

EPS = 1e-6
LRU_C = 8.0
ADAM_LR = 0.001
ADAM_B1 = 0.9
ADAM_B2 = 0.999
ADAM_EPS = 1e-08
ADAM_WD = 0.01
ADAM_STEP = 10

N_CHIPS = 4
SUBLANES = 8
BF16_ROWS = 16
LANES = 128
VMEM_LIMIT = 56 * 1024 * 1024
GELU_C = math.sqrt(2.0 / math.pi)
GELU_K = 0.044715


def _cp(*sem):
    return pltpu.CompilerParams(dimension_semantics=sem, vmem_limit_bytes=VMEM_LIMIT)


def _blk(dim, pref):
    if dim <= pref:
        return dim
    b = pref
    while dim % b:
        b //= 2
    return b


def _gelu(x):
    t = jnp.tanh(GELU_C * (x + GELU_K * x * x * x))
    return 0.5 * x * (1.0 + t)


def _gelu_and_grad(x):
    x2 = x * x
    t = jnp.tanh(GELU_C * (x + GELU_K * x * x2))
    g = 0.5 * x * (1.0 + t)
    dg = 0.5 * (1.0 + t) + 0.5 * x * (1.0 - t * t) * (GELU_C * (1.0 + 3.0 * GELU_K * x2))
    return g, dg


def _sigmoid(x):
    return 1.0 / (1.0 + jnp.exp(-x))


def _softplus(x):
    e = jnp.exp(-jnp.abs(x))
    series = e * (1.0 - e * (0.5 - e * (1.0 / 3.0 - e * (0.25 - e * 0.2))))
    return jnp.where(e < 0.01, series, jnp.log(1.0 + e)) + jnp.maximum(x, 0.0)


def _neg_expm1(x):
    series = -(x * (1.0 + x * (0.5 + x * (1.0 / 6.0 + x * (1.0 / 24.0)))))
    return jnp.where(x > -0.01, series, 1.0 - jnp.exp(x))


def _mesh_pos():
    return lax.axis_index("x"), lax.axis_index("y"), lax.axis_index("c")


def _other_chips(x, y):
    return [(1 - x, y), (x, 1 - y), (1 - x, 1 - y)]


def _remote(k, src, dst, to, send_sems, recv_sems):
    return pltpu.make_async_remote_copy(src_ref=src, dst_ref=dst, send_sem=send_sems.at[k],
                                        recv_sem=recv_sems.at[k], device_id=to, device_id_type=MESH)


class _GatherRows:
    n_sems = 6

    def __init__(self, buf, r0, r1):
        self.args = [buf]
        self.out_shape = [jax.ShapeDtypeStruct(buf.shape, buf.dtype)]
        self.aliases = {0: 0}
        self.r0, self.h = r0, (r1 - r0) // 2

    def _rows(self, half):
        return pl.ds(self.r0 + half * self.h, self.h)

    def start(self, ins, outs, ss, rs, base):
        x, y, c = _mesh_pos()
        mine = ins[0].at[2 * x + y, self._rows(c), :]
        for j, (px, py) in enumerate(_other_chips(x, y)):
            _remote(base + j, mine, outs[0].at[2 * x + y, self._rows(c), :], (px, py, c), ss, rs).start()

    def finish(self, ins, outs, ss, rs, base):
        x, y, c = _mesh_pos()
        sibling = (x, y, 1 - c)
        chips = _other_chips(x, y)
        mine = ins[0].at[2 * x + y, self._rows(c), :]
        for j, (px, py) in enumerate(chips):
            got = outs[0].at[2 * px + py, self._rows(c), :]
            _remote(base + j, got, got, (px, py, c), ss, rs).wait_recv()
            _remote(base + 3 + j, got, got, sibling, ss, rs).start()
        for j, (px, py) in enumerate(chips):
            fwd = outs[0].at[2 * px + py, self._rows(1 - c), :]
            _remote(base + 3 + j, fwd, fwd, sibling, ss, rs).wait_recv()
        for j, (px, py) in enumerate(chips):
            got = outs[0].at[2 * px + py, self._rows(c), :]
            _remote(base + j, mine, mine, (px, py, c), ss, rs).wait_send()
            _remote(base + 3 + j, got, got, sibling, ss, rs).wait_send()


class _ChipExchange:
    n_sems = 3

    def __init__(self, p):
        self.args = [p]
        self.out_shape = [jax.ShapeDtypeStruct((N_CHIPS - 1,) + p.shape[1:], p.dtype)]
        self.aliases = {}

    def _copies(self, ins, outs, ss, rs, base):
        x, y, c = _mesh_pos()
        return [_remote(base + j, ins[0].at[2 * px + py], outs[0].at[j], (px, py, c), ss, rs)
                for j, (px, py) in enumerate(_other_chips(x, y))]

    def start(self, ins, outs, ss, rs, base):
        for cp in self._copies(ins, outs, ss, rs, base):
            cp.start()

    def finish(self, ins, outs, ss, rs, base):
        for cp in self._copies(ins, outs, ss, rs, base):
            cp.wait()


class _ShareHalves:
    n_sems = 1

    def __init__(self, buf):
        self.args = [buf]
        self.out_shape = [jax.ShapeDtypeStruct(buf.shape, buf.dtype)]
        self.aliases = {0: 0}
        self.hr = buf.shape[0] // 2

    def start(self, ins, outs, ss, rs, base):
        x, y, c = _mesh_pos()
        rows = pl.ds(c * self.hr, self.hr)
        _remote(base, ins[0].at[rows, :], outs[0].at[rows, :], (x, y, 1 - c), ss, rs).start()

    def finish(self, ins, outs, ss, rs, base):
        x, y, c = _mesh_pos()
        mine = ins[0].at[pl.ds(c * self.hr, self.hr), :]
        theirs = outs[0].at[pl.ds((1 - c) * self.hr, self.hr), :]
        _remote(base, mine, mine, (x, y, 1 - c), ss, rs).wait_send()
        _remote(base, theirs, theirs, (x, y, 1 - c), ss, rs).wait_recv()


class _GatherSlabs:
    n_sems = 3

    def __init__(self, buf):
        self.args = [buf]
        self.out_shape = [jax.ShapeDtypeStruct(buf.shape, buf.dtype)]
        self.aliases = {0: 0}

    def start(self, ins, outs, ss, rs, base):
        x, y, c = _mesh_pos()
        for j, (px, py) in enumerate(_other_chips(x, y)):
            _remote(base + j, ins[0].at[2 * x + y], outs[0].at[2 * x + y], (px, py, c), ss, rs).start()

    def finish(self, ins, outs, ss, rs, base):
        x, y, c = _mesh_pos()
        mine = ins[0].at[2 * x + y]
        for j, (px, py) in enumerate(_other_chips(x, y)):
            got = outs[0].at[2 * px + py]
            _remote(base + j, mine, mine, (px, py, c), ss, rs).wait_send()
            _remote(base + j, got, got, (px, py, c), ss, rs).wait_recv()


class _Carry:
    def __init__(self, *items):
        self.items = items
        self.args = [a for it in items for a in it.args]
        self.out_shape = [o for it in items for o in it.out_shape]
        self.n_sems = sum(it.n_sems for it in items)

    def aliases(self, in_base, out_base):
        out, i0, o0 = {}, 0, 0
        for it in self.items:
            out.update({in_base + i0 + i: out_base + o0 + o for i, o in it.aliases.items()})
            i0 += len(it.args)
            o0 += len(it.out_shape)
        return out

    def _each(self, method, ins, outs, ss, rs):
        i0 = o0 = base = 0
        for it in self.items:
            ni, no = len(it.args), len(it.out_shape)
            getattr(it, method)(ins[i0:i0 + ni], outs[o0:o0 + no], ss, rs, base)
            i0, o0, base = i0 + ni, o0 + no, base + it.n_sems

    def start(self, ins, outs, ss, rs):
        self._each("start", ins, outs, ss, rs)

    def finish(self, ins, outs, ss, rs):
        self._each("finish", ins, outs, ss, rs)


def _carried_call(body, *, name, grid, in_specs, out_specs, out_shape, scratch_shapes, args, semantics, carry=None):
    if carry is None:
        outs = pl.pallas_call(body, name=name, grid=grid, in_specs=in_specs, out_specs=out_specs,
                              out_shape=out_shape, scratch_shapes=scratch_shapes,
                              compiler_params=_cp(*semantics))(*args)
        return list(outs), []
    n_in, n_out, n_scr = len(in_specs), len(out_specs), len(scratch_shapes)
    n_cin, n_cout = len(carry.args), len(carry.out_shape)

    def full(*refs):
        ins = refs[:n_in]
        cins = refs[n_in:n_in + n_cin]
        outs = refs[n_in + n_cin:n_in + n_cin + n_out]
        couts = refs[n_in + n_cin + n_out:n_in + n_cin + n_out + n_cout]
        scr = refs[n_in + n_cin + n_out + n_cout:n_in + n_cin + n_out + n_cout + n_scr]
        ss, rs = refs[-2], refs[-1]
        first = pl.program_id(0) == 0
        last = pl.program_id(0) == grid[0] - 1
        for d in range(1, len(grid)):
            first = jnp.logical_and(first, pl.program_id(d) == 0)
            last = jnp.logical_and(last, pl.program_id(d) == grid[d] - 1)

        @pl.when(first)
        def _():
            carry.start(cins, couts, ss, rs)

        body(*ins, *outs, *scr)

        @pl.when(last)
        def _():
            carry.finish(cins, couts, ss, rs)

    outs = pl.pallas_call(
        full, name=name, grid=grid, in_specs=list(in_specs) + [ANY] * n_cin,
        out_specs=list(out_specs) + [ANY] * n_cout, out_shape=list(out_shape) + carry.out_shape,
        scratch_shapes=list(scratch_shapes) + [pltpu.SemaphoreType.DMA((carry.n_sems,))] * 2,
        input_output_aliases=carry.aliases(n_in, n_out),
        compiler_params=_cp(*(("arbitrary",) * len(grid))),
    )(*args, *carry.args)
    return list(outs[:n_out]), list(outs[n_out:])


def _comm_call(carry, *, name):
    n_cin = len(carry.args)

    def body(*refs):
        cins, couts = refs[:n_cin], refs[n_cin:-2]
        carry.start(cins, couts, refs[-2], refs[-1])
        carry.finish(cins, couts, refs[-2], refs[-1])

    outs = pl.pallas_call(
        body, name=name, in_specs=[ANY] * n_cin, out_specs=[ANY] * len(carry.out_shape), out_shape=carry.out_shape,
        scratch_shapes=[pltpu.SemaphoreType.DMA((carry.n_sems,))] * 2,
        input_output_aliases=carry.aliases(0, 0),
    )(*carry.args)
    return list(outs)


def _mm_nn(a, b, *, out_dtype, name, res=None, carry=None):
    M, K = a.shape
    cs = b.shape[-1]
    N = cs * (b.shape[0] if b.ndim == 3 else 1)
    tm, tn, tk = _blk(M, 1024), _blk(cs, 1024), _blk(K, 2048)
    nbs, nk = cs // tn, K // tk
    if b.ndim == 3:
        b_spec = pl.BlockSpec((None, tk, tn), lambda i, j, k: (j // nbs, k, j % nbs))
    else:
        b_spec = pl.BlockSpec((tk, tn), lambda i, j, k: (k, j))
    in_specs = [pl.BlockSpec((tm, tk), lambda i, j, k: (i, k)), b_spec]
    args = [a, b]
    if res is not None:
        in_specs.append(pl.BlockSpec((tm, tn), lambda i, j, k: (i, j)))
        args.append(res)

    def body(*refs):
        a_ref, b_ref = refs[0], refs[1]
        r_ref = refs[2] if res is not None else None
        o_ref, acc = refs[-2], refs[-1]
        k = pl.program_id(2)
        p = jnp.dot(a_ref[...], b_ref[...], preferred_element_type=F32)

        @pl.when(k == 0)
        def _():
            acc[...] = p

        @pl.when(k > 0)
        def _():
            acc[...] += p

        @pl.when(k == nk - 1)
        def _():
            r = acc[...]
            if r_ref is not None:
                r = r + r_ref[...]
            o_ref[...] = r.astype(out_dtype)

    outs, carried = _carried_call(
        body, name=name, grid=(M // tm, N // tn, nk), in_specs=in_specs,
        out_specs=[pl.BlockSpec((tm, tn), lambda i, j, k: (i, j))],
        out_shape=[jax.ShapeDtypeStruct((M, N), out_dtype)],
        scratch_shapes=[pltpu.VMEM((tm, tn), F32)], args=args,
        semantics=("parallel", "parallel", "arbitrary"), carry=carry)
    return outs[0] if carry is None else (outs[0], carried)


def _mm_nt(a, b, *, out_dtype, name, carry=None):
    M, Kc = a.shape
    cs = b.shape[-1]
    N = b.shape[-2]
    tm, tn, tk = _blk(M, 1024), _blk(N, 1024), _blk(cs, 2048)
    nks, nk = cs // tk, Kc // tk
    if b.ndim == 3:
        b_spec = pl.BlockSpec((None, tn, tk), lambda i, j, k: (k // nks, j, k % nks))
    else:
        b_spec = pl.BlockSpec((tn, tk), lambda i, j, k: (j, k))

    def body(a_ref, b_ref, o_ref, acc):
        k = pl.program_id(2)
        p = lax.dot_general(a_ref[...], b_ref[...], (((1,), (1,)), ((), ())), preferred_element_type=F32)

        @pl.when(k == 0)
        def _():
            acc[...] = p

        @pl.when(k > 0)
        def _():
            acc[...] += p

        @pl.when(k == nk - 1)
        def _():
            o_ref[...] = acc[...].astype(out_dtype)

    outs, carried = _carried_call(
        body, name=name, grid=(M // tm, N // tn, nk),
        in_specs=[pl.BlockSpec((tm, tk), lambda i, j, k: (i, k)), b_spec],
        out_specs=[pl.BlockSpec((tm, tn), lambda i, j, k: (i, j))],
        out_shape=[jax.ShapeDtypeStruct((M, N), out_dtype)],
        scratch_shapes=[pltpu.VMEM((tm, tn), F32)], args=[a, b],
        semantics=("parallel", "parallel", "arbitrary"), carry=carry)
    return outs[0] if carry is None else (outs[0], carried)


def _mm_tn(a, b, *, out_dtype, name, col_shards=None):
    T, K1 = a.shape
    N = b.shape[1]
    cs = N // col_shards if col_shards else N
    tm, tn, tk = _blk(K1, 1024), _blk(cs, 1024), _blk(T, 2048)
    nbs, nk = cs // tn, T // tk
    if col_shards:
        o_spec = pl.BlockSpec((None, tm, tn), lambda i, j, k: (j // nbs, i, j % nbs))
        o_shape = jax.ShapeDtypeStruct((col_shards, K1, cs), out_dtype)
    else:
        o_spec = pl.BlockSpec((tm, tn), lambda i, j, k: (i, j))
        o_shape = jax.ShapeDtypeStruct((K1, N), out_dtype)

    def body(a_ref, b_ref, o_ref, acc):
        k = pl.program_id(2)
        p = lax.dot_general(a_ref[...], b_ref[...], (((0,), (0,)), ((), ())), preferred_element_type=F32)

        @pl.when(k == 0)
        def _():
            acc[...] = p

        @pl.when(k > 0)
        def _():
            acc[...] += p

        @pl.when(k == nk - 1)
        def _():
            o_ref[...] = acc[...].astype(out_dtype)

    return pl.pallas_call(
        body, name=name, grid=(K1 // tm, N // tn, nk),
        in_specs=[pl.BlockSpec((tk, tm), lambda i, j, k: (k, i)), pl.BlockSpec((tk, tn), lambda i, j, k: (k, j))],
        out_specs=o_spec, out_shape=o_shape,
        scratch_shapes=[pltpu.VMEM((tm, tn), F32)],
        compiler_params=_cp("parallel", "parallel", "arbitrary"),
    )(a, b)


def _rms_fwd(x, g, *, name):
    T, D = x.shape
    tm = _blk(T, 256)

    def body(x_ref, g_ref, o_ref):
        xv = x_ref[...]
        r = lax.rsqrt(jnp.mean(xv * xv, axis=-1, keepdims=True) + EPS)
        o_ref[...] = (xv * r * g_ref[...]).astype(BF16)

    return pl.pallas_call(
        body, name=name, grid=(T // tm,),
        in_specs=[pl.BlockSpec((tm, D), lambda i: (i, 0)), pl.BlockSpec((1, D), lambda i: (0, 0))],
        out_specs=pl.BlockSpec((tm, D), lambda i: (i, 0)),
        out_shape=jax.ShapeDtypeStruct((T, D), BF16),
        compiler_params=_cp("parallel"),
    )(x, g)


def _rms_bwd(x, g, dh, dres, *, name):
    T, D = x.shape
    tm = _blk(T, 256)

    def body(x_ref, g_ref, dh_ref, dres_ref, dx_ref, dxb_ref, dg_ref):
        i = pl.program_id(0)
        xv = x_ref[...]
        r = lax.rsqrt(jnp.mean(xv * xv, axis=-1, keepdims=True) + EPS)
        n = xv * r
        dh_v = dh_ref[...]
        dn = dh_v * g_ref[...]
        dx = dres_ref[...] + r * (dn - n * jnp.mean(dn * n, axis=-1, keepdims=True))
        dx_ref[...] = dx
        dxb_ref[...] = dx.astype(BF16)
        part = jnp.sum(dh_v * n, axis=0, keepdims=True)

        @pl.when(i == 0)
        def _():
            dg_ref[...] = part

        @pl.when(i > 0)
        def _():
            dg_ref[...] += part

    row = pl.BlockSpec((tm, D), lambda i: (i, 0))
    vec = pl.BlockSpec((1, D), lambda i: (0, 0))
    return pl.pallas_call(
        body, name=name, grid=(T // tm,),
        in_specs=[row, vec, row, row], out_specs=[row, row, vec],
        out_shape=[jax.ShapeDtypeStruct((T, D), F32), jax.ShapeDtypeStruct((T, D), BF16),
                   jax.ShapeDtypeStruct((1, D), F32)],
        compiler_params=_cp("arbitrary"),
    )(x, g, dh, dres)


def _loss_head(x2, tgt, g, *, name):
    T, D = x2.shape
    tm = _blk(T, 256)

    def body(x_ref, t_ref, g_ref, l_ref, dx_ref, dxb_ref, dg_ref):
        i = pl.program_id(0)
        xv = x_ref[...]
        gv = g_ref[...]
        r = lax.rsqrt(jnp.mean(xv * xv, axis=-1, keepdims=True) + EPS)
        n = xv * r
        err = n * gv - t_ref[...]
        dy = err * (1.0 / D)
        dn = dy * gv
        dx = r * (dn - n * jnp.mean(dn * n, axis=-1, keepdims=True))
        dx_ref[...] = dx
        dxb_ref[...] = dx.astype(BF16)
        lpart = jnp.sum(err * err, axis=0, keepdims=True)
        gpart = jnp.sum(dy * n, axis=0, keepdims=True)

        @pl.when(i == 0)
        def _():
            l_ref[...] = lpart
            dg_ref[...] = gpart

        @pl.when(i > 0)
        def _():
            l_ref[...] += lpart
            dg_ref[...] += gpart

    row = pl.BlockSpec((tm, D), lambda i: (i, 0))
    vec = pl.BlockSpec((1, D), lambda i: (0, 0))
    return pl.pallas_call(
        body, name=name, grid=(T // tm,),
        in_specs=[row, row, vec], out_specs=[vec, row, row, vec],
        out_shape=[jax.ShapeDtypeStruct((1, D), F32), jax.ShapeDtypeStruct((T, D), F32),
                   jax.ShapeDtypeStruct((T, D), BF16), jax.ShapeDtypeStruct((1, D), F32)],
        compiler_params=_cp("arbitrary"),
    )(x2, tgt, g)


def _merge_fwd(proj, pa, pb, *, off_a, name):
    T, D = pa.shape
    tm, tn = _blk(T, 256), _blk(D, 1024)
    oa, ob = off_a // tn, (off_a + D) // tn

    def body(ga_ref, gb_ref, pa_ref, pb_ref, o_ref):
        o_ref[...] = (_sigmoid(ga_ref[...]) * pa_ref[...] + _sigmoid(gb_ref[...]) * pb_ref[...]).astype(BF16)

    blk = pl.BlockSpec((tm, tn), lambda i, j: (i, j))
    return pl.pallas_call(
        body, name=name, grid=(T // tm, D // tn),
        in_specs=[pl.BlockSpec((tm, tn), lambda i, j: (i, oa + j)),
                  pl.BlockSpec((tm, tn), lambda i, j: (i, ob + j)), blk, blk],
        out_specs=blk, out_shape=jax.ShapeDtypeStruct((T, D), BF16),
        compiler_params=_cp("parallel", "parallel"),
    )(proj, proj, pa, pb)


def _merge_bwd(proj, pa, pb, dm, *, off_a, name):
    T, D = pa.shape
    tm, tn = _blk(T, 256), _blk(D, 1024)
    oa, ob = off_a // tn, (off_a + D) // tn

    def body(ga_ref, gb_ref, pa_ref, pb_ref, dm_ref, dga_ref, dgb_ref, dpa_ref, dpb_ref):
        dmv = dm_ref[...]
        sa = _sigmoid(ga_ref[...])
        sb = _sigmoid(gb_ref[...])
        dga_ref[...] = (dmv * pa_ref[...] * sa * (1.0 - sa)).astype(BF16)
        dgb_ref[...] = (dmv * pb_ref[...] * sb * (1.0 - sb)).astype(BF16)
        dpa_ref[...] = (dmv * sa).astype(BF16)
        dpb_ref[...] = (dmv * sb).astype(BF16)

    blk = pl.BlockSpec((tm, tn), lambda i, j: (i, j))
    out = jax.ShapeDtypeStruct((T, D), BF16)
    return pl.pallas_call(
        body, name=name, grid=(T // tm, D // tn),
        in_specs=[pl.BlockSpec((tm, tn), lambda i, j: (i, oa + j)),
                  pl.BlockSpec((tm, tn), lambda i, j: (i, ob + j)), blk, blk, blk],
        out_specs=[blk] * 4, out_shape=[out] * 4,
        compiler_params=_cp("parallel", "parallel"),
    )(proj, proj, pa, pb, dm)


def _sg_mask(sg_w, *, name):
    G, C, _ = sg_w.shape

    def body(w_ref, m_ref, mt_ref):
        row = lax.broadcasted_iota(jnp.int32, (C, C), 0)
        col = lax.broadcasted_iota(jnp.int32, (C, C), 1)
        for g in range(G):
            w = jnp.where(row >= col, w_ref[g], 0.0)
            m_ref[g] = w.astype(BF16)
            mt_ref[g] = w.T.astype(BF16)

    out = jax.ShapeDtypeStruct((G, C, C), BF16)
    return pl.pallas_call(body, name=name, out_shape=[out, out])(sg_w)


def _sg_layernorm(zv, lg, lb):
    v = _gelu(zv)
    mu = jnp.mean(v, axis=-1, keepdims=True)
    xc = v - mu
    rstd = lax.rsqrt(jnp.mean(xc * xc, axis=-1, keepdims=True) + EPS)
    vhat = xc * rstd
    return vhat, rstd, vhat * lg + lb


def _sg_fwd(proj, lg, lb, wm, bt, *, name):
    T = proj.shape[0]
    G, C, _ = wm.shape
    W = lg.shape[-1]
    gd = W // G

    def body(zu_ref, zv_ref, lg_ref, lb_ref, wm_ref, bt_ref, ya_ref, vn_scr):
        _, _, vn = _sg_layernorm(zv_ref[...], lg_ref[...], lb_ref[...])
        vn_scr[...] = vn.astype(BF16)
        for g in range(G):
            cols = slice(g * gd, (g + 1) * gd)
            mixed = jnp.dot(wm_ref[g], vn_scr[:, cols], preferred_element_type=F32) + bt_ref[:, g:g + 1]
            ya_ref[:, cols] = (_gelu(zu_ref[:, cols]) * mixed).astype(BF16)

    vec = pl.BlockSpec((1, W), lambda i: (0, 0))
    return pl.pallas_call(
        body, name=name, grid=(T // C,),
        in_specs=[pl.BlockSpec((C, W), lambda i: (i, 0)), pl.BlockSpec((C, W), lambda i: (i, 1)), vec, vec,
                  pl.BlockSpec((G, C, C), lambda i: (0, 0, 0)), pl.BlockSpec((C, G), lambda i: (0, 0))],
        out_specs=pl.BlockSpec((C, W), lambda i: (i, 0)),
        out_shape=jax.ShapeDtypeStruct((T, W), BF16),
        scratch_shapes=[pltpu.VMEM((C, W), BF16)],
        compiler_params=_cp("parallel"),
    )(proj, proj, lg, lb, wm, bt)


def _sg_bwd(proj, dya, lg, lb, wm, wmt, bt, *, name):
    T = proj.shape[0]
    G, C, _ = wm.shape
    W = lg.shape[-1]
    gd = W // G
    n_steps = T // C

    def body(zu_ref, zv_ref, dya_ref, lg_ref, lb_ref, wm_ref, wmt_ref, bt_ref,
             dz_ref, dwm_ref, dbt_ref, dlg_ref, dlb_ref, vn_scr, dvn_scr):
        i = pl.program_id(0)

        @pl.when(i == 0)
        def _():
            dwm_ref[...] = jnp.zeros_like(dwm_ref)
            dbt_ref[...] = jnp.zeros_like(dbt_ref)
            dlg_ref[...] = jnp.zeros_like(dlg_ref)
            dlb_ref[...] = jnp.zeros_like(dlb_ref)

        lgv = lg_ref[...]
        vhat, rstd, vn = _sg_layernorm(zv_ref[...], lgv, lb_ref[...])
        vn_scr[...] = vn.astype(BF16)
        for g in range(G):
            cols = slice(g * gd, (g + 1) * gd)
            vnb = vn_scr[:, cols]
            mixed = jnp.dot(wm_ref[g], vnb, preferred_element_type=F32) + bt_ref[:, g:g + 1]
            u, du = _gelu_and_grad(zu_ref[:, cols])
            dy = dya_ref[:, cols]
            dz_ref[:, cols] = (dy * mixed * du).astype(BF16)
            dmix = dy * u
            dmb = dmix.astype(BF16)
            dbt_ref[:, g:g + 1] += jnp.sum(dmix, axis=1, keepdims=True)
            dwm_ref[g] += lax.dot_general(dmb, vnb, (((1,), (1,)), ((), ())), preferred_element_type=F32)
            dvn_scr[:, cols] = jnp.dot(wmt_ref[g], dmb, preferred_element_type=F32)
        dvn = dvn_scr[...]
        dlg_ref[...] += jnp.sum(dvn * vhat, axis=0, keepdims=True)
        dlb_ref[...] += jnp.sum(dvn, axis=0, keepdims=True)
        dvh = dvn * lgv
        dv = rstd * (dvh - jnp.mean(dvh, axis=-1, keepdims=True)
                     - vhat * jnp.mean(dvh * vhat, axis=-1, keepdims=True))
        _, dgv = _gelu_and_grad(zv_ref[...])
        dz_ref[:, W:] = (dv * dgv).astype(BF16)

        @pl.when(i == n_steps - 1)
        def _():
            row = lax.broadcasted_iota(jnp.int32, (C, C), 0)
            col = lax.broadcasted_iota(jnp.int32, (C, C), 1)
            for g in range(G):
                dwm_ref[g] = jnp.where(row >= col, dwm_ref[g], 0.0)

    vec = pl.BlockSpec((1, W), lambda i: (0, 0))
    mat = pl.BlockSpec((G, C, C), lambda i: (0, 0, 0))
    bts = pl.BlockSpec((C, G), lambda i: (0, 0))
    return pl.pallas_call(
        body, name=name, grid=(n_steps,),
        in_specs=[pl.BlockSpec((C, W), lambda i: (i, 0)), pl.BlockSpec((C, W), lambda i: (i, 1)),
                  pl.BlockSpec((C, W), lambda i: (i, 0)), vec, vec, mat, mat, bts],
        out_specs=[pl.BlockSpec((C, 2 * W), lambda i: (i, 0)), mat, bts, vec, vec],
        out_shape=[jax.ShapeDtypeStruct((T, 2 * W), BF16), jax.ShapeDtypeStruct((G, C, C), F32),
                   jax.ShapeDtypeStruct((C, G), F32), jax.ShapeDtypeStruct((1, W), F32),
                   jax.ShapeDtypeStruct((1, W), F32)],
        scratch_shapes=[pltpu.VMEM((C, W), BF16), pltpu.VMEM((C, W), F32)],
        compiler_params=_cp("arbitrary"),
    )(proj, proj, dya, lg, lb, wm, wmt, bt)


def _rows_with_prev(ref, r0, rows, ci):
    p0 = pl.multiple_of(jnp.maximum(r0 - SUBLANES, 0), SUBLANES)
    prev = jnp.where(ci > 0, ref[pl.ds(p0, SUBLANES), :], 0.0)
    return jnp.concatenate([prev, ref[pl.ds(r0, rows), :]], axis=0)


def _rows_with_next(ref, r0, rows, ci, n_chunks, total):
    n0 = pl.multiple_of(jnp.minimum(r0 + rows, total - SUBLANES), SUBLANES)
    nxt = jnp.where(ci < n_chunks - 1, ref[pl.ds(n0, SUBLANES), :], 0.0)
    return jnp.concatenate([ref[pl.ds(r0, rows), :], nxt], axis=0)


def _delayed(xx, k, rows):
    if k == 0:
        return xx[SUBLANES:, :]
    return pltpu.roll(xx, k, 0)[SUBLANES:, :]


def _advanced(xx, k, rows):
    if k == 0:
        return xx[:rows, :]
    return pltpu.roll(xx, rows + SUBLANES - k, 0)[:rows, :]


def _conv_chunk(x_ref, w_ref, b_ref, r0, rows, ci):
    K = w_ref.shape[0]
    xx = _rows_with_prev(x_ref, r0, rows, ci)
    out = _delayed(xx, K - 1, rows) * w_ref[0:1, :]
    for k in range(1, K):
        out = out + _delayed(xx, K - 1 - k, rows) * w_ref[k:k + 1, :]
    return out + b_ref[...]


def _ffn_act_fwd(up0, cw, cb, *, batch, name):
    T, F2 = up0.shape
    F = F2 // 2
    S = T // batch
    K = cw.shape[0]
    cbk = _blk(F, 512)
    nj = F // cbk
    R = min(64, S // 2)
    n_chunks = S // R

    def body(ug_ref, uv_ref, wg_ref, wv_ref, bg_ref, bv_ref, act_ref):
        def chunk(ci, carry):
            r0 = pl.multiple_of(ci * R, R)
            cg = _conv_chunk(ug_ref, wg_ref, bg_ref, r0, R, ci)
            cv = _conv_chunk(uv_ref, wv_ref, bv_ref, r0, R, ci)
            act_ref[pl.ds(r0, R), :] = (_gelu(cg) * cv).astype(BF16)
            return carry

        lax.fori_loop(0, n_chunks, chunk, 0)

    return pl.pallas_call(
        body, name=name, grid=(nj, batch),
        in_specs=[pl.BlockSpec((S, cbk), lambda j, b: (b, j)), pl.BlockSpec((S, cbk), lambda j, b: (b, nj + j)),
                  pl.BlockSpec((K, cbk), lambda j, b: (0, j)), pl.BlockSpec((K, cbk), lambda j, b: (0, nj + j)),
                  pl.BlockSpec((1, cbk), lambda j, b: (0, j)), pl.BlockSpec((1, cbk), lambda j, b: (0, nj + j))],
        out_specs=pl.BlockSpec((S, cbk), lambda j, b: (b, j)),
        out_shape=jax.ShapeDtypeStruct((T, F), BF16),
        compiler_params=_cp("parallel", "parallel"),
    )(up0, up0, cw, cw, cb, cb)


def _ffn_act_bwd(up0, cw, cb, dact, *, batch, name):
    T, F2 = up0.shape
    F = F2 // 2
    S = T // batch
    K = cw.shape[0]
    cbk = _blk(F, 512)
    nj = F // cbk
    R = min(64, S // 2)
    n_chunks = S // R

    def body(ug_ref, uv_ref, wg_ref, wv_ref, bg_ref, bv_ref, da_ref,
             dug_ref, duv_ref, dw_g_ref, dw_v_ref, db_g_ref, db_v_ref, dcg_scr, dcv_scr):
        b = pl.program_id(1)

        def chunk_a(ci, acc):
            r0 = pl.multiple_of(ci * R, R)
            xg = _rows_with_prev(ug_ref, r0, R, ci)
            xv = _rows_with_prev(uv_ref, r0, R, ci)
            dg_taps = [_delayed(xg, K - 1 - k, R) for k in range(K)]
            dv_taps = [_delayed(xv, K - 1 - k, R) for k in range(K)]
            cg = dg_taps[0] * wg_ref[0:1, :]
            cv = dv_taps[0] * wv_ref[0:1, :]
            for k in range(1, K):
                cg = cg + dg_taps[k] * wg_ref[k:k + 1, :]
                cv = cv + dv_taps[k] * wv_ref[k:k + 1, :]
            cg = cg + bg_ref[...]
            cv = cv + bv_ref[...]
            gl, dgl = _gelu_and_grad(cg)
            da = da_ref[pl.ds(r0, R), :]
            dcg = da * cv * dgl
            dcv = da * gl
            dcg_scr[pl.ds(r0, R), :] = dcg
            dcv_scr[pl.ds(r0, R), :] = dcv
            new = []
            for k in range(K):
                new.append(acc[k] + jnp.sum(dcg * dg_taps[k], axis=0, keepdims=True))
            for k in range(K):
                new.append(acc[K + k] + jnp.sum(dcv * dv_taps[k], axis=0, keepdims=True))
            new.append(acc[2 * K] + jnp.sum(dcg, axis=0, keepdims=True))
            new.append(acc[2 * K + 1] + jnp.sum(dcv, axis=0, keepdims=True))
            return tuple(new)

        zero = jnp.zeros((1, cbk), F32)
        acc = lax.fori_loop(0, n_chunks, chunk_a, (zero,) * (2 * K + 2))

        def chunk_b(ci, carry):
            r0 = pl.multiple_of(ci * R, R)
            dg = _rows_with_next(dcg_scr, r0, R, ci, n_chunks, S)
            dv = _rows_with_next(dcv_scr, r0, R, ci, n_chunks, S)
            og = _advanced(dg, 0, R) * wg_ref[K - 1:K, :]
            ov = _advanced(dv, 0, R) * wv_ref[K - 1:K, :]
            for j in range(1, K):
                og = og + _advanced(dg, j, R) * wg_ref[K - 1 - j:K - j, :]
                ov = ov + _advanced(dv, j, R) * wv_ref[K - 1 - j:K - j, :]
            dug_ref[pl.ds(r0, R), :] = og.astype(BF16)
            duv_ref[pl.ds(r0, R), :] = ov.astype(BF16)
            return carry

        lax.fori_loop(0, n_chunks, chunk_b, 0)

        @pl.when(b == 0)
        def _():
            for k in range(K):
                dw_g_ref[k:k + 1, :] = acc[k]
                dw_v_ref[k:k + 1, :] = acc[K + k]
            db_g_ref[...] = acc[2 * K]
            db_v_ref[...] = acc[2 * K + 1]

        @pl.when(b > 0)
        def _():
            for k in range(K):
                dw_g_ref[k:k + 1, :] += acc[k]
                dw_v_ref[k:k + 1, :] += acc[K + k]
            db_g_ref[...] += acc[2 * K]
            db_v_ref[...] += acc[2 * K + 1]

    seq = pl.BlockSpec((S, cbk), lambda j, b: (b, j))
    wk = pl.BlockSpec((K, cbk), lambda j, b: (0, j))
    w1 = pl.BlockSpec((1, cbk), lambda j, b: (0, j))
    outs = pl.pallas_call(
        body, name=name, grid=(nj, batch),
        in_specs=[seq, pl.BlockSpec((S, cbk), lambda j, b: (b, nj + j)),
                  wk, pl.BlockSpec((K, cbk), lambda j, b: (0, nj + j)),
                  w1, pl.BlockSpec((1, cbk), lambda j, b: (0, nj + j)), seq],
        out_specs=[seq, seq, wk, wk, w1, w1],
        out_shape=[jax.ShapeDtypeStruct((T, F), BF16), jax.ShapeDtypeStruct((T, F), BF16),
                   jax.ShapeDtypeStruct((K, F), F32), jax.ShapeDtypeStruct((K, F), F32),
                   jax.ShapeDtypeStruct((1, F), F32), jax.ShapeDtypeStruct((1, F), F32)],
        scratch_shapes=[pltpu.VMEM((S, cbk), F32), pltpu.VMEM((S, cbk), F32)],
        compiler_params=_cp("parallel", "arbitrary"),
    )(up0, up0, cw, cw, cb, cb, dact)
    dug, duv, dwg, dwv, dbg, dbv = outs
    return dug, duv, jnp.concatenate([dwg, dwv], axis=1), jnp.concatenate([dbg, dbv], axis=1)


def _lru_gate_rows(xr_ref, cw_ref, cb_ref, wa_ref, wx_ref, ba_ref, bx_ref, xc_scr, za_scr, zx_scr, S, R):
    def chunk(ci, carry):
        r0 = pl.multiple_of(ci * R, R)
        xc = _conv_chunk(xr_ref, cw_ref, cb_ref, r0, R, ci)
        xc_scr[pl.ds(r0, R), :] = xc
        xb = xc.astype(BF16)
        za_scr[pl.ds(r0, R), :] = jnp.dot(xb, wa_ref[...], preferred_element_type=F32) + ba_ref[...]
        zx_scr[pl.ds(r0, R), :] = jnp.dot(xb, wx_ref[...], preferred_element_type=F32) + bx_ref[...]
        return carry

    lax.fori_loop(0, S // R, chunk, 0)


def _lru_gates(za, zx, sp):
    ra = _sigmoid(za)
    ig = _sigmoid(zx)
    la = -LRU_C * ra * sp
    a = jnp.exp(la)
    s = jnp.sqrt(_neg_expm1(2.0 * la))
    return ra, ig, a, s


def _lru_fwd(proj, cw, cb, wa, wx, ba, bx, lam, *, batch, off_x, name, carry=None):
    T = proj.shape[0]
    H, Dh, _ = wa.shape
    W = H * Dh
    S = T // batch
    K = cw.shape[0]
    ox, oy = off_x // Dh, (off_x + W) // Dh
    R = min(256, S // 2)
    n16 = S // BF16_ROWS

    def body(xr_ref, yr_ref, cw_ref, cb_ref, wa_ref, wx_ref, ba_ref, bx_ref, lam_ref,
             yb_ref, h_ref, xc_scr, za_scr, zx_scr):
        _lru_gate_rows(xr_ref, cw_ref, cb_ref, wa_ref, wx_ref, ba_ref, bx_ref, xc_scr, za_scr, zx_scr, S, R)
        sp = _softplus(-lam_ref[...])
        row = lax.broadcasted_iota(jnp.int32, (SUBLANES, Dh), 0)

        def tile(r0, carry):
            rows = pl.ds(r0, SUBLANES)
            xc = xc_scr[rows, :]
            _, ig, a, s = _lru_gates(za_scr[rows, :], zx_scr[rows, :], sp)
            A, B = a, s * (ig * xc)
            for d in (1, 2, 4):
                m = row >= d
                Bs = pltpu.roll(B, d, 0)
                As = pltpu.roll(A, d, 0)
                B = jnp.where(m, B + A * Bs, B)
                A = jnp.where(m, A * As, A)
            hh = B + A * carry
            h_ref[rows, :] = hh
            return hh, hh[SUBLANES - 1:SUBLANES, :]

        def step(i, carry):
            r0 = pl.multiple_of(i * BF16_ROWS, BF16_ROWS)
            h0, carry = tile(r0, carry)
            h1, carry = tile(r0 + SUBLANES, carry)
            hh = jnp.concatenate([h0, h1], axis=0)
            yb_ref[pl.ds(r0, BF16_ROWS), :] = (hh * _gelu(yr_ref[pl.ds(r0, BF16_ROWS), :])).astype(BF16)
            return carry

        lax.fori_loop(0, n16, step, jnp.zeros((1, Dh), F32))

    vec = pl.BlockSpec((1, Dh), lambda b, h: (0, h))
    wsp = pl.BlockSpec((None, Dh, Dh), lambda b, h: (h, 0, 0))
    seq = pl.BlockSpec((S, Dh), lambda b, h: (b, h))
    outs, carried = _carried_call(
        body, name=name, grid=(batch, H),
        in_specs=[pl.BlockSpec((S, Dh), lambda b, h: (b, ox + h)), pl.BlockSpec((S, Dh), lambda b, h: (b, oy + h)),
                  pl.BlockSpec((K, Dh), lambda b, h: (0, h)), vec, wsp, wsp, vec, vec, vec],
        out_specs=[seq, seq],
        out_shape=[jax.ShapeDtypeStruct((T, W), BF16), jax.ShapeDtypeStruct((T, W), F32)],
        scratch_shapes=[pltpu.VMEM((S, Dh), F32)] * 3,
        args=[proj, proj, cw, cb, wa, wx, ba, bx, lam], semantics=("parallel", "parallel"), carry=carry)
    return (outs[0], outs[1]) if carry is None else (outs[0], outs[1], carried)


def _lru_bwd(proj, hseq, dyb, cw, cb, wa, wx, wat, wxt, ba, bx, lam, *, batch, off_x, name):
    T = proj.shape[0]
    H, Dh, _ = wa.shape
    W = H * Dh
    S = T // batch
    K = cw.shape[0]
    ox, oy = off_x // Dh, (off_x + W) // Dh
    R = min(256, S // 2)
    n_chunks = S // R
    n16 = S // BF16_ROWS

    def body(xr_ref, yr_ref, h_ref, dyb_ref, cw_ref, cb_ref, wa_ref, wx_ref, wat_ref, wxt_ref,
             ba_ref, bx_ref, lam_ref,
             dxr_ref, dyr_ref, dwa_ref, dwx_ref, dcw_ref, dcb_ref, dba_ref, dbx_ref, dlam_ref,
             xc_scr, za_scr, zx_scr, dza_scr, dzx_scr, dxc_scr):
        b = pl.program_id(1)
        _lru_gate_rows(xr_ref, cw_ref, cb_ref, wa_ref, wx_ref, ba_ref, bx_ref, xc_scr, za_scr, zx_scr, S, R)
        lam_v = lam_ref[...]
        sp = _softplus(-lam_v)
        row = lax.broadcasted_iota(jnp.int32, (SUBLANES, Dh), 0)

        def tile(r0, carry):
            a_next, g_next, s_ba, s_bx, s_lam = carry
            rows = pl.ds(r0, SUBLANES)
            xc = xc_scr[rows, :]
            ra, ig, a, s = _lru_gates(za_scr[rows, :], zx_scr[rows, :], sp)
            hh = h_ref[rows, :]
            gy, dgy = _gelu_and_grad(yr_ref[rows, :])
            dy = dyb_ref[rows, :]
            dyr = dy * hh * dgy
            C = jnp.where(row == SUBLANES - 1, a_next, pltpu.roll(a, SUBLANES - 1, 0))
            B = dy * gy
            for d in (1, 2, 4):
                m = row < SUBLANES - d
                Bs = pltpu.roll(B, SUBLANES - d, 0)
                Cs = pltpu.roll(C, SUBLANES - d, 0)
                B = jnp.where(m, B + C * Bs, B)
                C = jnp.where(m, C * Cs, C)
            G = B + C * g_next
            p0 = pl.multiple_of(jnp.maximum(r0 - SUBLANES, 0), SUBLANES)
            h_before = jnp.where(r0 > 0, h_ref[pl.ds(p0, SUBLANES), :][SUBLANES - 1:SUBLANES, :], 0.0)
            h_prev = jnp.where(row == 0, h_before, pltpu.roll(hh, 1, 0))
            da = G * h_prev
            dig = G * s * xc
            ds = G * ig * xc
            dxc_scr[rows, :] = G * s * ig
            dla = da * a - ds * (a * a) / s
            dza = dla * (-LRU_C * sp) * ra * (1.0 - ra)
            dzx = dig * ig * (1.0 - ig)
            dza_scr[rows, :] = dza
            dzx_scr[rows, :] = dzx
            carry = (a[0:1, :], G[0:1, :], s_ba + dza, s_bx + dzx, s_lam + dla * ra)
            return dyr, carry

        def step(it, carry):
            r0 = pl.multiple_of((n16 - 1 - it) * BF16_ROWS, BF16_ROWS)
            d1, carry = tile(r0 + SUBLANES, carry)
            d0, carry = tile(r0, carry)
            dyr_ref[pl.ds(r0, BF16_ROWS), :] = jnp.concatenate([d0, d1], axis=0).astype(BF16)
            return carry

        z1 = jnp.zeros((1, Dh), F32)
        z8 = jnp.zeros((SUBLANES, Dh), F32)
        _, _, s_ba, s_bx, s_lam = lax.fori_loop(0, n16, step, (z1, z1, z8, z8, z8))
        dba = jnp.sum(s_ba, axis=0, keepdims=True)
        dbx = jnp.sum(s_bx, axis=0, keepdims=True)
        dlam = jnp.sum(s_lam, axis=0, keepdims=True) * (LRU_C * _sigmoid(-lam_v))

        @pl.when(b == 0)
        def _():
            dwa_ref[...] = jnp.zeros_like(dwa_ref)
            dwx_ref[...] = jnp.zeros_like(dwx_ref)

        def chunk_c(ci, carry):
            r0 = pl.multiple_of(ci * R, R)
            rows = pl.ds(r0, R)
            xb = xc_scr[rows, :].astype(BF16)
            dzab = dza_scr[rows, :].astype(BF16)
            dzxb = dzx_scr[rows, :].astype(BF16)
            dwa_ref[...] += lax.dot_general(xb, dzab, (((0,), (0,)), ((), ())), preferred_element_type=F32)
            dwx_ref[...] += lax.dot_general(xb, dzxb, (((0,), (0,)), ((), ())), preferred_element_type=F32)
            dxc_scr[rows, :] += (jnp.dot(dzab, wat_ref[...], preferred_element_type=F32)
                                 + jnp.dot(dzxb, wxt_ref[...], preferred_element_type=F32))
            return carry

        lax.fori_loop(0, n_chunks, chunk_c, 0)

        def chunk_d(ci, acc):
            r0 = pl.multiple_of(ci * R, R)
            dd = _rows_with_next(dxc_scr, r0, R, ci, n_chunks, S)
            xx = _rows_with_prev(xr_ref, r0, R, ci)
            dxc = dd[:R, :]
            out = dxc * cw_ref[K - 1:K, :]
            for j in range(1, K):
                out = out + _advanced(dd, j, R) * cw_ref[K - 1 - j:K - j, :]
            dxr_ref[pl.ds(r0, R), :] = out.astype(BF16)
            new = [acc[k] + jnp.sum(dxc * _delayed(xx, K - 1 - k, R), axis=0, keepdims=True) for k in range(K)]
            new.append(acc[K] + jnp.sum(dxc, axis=0, keepdims=True))
            return tuple(new)

        acc = lax.fori_loop(0, n_chunks, chunk_d, (z1,) * (K + 1))

        @pl.when(b == 0)
        def _():
            for k in range(K):
                dcw_ref[k:k + 1, :] = acc[k]
            dcb_ref[...] = acc[K]
            dba_ref[...] = dba
            dbx_ref[...] = dbx
            dlam_ref[...] = dlam

        @pl.when(b > 0)
        def _():
            for k in range(K):
                dcw_ref[k:k + 1, :] += acc[k]
            dcb_ref[...] += acc[K]
            dba_ref[...] += dba
            dbx_ref[...] += dbx
            dlam_ref[...] += dlam

    vec = pl.BlockSpec((1, Dh), lambda h, b: (0, h))
    wsp = pl.BlockSpec((None, Dh, Dh), lambda h, b: (h, 0, 0))
    seq = pl.BlockSpec((S, Dh), lambda h, b: (b, h))
    ck = pl.BlockSpec((K, Dh), lambda h, b: (0, h))
    row_out = jax.ShapeDtypeStruct((1, W), F32)
    return pl.pallas_call(
        body, name=name, grid=(H, batch),
        in_specs=[pl.BlockSpec((S, Dh), lambda h, b: (b, ox + h)), pl.BlockSpec((S, Dh), lambda h, b: (b, oy + h)),
                  seq, seq, ck, vec, wsp, wsp, wsp, wsp, vec, vec, vec],
        out_specs=[seq, seq, wsp, wsp, ck, vec, vec, vec, vec],
        out_shape=[jax.ShapeDtypeStruct((T, W), BF16), jax.ShapeDtypeStruct((T, W), BF16),
                   jax.ShapeDtypeStruct((H, Dh, Dh), F32), jax.ShapeDtypeStruct((H, Dh, Dh), F32),
                   jax.ShapeDtypeStruct((K, W), F32), row_out, row_out, row_out, row_out],
        scratch_shapes=[pltpu.VMEM((S, Dh), F32)] * 6,
        compiler_params=_cp("parallel", "arbitrary"),
    )(proj, proj, hseq, dyb, cw, cb, wa, wx, wat, wxt, ba, bx, lam)


def _adamw(w, g, m, v, *, name):
    R, C = w.shape
    tr, tc = _blk(R, 256), _blk(C, 1024)

    def body(w_ref, g_ref, m_ref, v_ref, d_ref, nm_ref, nv_ref):
        gv = g_ref[...]
        nm = ADAM_B1 * m_ref[...] + (1.0 - ADAM_B1) * gv
        nv = ADAM_B2 * v_ref[...] + (1.0 - ADAM_B2) * (gv * gv)
        m_hat = nm / (1.0 - ADAM_B1 ** ADAM_STEP)
        v_hat = nv / (1.0 - ADAM_B2 ** ADAM_STEP)
        d_ref[...] = -ADAM_LR * (m_hat / (jnp.sqrt(v_hat) + ADAM_EPS) + ADAM_WD * w_ref[...])
        nm_ref[...] = nm
        nv_ref[...] = nv

    blk = pl.BlockSpec((tr, tc), lambda i, j: (i, j))
    out = jax.ShapeDtypeStruct((R, C), F32)
    return pl.pallas_call(
        body, name=name, grid=(R // tr, C // tc), in_specs=[blk] * 4, out_specs=[blk] * 3,
        out_shape=[out] * 3, compiler_params=_cp("parallel", "parallel"),
    )(w, g, m, v)


def _to_slab(a, pos, dtype, *, name, b=None):
    R, C = a.shape
    tr, tc = _blk(R, 512), _blk(C, 1024)

    def body(p_ref, *refs):
        v = refs[0][...]
        if b is not None:
            v = v + refs[1][...]
        refs[-1][...] = v.astype(dtype)

    blk = pl.BlockSpec((tr, tc), lambda i, j, p_ref: (i, j))
    return pl.pallas_call(
        body, name=name,
        grid_spec=pltpu.PrefetchScalarGridSpec(
            num_scalar_prefetch=1, grid=(R // tr, C // tc), in_specs=[blk] * (1 if b is None else 2),
            out_specs=pl.BlockSpec((None, tr, tc), lambda i, j, p_ref: (p_ref[0], i, j))),
        out_shape=jax.ShapeDtypeStruct((N_CHIPS, R, C), dtype),
        compiler_params=_cp("parallel", "parallel"),
    )(pos, a, *([] if b is None else [b]))


def _sum_chips(q, *, name):
    _, R, C = q.shape
    tr = _blk(R, 1024)

    def body(q_ref, o_ref):
        o_ref[...] = ((q_ref[0] + q_ref[1]) + q_ref[2]) + q_ref[3]

    return pl.pallas_call(body, name=name, grid=(R // tr,),
                          in_specs=[pl.BlockSpec((N_CHIPS, tr, C), lambda i: (0, i, 0))],
                          out_specs=pl.BlockSpec((tr, C), lambda i: (i, 0)),
                          out_shape=jax.ShapeDtypeStruct((R, C), q.dtype), compiler_params=_cp("parallel"))(q)


def _pair_swap_halves(g, *, name):
    n, R, C = g.shape
    hr = R // 2

    def body(g_ref, o_ref, send_sem, recv_sem):
        x, y, c = _mesh_pos()
        cp = pltpu.make_async_remote_copy(
            src_ref=g_ref.at[:, pl.ds((1 - c) * hr, hr), :], dst_ref=o_ref, send_sem=send_sem, recv_sem=recv_sem,
            device_id=(x, y, 1 - c), device_id_type=MESH)
        cp.start()
        cp.wait()

    return pl.pallas_call(
        body, name=name, in_specs=[ANY], out_specs=ANY,
        out_shape=jax.ShapeDtypeStruct((n, hr, C), g.dtype),
        scratch_shapes=[pltpu.SemaphoreType.DMA, pltpu.SemaphoreType.DMA],
    )(g)


def _pair_add_halves(g, rb, cpos, *, name):
    n, R, C = g.shape
    hr = R // 2
    tr, tc = _blk(hr, 512), _blk(C, 1024)
    nrb = hr // tr

    def body(c_ref, g_ref, r_ref, o_ref):
        o_ref[...] = (g_ref[...].astype(F32) + r_ref[...].astype(F32)).astype(o_ref.dtype)

    return pl.pallas_call(
        body, name=name,
        grid_spec=pltpu.PrefetchScalarGridSpec(
            num_scalar_prefetch=1, grid=(n, nrb, C // tc),
            in_specs=[pl.BlockSpec((None, tr, tc), lambda s, i, j, c_ref: (s, c_ref[0] * nrb + i, j)),
                      pl.BlockSpec((None, tr, tc), lambda s, i, j, c_ref: (s, i, j))],
            out_specs=pl.BlockSpec((None, tr, tc), lambda s, i, j, c_ref: (s, i, j))),
        out_shape=jax.ShapeDtypeStruct((n, hr, C), g.dtype),
        compiler_params=_cp("parallel", "parallel", "parallel"),
    )(cpos, g, rb)


def _chip_final_add(p, q, pos, *, name):
    _, hr, C = p.shape
    tr, tc = _blk(hr, 512), _blk(C, 1024)
    nrb = hr // tr

    def body(k_ref, p_ref, q_ref, o_ref):
        o_ref[...] = ((p_ref[...].astype(F32) + q_ref[0].astype(F32)) + q_ref[1].astype(F32)) + q_ref[2].astype(F32)

    return pl.pallas_call(
        body, name=name,
        grid_spec=pltpu.PrefetchScalarGridSpec(
            num_scalar_prefetch=1, grid=(nrb, C // tc),
            in_specs=[pl.BlockSpec((None, tr, tc), lambda i, j, k_ref: (k_ref[0], i, j)),
                      pl.BlockSpec((N_CHIPS - 1, tr, tc), lambda i, j, k_ref: (0, i, j))],
            out_specs=pl.BlockSpec((tr, tc), lambda i, j, k_ref: (k_ref[1] * nrb + i, j))),
        out_shape=jax.ShapeDtypeStruct((2 * hr, C), F32),
        compiler_params=_cp("parallel", "parallel"),
    )(pos, p, q)


def _pair_swap(v, *, name):
    def body(v_ref, o_ref, send_sem, recv_sem):
        x, y, c = _mesh_pos()
        cp = pltpu.make_async_remote_copy(src_ref=v_ref, dst_ref=o_ref, send_sem=send_sem, recv_sem=recv_sem,
                                          device_id=(x, y, 1 - c), device_id_type=MESH)
        cp.start()
        cp.wait()

    return pl.pallas_call(
        body, name=name, in_specs=[ANY], out_specs=ANY, out_shape=jax.ShapeDtypeStruct(v.shape, v.dtype),
        scratch_shapes=[pltpu.SemaphoreType.DMA, pltpu.SemaphoreType.DMA],
    )(v)


def _pair_sum(g, cpos, *, tag):
    rb = _pair_swap_halves(g, name=f"rs_pair_swap_{tag}")
    return _pair_add_halves(g, rb, cpos, name=f"rs_pair_add_{tag}")


def _all_reduce(v, pos, *, tag):
    other = _pair_swap(v, name=f"ar_pair_swap_{tag}")
    slabs = _to_slab(v, pos, F32, b=other, name=f"ar_pair_add_{tag}")
    slabs, = _comm_call(_Carry(_GatherSlabs(slabs)), name=f"ar_allgather_{tag}")
    return _sum_chips(slabs, name=f"ar_sum_{tag}")


def _pack(arrays, unit, total_unit=None):
    parts, n = [], 0
    for a in arrays:
        flat = a.reshape(-1)
        pad = (-flat.shape[0]) % unit
        parts.append(jnp.pad(flat, (0, pad)) if pad else flat)
        n += flat.shape[0] + pad
    if total_unit and n % total_unit:
        parts.append(jnp.zeros((-n) % total_unit, arrays[0].dtype))
    return jnp.concatenate(parts).reshape(-1, LANES)


def _unpack(packed, shapes, unit):
    lead = packed.shape[:-2]
    flat = packed.reshape(lead + (-1,))
    out, pos = [], 0
    for shp in shapes:
        n = math.prod(shp)
        out.append(flat[..., pos:pos + n].reshape(lead + tuple(shp)))
        pos += n + (-n) % unit
    return out


def kernel(x, g_mix, w_in, sg_ln_g, sg_ln_b, sg_w, sg_b, lru_conv_w, lru_conv_b, lru_wa, lru_ba, lru_wx, lru_bx, lru_lam, p_sg, p_lru, w_out, g_ffn, w_up, ffn_conv_w, ffn_conv_b, w_down, g_final, loss_target, m_g_mix, m_w_in, m_sg_ln_g, m_sg_ln_b, m_sg_w, m_sg_b, m_lru_conv_w, m_lru_conv_b, m_lru_wa, m_lru_ba, m_lru_wx, m_lru_bx, m_lru_lam, m_p_sg, m_p_lru, m_w_out, m_g_ffn, m_w_up, m_ffn_conv_w, m_ffn_conv_b, m_w_down, m_g_final, v_g_mix, v_w_in, v_sg_ln_g, v_sg_ln_b, v_sg_w, v_sg_b, v_lru_conv_w, v_lru_conv_b, v_lru_wa, v_lru_ba, v_lru_wx, v_lru_bx, v_lru_lam, v_p_sg, v_p_lru, v_w_out, v_g_ffn, v_w_up, v_ffn_conv_w, v_ffn_conv_b, v_w_down, v_g_final):
    params = dict(g_mix=g_mix, w_in=w_in, sg_ln_g=sg_ln_g, sg_ln_b=sg_ln_b, sg_w=sg_w, sg_b=sg_b,
                  lru_conv_w=lru_conv_w, lru_conv_b=lru_conv_b, lru_wa=lru_wa, lru_ba=lru_ba, lru_wx=lru_wx,
                  lru_bx=lru_bx, lru_lam=lru_lam, p_sg=p_sg, p_lru=p_lru, w_out=w_out, g_ffn=g_ffn, w_up=w_up,
                  ffn_conv_w=ffn_conv_w, ffn_conv_b=ffn_conv_b, w_down=w_down, g_final=g_final)
    mom1 = dict(g_mix=m_g_mix, w_in=m_w_in, sg_ln_g=m_sg_ln_g, sg_ln_b=m_sg_ln_b, sg_w=m_sg_w, sg_b=m_sg_b,
                lru_conv_w=m_lru_conv_w, lru_conv_b=m_lru_conv_b, lru_wa=m_lru_wa, lru_ba=m_lru_ba,
                lru_wx=m_lru_wx, lru_bx=m_lru_bx, lru_lam=m_lru_lam, p_sg=m_p_sg, p_lru=m_p_lru, w_out=m_w_out,
                g_ffn=m_g_ffn, w_up=m_w_up, ffn_conv_w=m_ffn_conv_w, ffn_conv_b=m_ffn_conv_b, w_down=m_w_down,
                g_final=m_g_final)
    mom2 = dict(g_mix=v_g_mix, w_in=v_w_in, sg_ln_g=v_sg_ln_g, sg_ln_b=v_sg_ln_b, sg_w=v_sg_w, sg_b=v_sg_b,
                lru_conv_w=v_lru_conv_w, lru_conv_b=v_lru_conv_b, lru_wa=v_lru_wa, lru_ba=v_lru_ba,
                lru_wx=v_lru_wx, lru_bx=v_lru_bx, lru_lam=v_lru_lam, p_sg=v_p_sg, p_lru=v_p_lru, w_out=v_w_out,
                g_ffn=v_g_ffn, w_up=v_w_up, ffn_conv_w=v_ffn_conv_w, ffn_conv_b=v_ffn_conv_b, w_down=v_w_down,
                g_final=v_g_final)
    names = list(params)
    big = ["w_in", "p_sg", "p_lru", "w_out", "w_up", "w_down"]
    col_sharded = {"w_in", "p_sg", "w_up"}
    small = [n for n in names if n not in big]

    batch, S, D = x.shape
    T = batch * S
    W_sg = sg_ln_g.shape[-1]
    H, _, Dh = lru_wa.shape[1:]
    W_lru = H * Dh
    K_lru = lru_conv_w.shape[1]
    K_ffn = ffn_conv_w.shape[1]
    F2 = ffn_conv_b.shape[-1]
    off_lru = 2 * W_sg
    off_gate = off_lru + 2 * W_lru

    cx, cy, cc = _mesh_pos()
    chip = 2 * cx + cy
    cpos = jnp.reshape(cc, (1,)).astype(jnp.int32)
    pos = jnp.stack([chip, cc]).astype(jnp.int32)

    xf = x.reshape(T, D)
    tgt = loss_target.reshape(T, D)

    wb = {n: _to_slab(params[n][0], pos, BF16, name=f"cast_{n}") for n in big}
    rows = {n: wb[n].shape[1] for n in big}
    sharded_small = ["lru_conv_w", "ffn_conv_w", "lru_wa", "lru_wx"]
    unit_g = 2 * BF16_ROWS * LANES
    pack_g = 256 * LANES
    sm_shapes = [params[n][0].shape for n in sharded_small]
    sm = _to_slab(_pack([params[n][0] for n in sharded_small], unit_g, pack_g), pos, F32, name="slab_small")

    def gather(n, lo, hi):
        return _GatherRows(wb[n], rows[n] * lo // 8, rows[n] * hi // 8)

    w_in_g, sm = _comm_call(_Carry(gather("w_in", 0, 8), _GatherRows(sm, 0, sm.shape[1])), name="gather_first")
    cwl_s, cwf_s, wa_s, wx_s = _unpack(sm, sm_shapes, unit_g)
    lru_cw = jnp.transpose(cwl_s, (1, 0, 2)).reshape(K_lru, W_lru)
    ffn_cw = jnp.transpose(cwf_s, (1, 0, 2)).reshape(K_ffn, F2)
    wa_full = jnp.transpose(wa_s, (1, 0, 2, 3)).reshape(H, Dh, Dh)
    wx_full = jnp.transpose(wx_s, (1, 0, 2, 3)).reshape(H, Dh, Dh)
    wa_b, wx_b = wa_full.astype(BF16), wx_full.astype(BF16)
    wat_b, wxt_b = jnp.swapaxes(wa_b, 1, 2), jnp.swapaxes(wx_b, 1, 2)

    wm, wmt = _sg_mask(sg_w[0], name="sg_mask")
    bt = sg_b[0].T

    h1 = _rms_fwd(xf, g_mix, name="rms1_fwd")
    proj, (p_sg_g, p_lru_g, w_out_g, wb["w_up"]) = _mm_nn(
        h1, w_in_g, out_dtype=F32, name="mm_proj",
        carry=_Carry(gather("p_sg", 0, 8), gather("p_lru", 0, 8), gather("w_out", 0, 8), gather("w_up", 0, 2)))
    p_lru_g = p_lru_g.reshape(-1, D)
    w_out_g = w_out_g.reshape(-1, D)
    y_a = _sg_fwd(proj, sg_ln_g, sg_ln_b, wm, bt, name="sg_fwd")
    y_b, hseq, (wb["w_up"],) = _lru_fwd(proj, lru_cw, lru_conv_b, wa_b, wx_b, lru_ba, lru_bx, lru_lam,
                                        batch=batch, off_x=off_lru, name="lru_fwd",
                                        carry=_Carry(gather("w_up", 2, 4)))
    pa = _mm_nn(y_a, p_sg_g, out_dtype=F32, name="mm_pa")
    pb, (wb["w_up"],) = _mm_nn(y_b, p_lru_g, out_dtype=F32, name="mm_pb", carry=_Carry(gather("w_up", 4, 5)))
    merged = _merge_fwd(proj, pa, pb, off_a=off_gate, name="merge_fwd")
    x1, (w_up_g,) = _mm_nn(merged, w_out_g, out_dtype=F32, res=xf, name="mm_out",
                           carry=_Carry(gather("w_up", 5, 8)))
    h2 = _rms_fwd(x1, g_ffn, name="rms2_fwd")
    up0, (w_down_g,) = _mm_nn(h2, w_up_g, out_dtype=F32, name="mm_up", carry=_Carry(gather("w_down", 0, 8)))
    w_down_g = w_down_g.reshape(-1, D)
    act = _ffn_act_fwd(up0, ffn_cw, ffn_conv_b, batch=batch, name="ffn_act_fwd")
    x2 = _mm_nn(act, w_down_g, out_dtype=F32, res=x1, name="mm_down")
    lvec, dx2, dx2_b, dg_final = _loss_head(x2, tgt, g_final.reshape(1, D), name="loss_head")
    loss = lax.psum(jnp.sum(lvec) * (0.5 / D), ("x", "y", "c"))

    ps, qs = {}, {}
    g = _mm_tn(act, dx2_b, out_dtype=BF16, name="mm_dw_down").reshape(N_CHIPS, -1, D)
    ps["w_down"] = _pair_sum(g, cpos, tag="w_down")
    dact, (qs["w_down"],) = _mm_nt(dx2_b, w_down_g, out_dtype=F32, name="mm_dact",
                                   carry=_Carry(_ChipExchange(ps["w_down"])))
    dug, duv, d_ffn_cw, d_ffn_cb = _ffn_act_bwd(up0, ffn_cw, ffn_conv_b, dact, batch=batch, name="ffn_act_bwd")
    dup0 = jnp.concatenate([dug, duv], axis=1)
    g = _mm_tn(h2, dup0, out_dtype=BF16, col_shards=N_CHIPS, name="mm_dw_up")
    ps["w_up"] = _pair_sum(g, cpos, tag="w_up")
    dh2, (qs["w_up"],) = _mm_nt(dup0, w_up_g, out_dtype=F32, name="mm_dh2", carry=_Carry(_ChipExchange(ps["w_up"])))
    dx1, dx1_b, dg_ffn = _rms_bwd(x1, g_ffn, dh2, dx2, name="rms2_bwd")
    g = _mm_tn(merged, dx1_b, out_dtype=BF16, name="mm_dw_out").reshape(N_CHIPS, -1, D)
    ps["w_out"] = _pair_sum(g, cpos, tag="w_out")
    dmerged, (qs["w_out"],) = _mm_nt(dx1_b, w_out_g, out_dtype=F32, name="mm_dmerged",
                                     carry=_Carry(_ChipExchange(ps["w_out"])))
    dga, dgb, dpa, dpb = _merge_bwd(proj, pa, pb, dmerged, off_a=off_gate, name="merge_bwd")
    g = _mm_tn(y_a, dpa, out_dtype=BF16, col_shards=N_CHIPS, name="mm_dp_sg")
    ps["p_sg"] = _pair_sum(g, cpos, tag="p_sg")
    g = _mm_tn(y_b, dpb, out_dtype=BF16, name="mm_dp_lru").reshape(N_CHIPS, -1, D)
    ps["p_lru"] = _pair_sum(g, cpos, tag="p_lru")
    dya, (qs["p_sg"],) = _mm_nt(dpa, p_sg_g, out_dtype=F32, name="mm_dya", carry=_Carry(_ChipExchange(ps["p_sg"])))
    dyb, (qs["p_lru"],) = _mm_nt(dpb, p_lru_g, out_dtype=F32, name="mm_dyb",
                                 carry=_Carry(_ChipExchange(ps["p_lru"])))
    dxr, dyr, d_wa, d_wx, d_lru_cw, d_lru_cb, d_ba, d_bx, d_lam = _lru_bwd(
        proj, hseq, dyb, lru_cw, lru_conv_b, wa_b, wx_b, wat_b, wxt_b, lru_ba, lru_bx, lru_lam,
        batch=batch, off_x=off_lru, name="lru_bwd")
    dzuv, d_wm, d_bt, d_lg, d_lb = _sg_bwd(proj, dya, sg_ln_g, sg_ln_b, wm, wmt, bt, name="sg_bwd")
    dproj = jnp.concatenate([dzuv, dxr, dyr, dga, dgb], axis=1)
    g = _mm_tn(h1, dproj, out_dtype=BF16, col_shards=N_CHIPS, name="mm_dw_in")
    ps["w_in"] = _pair_sum(g, cpos, tag="w_in")
    dh1, (qs["w_in"],) = _mm_nt(dproj, w_in_g, out_dtype=F32, name="mm_dh1", carry=_Carry(_ChipExchange(ps["w_in"])))
    dx, _, dg_mix = _rms_bwd(xf, g_mix, dh1, dx1, name="rms1_bwd")

    halves = [_chip_final_add(ps[n], qs[n], pos, name=f"rs_final_add_{n}") for n in big]
    grads = dict(zip(big, _comm_call(_Carry(*[_ShareHalves(h) for h in halves]), name="rs_share")))
    small_full = dict(g_mix=dg_mix, sg_ln_g=d_lg, sg_ln_b=d_lb, sg_w=d_wm, sg_b=d_bt.T, lru_conv_w=d_lru_cw,
                      lru_conv_b=d_lru_cb, lru_wa=d_wa, lru_ba=d_ba, lru_wx=d_wx, lru_bx=d_bx, lru_lam=d_lam,
                      g_ffn=dg_ffn, ffn_conv_w=d_ffn_cw, ffn_conv_b=d_ffn_cb, g_final=dg_final)
    unit_s = SUBLANES * LANES
    pack_s = 512 * LANES
    red = _all_reduce(_pack([small_full[n] for n in small], unit_s, pack_s), pos, tag="small")
    red = dict(zip(small, _unpack(red, [small_full[n].shape for n in small], unit_s)))
    cs_lru = W_lru // N_CHIPS
    cs_ffn = F2 // N_CHIPS
    rs_wa = Dh // N_CHIPS
    red["lru_conv_w"] = lax.dynamic_slice_in_dim(red["lru_conv_w"], chip * cs_lru, cs_lru, axis=1)
    red["ffn_conv_w"] = lax.dynamic_slice_in_dim(red["ffn_conv_w"], chip * cs_ffn, cs_ffn, axis=1)
    red["lru_wa"] = lax.dynamic_slice_in_dim(red["lru_wa"], chip * rs_wa, rs_wa, axis=1)
    red["lru_wx"] = lax.dynamic_slice_in_dim(red["lru_wx"], chip * rs_wa, rs_wa, axis=1)
    for n in small:
        grads[n] = red[n].reshape(params[n].shape)
    for n in big:
        grads[n] = grads[n].reshape(params[n].shape)

    delta, new_m, new_v = {}, {}, {}
    for n in big:
        shp = params[n].shape
        two_d = (-1, shp[-1])
        d, nm, nv = _adamw(params[n].reshape(two_d), grads[n].reshape(two_d), mom1[n].reshape(two_d),
                           mom2[n].reshape(two_d), name=f"adamw_{n}")
        delta[n], new_m[n], new_v[n] = d.reshape(shp), nm.reshape(shp), nv.reshape(shp)
    packs = [_pack([src[n] for n in small], unit_s, pack_s) for src in (params, grads, mom1, mom2)]
    outs = _adamw(*packs, name="adamw_small")
    shapes = [params[n].shape for n in small]
    for dst, packed in zip((delta, new_m, new_v), outs):
        dst.update(dict(zip(small, _unpack(packed, shapes, unit_s))))

    return (loss, dx.reshape(x.shape), *[grads[n] for n in names], *[delta[n] for n in names],
            *[new_m[n] for n in names], *[new_v[n] for n in names])
```

```python
import math

import jax
import jax.numpy as jnp
from jax import lax
from jax.experimental import pallas as pl
from jax.experimental.pallas import tpu as pltpu

F32 = jnp.float32
BF16 = jnp.bfloat16
MESH = pl.DeviceIdType.MESH
ANY = pl.BlockSpec(memory_space=pl.ANY)

EPS = 1e-6
LRU_C = 8.0
ADAM_LR = 0.001
ADAM_B1 = 0.9
ADAM_B2 = 0.999
ADAM_EPS = 1e-08
ADAM_WD = 0.01
ADAM_STEP = 10

N_CHIPS = 4
SUBLANES = 8
BF16_ROWS = 16
LANES = 128
VMEM_LIMIT = 56 * 1024 * 1024
GELU_C = math.sqrt(2.0 / math.pi)
GELU_K = 0.044715


def _cp(*sem):
    return pltpu.CompilerParams(dimension_semantics=sem, vmem_limit_bytes=VMEM_LIMIT)


def _blk(dim, pref):
    if dim <= pref:
        return dim
    b = pref
    while dim % b:
        b //= 2
    return b


def _gelu(x):
    t = jnp.tanh(GELU_C * (x + GELU_K * x * x * x))
    return 0.5 * x * (1.0 + t)


def _gelu_and_grad(x):
    x2 = x * x
    t = jnp.tanh(GELU_C * (x + GELU_K * x * x2))
    g = 0.5 * x * (1.0 + t)
    dg = 0.5 * (1.0 + t) + 0.5 * x * (1.0 - t * t) * (GELU_C * (1.0 + 3.0 * GELU_K * x2))
    return g, dg


def _sigmoid(x):
    return 1.0 / (1.0 + jnp.exp(-x))


def _softplus(x):
    e = jnp.exp(-jnp.abs(x))
    series = e * (1.0 - e * (0.5 - e * (1.0 / 3.0 - e * (0.25 - e * 0.2))))
    return jnp.where(e < 0.01, series, jnp.log(1.0 + e)) + jnp.maximum(x, 0.0)


def _neg_expm1(x):
    series = -(x * (1.0 + x * (0.5 + x * (1.0 / 6.0 + x * (1.0 / 24.0)))))
    return jnp.where(x > -0.01, series, 1.0 - jnp.exp(x))


def _mesh_pos():
    return lax.axis_index("x"), lax.axis_index("y"), lax.axis_index("c")


def _other_chips(x, y):
    return [(1 - x, y), (x, 1 - y), (1 - x, 1 - y)]


def _remote(k, src, dst, to, send_sems, recv_sems):
    return pltpu.make_async_remote_copy(src_ref=src, dst_ref=dst, send_sem=send_sems.at[k],
                                        recv_sem=recv_sems.at[k], device_id=to, device_id_type=MESH)


class _GatherRows:
    n_sems = 6

    def __init__(self, buf, r0, r1):
        self.args = [buf]
        self.out_shape = [jax.ShapeDtypeStruct(buf.shape, buf.dtype)]
        self.aliases = {0: 0}
        self.r0, self.h = r0, (r1 - r0) // 2

    def _rows(self, half):
        return pl.ds(self.r0 + half * self.h, self.h)

    def start(self, ins, outs, ss, rs, base):
        x, y, c = _mesh_pos()
        mine = ins[0].at[2 * x + y, self._rows(c), :]
        for j, (px, py) in enumerate(_other_chips(x, y)):
            _remote(base + j, mine, outs[0].at[2 * x + y, self._rows(c), :], (px, py, c), ss, rs).start()

    def finish(self, ins, outs, ss, rs, base):
        x, y, c = _mesh_pos()
        sibling = (x, y, 1 - c)
        chips = _other_chips(x, y)
        mine = ins[0].at[2 * x + y, self._rows(c), :]
        for j, (px, py) in enumerate(chips):
            got = outs[0].at[2 * px + py, self._rows(c), :]
            _remote(base + j, got, got, (px, py, c), ss, rs).wait_recv()
            _remote(base + 3 + j, got, got, sibling, ss, rs).start()
        for j, (px, py) in enumerate(chips):
            fwd = outs[0].at[2 * px + py, self._rows(1 - c), :]
            _remote(base + 3 + j, fwd, fwd, sibling, ss, rs).wait_recv()
        for j, (px, py) in enumerate(chips):
            got = outs[0].at[2 * px + py, self._rows(c), :]
            _remote(base + j, mine, mine, (px, py, c), ss, rs).wait_send()
            _remote(base + 3 + j, got, got, sibling, ss, rs).wait_send()


class _GatherIci(_GatherRows):
    n_sems = 3

    def finish(self, ins, outs, ss, rs, base):
        x, y, c = _mesh_pos()
        mine = ins[0].at[2 * x + y, self._rows(c), :]
        for j, (px, py) in enumerate(_other_chips(x, y)):
            got = outs[0].at[2 * px + py, self._rows(c), :]
            _remote(base + j, got, got, (px, py, c), ss, rs).wait_recv()
            _remote(base + j, mine, mine, (px, py, c), ss, rs).wait_send()


class _GatherFwd(_GatherRows):
    n_sems = 3

    def start(self, ins, outs, ss, rs, base):
        x, y, c = _mesh_pos()
        for j, (px, py) in enumerate(_other_chips(x, y)):
            _remote(base + j, ins[0].at[2 * px + py, self._rows(c), :], outs[0].at[2 * px + py, self._rows(c), :],
                    (x, y, 1 - c), ss, rs).start()

    def finish(self, ins, outs, ss, rs, base):
        x, y, c = _mesh_pos()
        for j, (px, py) in enumerate(_other_chips(x, y)):
            got = ins[0].at[2 * px + py, self._rows(c), :]
            fwd = outs[0].at[2 * px + py, self._rows(1 - c), :]
            _remote(base + j, got, got, (x, y, 1 - c), ss, rs).wait_send()
            _remote(base + j, fwd, fwd, (x, y, 1 - c), ss, rs).wait_recv()


class _ChipExchange:
    n_sems = 3

    def __init__(self, p):
        self.args = [p]
        self.out_shape = [jax.ShapeDtypeStruct((N_CHIPS - 1,) + p.shape[1:], p.dtype)]
        self.aliases = {}

    def _copies(self, ins, outs, ss, rs, base):
        x, y, c = _mesh_pos()
        return [_remote(base + j, ins[0].at[2 * px + py], outs[0].at[j], (px, py, c), ss, rs)
                for j, (px, py) in enumerate(_other_chips(x, y))]

    def start(self, ins, outs, ss, rs, base):
        for cp in self._copies(ins, outs, ss, rs, base):
            cp.start()

    def finish(self, ins, outs, ss, rs, base):
        for cp in self._copies(ins, outs, ss, rs, base):
            cp.wait()


class _ShareHalves:
    n_sems = 1

    def __init__(self, buf):
        self.args = [buf]
        self.out_shape = [jax.ShapeDtypeStruct(buf.shape, buf.dtype)]
        self.aliases = {0: 0}
        self.hr = buf.shape[0] // 2

    def start(self, ins, outs, ss, rs, base):
        x, y, c = _mesh_pos()
        rows = pl.ds(c * self.hr, self.hr)
        _remote(base, ins[0].at[rows, :], outs[0].at[rows, :], (x, y, 1 - c), ss, rs).start()

    def finish(self, ins, outs, ss, rs, base):
        x, y, c = _mesh_pos()
        mine = ins[0].at[pl.ds(c * self.hr, self.hr), :]
        theirs = outs[0].at[pl.ds((1 - c) * self.hr, self.hr), :]
        _remote(base, mine, mine, (x, y, 1 - c), ss, rs).wait_send()
        _remote(base, theirs, theirs, (x, y, 1 - c), ss, rs).wait_recv()


class _GatherSlabs:
    n_sems = 3

    def __init__(self, buf):
        self.args = [buf]
        self.out_shape = [jax.ShapeDtypeStruct(buf.shape, buf.dtype)]
        self.aliases = {0: 0}

    def start(self, ins, outs, ss, rs, base):
        x, y, c = _mesh_pos()
        for j, (px, py) in enumerate(_other_chips(x, y)):
            _remote(base + j, ins[0].at[2 * x + y], outs[0].at[2 * x + y], (px, py, c), ss, rs).start()

    def finish(self, ins, outs, ss, rs, base):
        x, y, c = _mesh_pos()
        mine = ins[0].at[2 * x + y]
        for j, (px, py) in enumerate(_other_chips(x, y)):
            got = outs[0].at[2 * px + py]
            _remote(base + j, mine, mine, (px, py, c), ss, rs).wait_send()
            _remote(base + j, got, got, (px, py, c), ss, rs).wait_recv()


class _Carry:
    def __init__(self, *items):
        self.items = items
        self.args, self.out_shape, self._alias, self._slots = [], [], {}, []
        seen = {}
        for it in items:
            key = id(it.args[0]) if it.aliases else None
            if key is None or key not in seen:
                slot = (len(self.args), len(self.out_shape))
                if it.aliases:
                    seen[key] = slot
                    self._alias[slot[0]] = slot[1]
                self.args.append(it.args[0])
                self.out_shape.append(it.out_shape[0])
            else:
                slot = seen[key]
            self._slots.append(slot)
        self.n_sems = sum(it.n_sems for it in items)

    def aliases(self, in_base, out_base):
        return {in_base + i: out_base + o for i, o in self._alias.items()}

    def _each(self, method, ins, outs, ss, rs):
        base = 0
        for it, (i, o) in zip(self.items, self._slots):
            getattr(it, method)([ins[i]], [outs[o]], ss, rs, base)
            base += it.n_sems

    def start(self, ins, outs, ss, rs):
        self._each("start", ins, outs, ss, rs)

    def finish(self, ins, outs, ss, rs):
        self._each("finish", ins, outs, ss, rs)


def _carried_call(body, *, name, grid, in_specs, out_specs, out_shape, scratch_shapes, args, semantics, carry=None):
    if carry is None:
        outs = pl.pallas_call(body, name=name, grid=grid, in_specs=in_specs, out_specs=out_specs,
                              out_shape=out_shape, scratch_shapes=scratch_shapes,
                              compiler_params=_cp(*semantics))(*args)
        return list(outs), []
    n_in, n_out, n_scr = len(in_specs), len(out_specs), len(scratch_shapes)
    n_cin, n_cout = len(carry.args), len(carry.out_shape)

    def full(*refs):
        ins = refs[:n_in]
        cins = refs[n_in:n_in + n_cin]
        outs = refs[n_in + n_cin:n_in + n_cin + n_out]
        couts = refs[n_in + n_cin + n_out:n_in + n_cin + n_out + n_cout]
        scr = refs[n_in + n_cin + n_out + n_cout:n_in + n_cin + n_out + n_cout + n_scr]
        ss, rs = refs[-2], refs[-1]
        first = pl.program_id(0) == 0
        last = pl.program_id(0) == grid[0] - 1
        for d in range(1, len(grid)):
            first = jnp.logical_and(first, pl.program_id(d) == 0)
            last = jnp.logical_and(last, pl.program_id(d) == grid[d] - 1)

        @pl.when(first)
        def _():
            carry.start(cins, couts, ss, rs)

        body(*ins, *outs, *scr)

        @pl.when(last)
        def _():
            carry.finish(cins, couts, ss, rs)

    outs = pl.pallas_call(
        full, name=name, grid=grid, in_specs=list(in_specs) + [ANY] * n_cin,
        out_specs=list(out_specs) + [ANY] * n_cout, out_shape=list(out_shape) + carry.out_shape,
        scratch_shapes=list(scratch_shapes) + [pltpu.SemaphoreType.DMA((carry.n_sems,))] * 2,
        input_output_aliases=carry.aliases(n_in, n_out),
        compiler_params=_cp(*(("arbitrary",) * len(grid))),
    )(*args, *carry.args)
    return list(outs[:n_out]), list(outs[n_out:])


def _comm_call(carry, *, name):
    n_cin = len(carry.args)

    def body(*refs):
        cins, couts = refs[:n_cin], refs[n_cin:-2]
        carry.start(cins, couts, refs[-2], refs[-1])
        carry.finish(cins, couts, refs[-2], refs[-1])

    outs = pl.pallas_call(
        body, name=name, in_specs=[ANY] * n_cin, out_specs=[ANY] * len(carry.out_shape), out_shape=carry.out_shape,
        scratch_shapes=[pltpu.SemaphoreType.DMA((carry.n_sems,))] * 2,
        input_output_aliases=carry.aliases(0, 0),
    )(*carry.args)
    return list(outs)


def _mm_nn(a, b, *, out_dtype, name, res=None, carry=None):
    M, K = a.shape
    cs = b.shape[-1]
    N = cs * (b.shape[0] if b.ndim == 3 else 1)
    tm, tn, tk = _blk(M, 1024), _blk(cs, 1024), _blk(K, 4096 if res is None else 2048)
    nbs, nk = cs // tn, K // tk
    if b.ndim == 3:
        b_spec = pl.BlockSpec((None, tk, tn), lambda i, j, k: (j // nbs, k, j % nbs))
    else:
        b_spec = pl.BlockSpec((tk, tn), lambda i, j, k: (k, j))
    in_specs = [pl.BlockSpec((tm, tk), lambda i, j, k: (i, k)), b_spec]
    args = [a, b]
    if res is not None:
        in_specs.append(pl.BlockSpec((tm, tn), lambda i, j, k: (i, j)))
        args.append(res)

    def body(*refs):
        a_ref, b_ref = refs[0], refs[1]
        r_ref = refs[2] if res is not None else None
        p = jnp.dot(a_ref[...], b_ref[...], preferred_element_type=F32)
        if nk == 1:
            o_ref = refs[-1]
            o_ref[...] = (p if r_ref is None else p + r_ref[...]).astype(out_dtype)
            return
        o_ref, acc = refs[-2], refs[-1]
        k = pl.program_id(2)

        @pl.when(k == 0)
        def _():
            acc[...] = p

        @pl.when(k > 0)
        def _():
            acc[...] += p

        @pl.when(k == nk - 1)
        def _():
            r = acc[...]
            if r_ref is not None:
                r = r + r_ref[...]
            o_ref[...] = r.astype(out_dtype)

    outs, carried = _carried_call(
        body, name=name, grid=(M // tm, N // tn, nk), in_specs=in_specs,
        out_specs=[pl.BlockSpec((tm, tn), lambda i, j, k: (i, j))],
        out_shape=[jax.ShapeDtypeStruct((M, N), out_dtype)],
        scratch_shapes=[pltpu.VMEM((tm, tn), F32)] if nk > 1 else [], args=args,
        semantics=("parallel", "parallel", "arbitrary"), carry=carry)
    return outs[0] if carry is None else (outs[0], carried)


def _mm_nt(a, b, *, out_dtype, name, carry=None):
    M, Kc = a.shape
    cs = b.shape[-1]
    N = b.shape[-2]
    tm, tn, tk = _blk(M, 1024), _blk(N, 1024), _blk(cs, 2048)
    nks, nk = cs // tk, Kc // tk
    if b.ndim == 3:
        b_spec = pl.BlockSpec((None, tn, tk), lambda i, j, k: (k // nks, j, k % nks))
    else:
        b_spec = pl.BlockSpec((tn, tk), lambda i, j, k: (j, k))

    def body(a_ref, b_ref, o_ref, acc):
        k = pl.program_id(2)
        p = lax.dot_general(a_ref[...], b_ref[...], (((1,), (1,)), ((), ())), preferred_element_type=F32)

        @pl.when(k == 0)
        def _():
            acc[...] = p

        @pl.when(k > 0)
        def _():
            acc[...] += p

        @pl.when(k == nk - 1)
        def _():
            o_ref[...] = acc[...].astype(out_dtype)

    outs, carried = _carried_call(
        body, name=name, grid=(M // tm, N // tn, nk),
        in_specs=[pl.BlockSpec((tm, tk), lambda i, j, k: (i, k)), b_spec],
        out_specs=[pl.BlockSpec((tm, tn), lambda i, j, k: (i, j))],
        out_shape=[jax.ShapeDtypeStruct((M, N), out_dtype)],
        scratch_shapes=[pltpu.VMEM((tm, tn), F32)], args=[a, b],
        semantics=("parallel", "parallel", "arbitrary"), carry=carry)
    return outs[0] if carry is None else (outs[0], carried)


def _mm_tn(a, b, *, out_dtype, name, col_shards=None, carry=None):
    T, K1 = a.shape
    N = b.shape[1]
    cs = N // col_shards if col_shards else N
    tm, tn, tk = _blk(K1, 1024), _blk(cs, 1024), _blk(T, 4096)
    nbs, nk = cs // tn, T // tk
    if col_shards:
        o_spec = pl.BlockSpec((None, tm, tn), lambda i, j, k: (j // nbs, i, j % nbs))
        o_shape = jax.ShapeDtypeStruct((col_shards, K1, cs), out_dtype)
    else:
        o_spec = pl.BlockSpec((tm, tn), lambda i, j, k: (i, j))
        o_shape = jax.ShapeDtypeStruct((K1, N), out_dtype)

    def body(a_ref, b_ref, o_ref, *scr):
        p = lax.dot_general(a_ref[...], b_ref[...], (((0,), (0,)), ((), ())), preferred_element_type=F32)
        if nk == 1:
            o_ref[...] = p.astype(out_dtype)
            return
        acc = scr[0]
        k = pl.program_id(2)

        @pl.when(k == 0)
        def _():
            acc[...] = p

        @pl.when(k > 0)
        def _():
            acc[...] += p

        @pl.when(k == nk - 1)
        def _():
            o_ref[...] = acc[...].astype(out_dtype)

    outs, carried = _carried_call(
        body, name=name, grid=(K1 // tm, N // tn, nk),
        in_specs=[pl.BlockSpec((tk, tm), lambda i, j, k: (k, i)), pl.BlockSpec((tk, tn), lambda i, j, k: (k, j))],
        out_specs=[o_spec], out_shape=[o_shape],
        scratch_shapes=[pltpu.VMEM((tm, tn), F32)] if nk > 1 else [], args=[a, b],
        semantics=("parallel", "parallel", "arbitrary"), carry=carry)
    return outs[0] if carry is None else (outs[0], carried)


def _rms_fwd(x, g, *, name, carry=None):
    T, D = x.shape
    tm = _blk(T, 256)

    def body(x_ref, g_ref, o_ref):
        xv = x_ref[...]
        r = lax.rsqrt(jnp.mean(xv * xv, axis=-1, keepdims=True) + EPS)
        o_ref[...] = (xv * r * g_ref[...]).astype(BF16)

    outs, carried = _carried_call(
        body, name=name, grid=(T // tm,),
        in_specs=[pl.BlockSpec((tm, D), lambda i: (i, 0)), pl.BlockSpec((1, D), lambda i: (0, 0))],
        out_specs=[pl.BlockSpec((tm, D), lambda i: (i, 0))],
        out_shape=[jax.ShapeDtypeStruct((T, D), BF16)], scratch_shapes=[], args=[x, g],
        semantics=("parallel",), carry=carry)
    return outs[0] if carry is None else (outs[0], carried)


def _rms_bwd(x, g, dh, dres, *, name):
    T, D = x.shape
    tm = _blk(T, 256)

    def body(x_ref, g_ref, dh_ref, dres_ref, dx_ref, dxb_ref, dg_ref):
        i = pl.program_id(0)
        xv = x_ref[...]
        r = lax.rsqrt(jnp.mean(xv * xv, axis=-1, keepdims=True) + EPS)
        n = xv * r
        dh_v = dh_ref[...]
        dn = dh_v * g_ref[...]
        dx = dres_ref[...] + r * (dn - n * jnp.mean(dn * n, axis=-1, keepdims=True))
        dx_ref[...] = dx
        dxb_ref[...] = dx.astype(BF16)
        part = jnp.sum(dh_v * n, axis=0, keepdims=True)

        @pl.when(i == 0)
        def _():
            dg_ref[...] = part

        @pl.when(i > 0)
        def _():
            dg_ref[...] += part

    row = pl.BlockSpec((tm, D), lambda i: (i, 0))
    vec = pl.BlockSpec((1, D), lambda i: (0, 0))
    return pl.pallas_call(
        body, name=name, grid=(T // tm,),
        in_specs=[row, vec, row, row], out_specs=[row, row, vec],
        out_shape=[jax.ShapeDtypeStruct((T, D), F32), jax.ShapeDtypeStruct((T, D), BF16),
                   jax.ShapeDtypeStruct((1, D), F32)],
        compiler_params=_cp("arbitrary"),
    )(x, g, dh, dres)


def _loss_head(x2, tgt, g, *, name):
    T, D = x2.shape
    tm = _blk(T, 256)

    def body(x_ref, t_ref, g_ref, l_ref, dx_ref, dxb_ref, dg_ref):
        i = pl.program_id(0)
        xv = x_ref[...]
        gv = g_ref[...]
        r = lax.rsqrt(jnp.mean(xv * xv, axis=-1, keepdims=True) + EPS)
        n = xv * r
        err = n * gv - t_ref[...]
        dy = err * (1.0 / D)
        dn = dy * gv
        dx = r * (dn - n * jnp.mean(dn * n, axis=-1, keepdims=True))
        dx_ref[...] = dx
        dxb_ref[...] = dx.astype(BF16)
        lpart = jnp.sum(err * err, axis=0, keepdims=True)
        gpart = jnp.sum(dy * n, axis=0, keepdims=True)

        @pl.when(i == 0)
        def _():
            l_ref[...] = lpart
            dg_ref[...] = gpart

        @pl.when(i > 0)
        def _():
            l_ref[...] += lpart
            dg_ref[...] += gpart

    row = pl.BlockSpec((tm, D), lambda i: (i, 0))
    vec = pl.BlockSpec((1, D), lambda i: (0, 0))
    return pl.pallas_call(
        body, name=name, grid=(T // tm,),
        in_specs=[row, row, vec], out_specs=[vec, row, row, vec],
        out_shape=[jax.ShapeDtypeStruct((1, D), F32), jax.ShapeDtypeStruct((T, D), F32),
                   jax.ShapeDtypeStruct((T, D), BF16), jax.ShapeDtypeStruct((1, D), F32)],
        compiler_params=_cp("arbitrary"),
    )(x2, tgt, g)


def _merge_fwd(proj, pa, pb, *, off_a, name, carry=None):
    T, D = pa.shape
    tm, tn = _blk(T, 256), _blk(D, 1024)
    oa, ob = off_a // tn, (off_a + D) // tn

    def body(ga_ref, gb_ref, pa_ref, pb_ref, o_ref):
        o_ref[...] = (_sigmoid(ga_ref[...]) * pa_ref[...] + _sigmoid(gb_ref[...]) * pb_ref[...]).astype(BF16)

    blk = pl.BlockSpec((tm, tn), lambda i, j: (i, j))
    outs, carried = _carried_call(
        body, name=name, grid=(T // tm, D // tn),
        in_specs=[pl.BlockSpec((tm, tn), lambda i, j: (i, oa + j)),
                  pl.BlockSpec((tm, tn), lambda i, j: (i, ob + j)), blk, blk],
        out_specs=[blk], out_shape=[jax.ShapeDtypeStruct((T, D), BF16)], scratch_shapes=[],
        args=[proj, proj, pa, pb], semantics=("parallel", "parallel"), carry=carry)
    return outs[0] if carry is None else (outs[0], carried)


def _merge_bwd(proj, pa, pb, dm, *, off_a, name):
    T, D = pa.shape
    tm, tn = _blk(T, 256), _blk(D, 1024)
    oa, ob = off_a // tn, (off_a + D) // tn

    def body(ga_ref, gb_ref, pa_ref, pb_ref, dm_ref, dga_ref, dgb_ref, dpa_ref, dpb_ref):
        dmv = dm_ref[...]
        sa = _sigmoid(ga_ref[...])
        sb = _sigmoid(gb_ref[...])
        dga_ref[...] = (dmv * pa_ref[...] * sa * (1.0 - sa)).astype(BF16)
        dgb_ref[...] = (dmv * pb_ref[...] * sb * (1.0 - sb)).astype(BF16)
        dpa_ref[...] = (dmv * sa).astype(BF16)
        dpb_ref[...] = (dmv * sb).astype(BF16)

    blk = pl.BlockSpec((tm, tn), lambda i, j: (i, j))
    out = jax.ShapeDtypeStruct((T, D), BF16)
    return pl.pallas_call(
        body, name=name, grid=(T // tm, D // tn),
        in_specs=[pl.BlockSpec((tm, tn), lambda i, j: (i, oa + j)),
                  pl.BlockSpec((tm, tn), lambda i, j: (i, ob + j)), blk, blk, blk],
        out_specs=[blk] * 4, out_shape=[out] * 4,
        compiler_params=_cp("parallel", "parallel"),
    )(proj, proj, pa, pb, dm)


def _sg_mask(sg_w, *, name):
    G, C, _ = sg_w.shape

    def body(w_ref, m_ref, mt_ref):
        row = lax.broadcasted_iota(jnp.int32, (C, C), 0)
        col = lax.broadcasted_iota(jnp.int32, (C, C), 1)
        for g in range(G):
            w = jnp.where(row >= col, w_ref[g], 0.0)
            m_ref[g] = w.astype(BF16)
            mt_ref[g] = w.T.astype(BF16)

    out = jax.ShapeDtypeStruct((G, C, C), BF16)
    return pl.pallas_call(body, name=name, out_shape=[out, out])(sg_w)


def _sg_layernorm(zv, lg, lb):
    v = _gelu(zv)
    mu = jnp.mean(v, axis=-1, keepdims=True)
    xc = v - mu
    rstd = lax.rsqrt(jnp.mean(xc * xc, axis=-1, keepdims=True) + EPS)
    vhat = xc * rstd
    return vhat, rstd, vhat * lg + lb


def _sg_fwd(proj, lg, lb, wm, bt, *, name, carry=None):
    T = proj.shape[0]
    G, C, _ = wm.shape
    W = lg.shape[-1]
    gd = W // G

    def body(zu_ref, zv_ref, lg_ref, lb_ref, wm_ref, bt_ref, ya_ref, vn_scr):
        _, _, vn = _sg_layernorm(zv_ref[...], lg_ref[...], lb_ref[...])
        vn_scr[...] = vn.astype(BF16)
        for g in range(G):
            cols = slice(g * gd, (g + 1) * gd)
            mixed = jnp.dot(wm_ref[g], vn_scr[:, cols], preferred_element_type=F32) + bt_ref[:, g:g + 1]
            ya_ref[:, cols] = (_gelu(zu_ref[:, cols]) * mixed).astype(BF16)

    vec = pl.BlockSpec((1, W), lambda i: (0, 0))
    outs, carried = _carried_call(
        body, name=name, grid=(T // C,),
        in_specs=[pl.BlockSpec((C, W), lambda i: (i, 0)), pl.BlockSpec((C, W), lambda i: (i, 1)), vec, vec,
                  pl.BlockSpec((G, C, C), lambda i: (0, 0, 0)), pl.BlockSpec((C, G), lambda i: (0, 0))],
        out_specs=[pl.BlockSpec((C, W), lambda i: (i, 0))],
        out_shape=[jax.ShapeDtypeStruct((T, W), BF16)],
        scratch_shapes=[pltpu.VMEM((C, W), BF16)], args=[proj, proj, lg, lb, wm, bt],
        semantics=("parallel",), carry=carry)
    return outs[0] if carry is None else (outs[0], carried)


def _sg_bwd(proj, dya, lg, lb, wm, wmt, bt, *, name):
    T = proj.shape[0]
    G, C, _ = wm.shape
    W = lg.shape[-1]
    gd = W // G
    n_steps = T // C

    def body(zu_ref, zv_ref, dya_ref, lg_ref, lb_ref, wm_ref, wmt_ref, bt_ref,
             dz_ref, dwm_ref, dbt_ref, dlg_ref, dlb_ref, vn_scr, dvn_scr):
        i = pl.program_id(0)

        @pl.when(i == 0)
        def _():
            dwm_ref[...] = jnp.zeros_like(dwm_ref)
            dbt_ref[...] = jnp.zeros_like(dbt_ref)
            dlg_ref[...] = jnp.zeros_like(dlg_ref)
            dlb_ref[...] = jnp.zeros_like(dlb_ref)

        lgv = lg_ref[...]
        vhat, rstd, vn = _sg_layernorm(zv_ref[...], lgv, lb_ref[...])
        vn_scr[...] = vn.astype(BF16)
        for g in range(G):
            cols = slice(g * gd, (g + 1) * gd)
            vnb = vn_scr[:, cols]
            mixed = jnp.dot(wm_ref[g], vnb, preferred_element_type=F32) + bt_ref[:, g:g + 1]
            u, du = _gelu_and_grad(zu_ref[:, cols])
            dy = dya_ref[:, cols]
            dz_ref[:, cols] = (dy * mixed * du).astype(BF16)
            dmix = dy * u
            dmb = dmix.astype(BF16)
            dbt_ref[:, g:g + 1] += jnp.sum(dmix, axis=1, keepdims=True)
            dwm_ref[g] += lax.dot_general(dmb, vnb, (((1,), (1,)), ((), ())), preferred_element_type=F32)
            dvn_scr[:, cols] = jnp.dot(wmt_ref[g], dmb, preferred_element_type=F32)
        dvn = dvn_scr[...]
        dlg_ref[...] += jnp.sum(dvn * vhat, axis=0, keepdims=True)
        dlb_ref[...] += jnp.sum(dvn, axis=0, keepdims=True)
        dvh = dvn * lgv
        dv = rstd * (dvh - jnp.mean(dvh, axis=-1, keepdims=True)
                     - vhat * jnp.mean(dvh * vhat, axis=-1, keepdims=True))
        _, dgv = _gelu_and_grad(zv_ref[...])
        dz_ref[:, W:] = (dv * dgv).astype(BF16)

        @pl.when(i == n_steps - 1)
        def _():
            row = lax.broadcasted_iota(jnp.int32, (C, C), 0)
            col = lax.broadcasted_iota(jnp.int32, (C, C), 1)
            for g in range(G):
                dwm_ref[g] = jnp.where(row >= col, dwm_ref[g], 0.0)

    vec = pl.BlockSpec((1, W), lambda i: (0, 0))
    mat = pl.BlockSpec((G, C, C), lambda i: (0, 0, 0))
    bts = pl.BlockSpec((C, G), lambda i: (0, 0))
    return pl.pallas_call(
        body, name=name, grid=(n_steps,),
        in_specs=[pl.BlockSpec((C, W), lambda i: (i, 0)), pl.BlockSpec((C, W), lambda i: (i, 1)),
                  pl.BlockSpec((C, W), lambda i: (i, 0)), vec, vec, mat, mat, bts],
        out_specs=[pl.BlockSpec((C, 2 * W), lambda i: (i, 0)), mat, bts, vec, vec],
        out_shape=[jax.ShapeDtypeStruct((T, 2 * W), BF16), jax.ShapeDtypeStruct((G, C, C), F32),
                   jax.ShapeDtypeStruct((C, G), F32), jax.ShapeDtypeStruct((1, W), F32),
                   jax.ShapeDtypeStruct((1, W), F32)],
        scratch_shapes=[pltpu.VMEM((C, W), BF16), pltpu.VMEM((C, W), F32)],
        compiler_params=_cp("arbitrary"),
    )(proj, proj, dya, lg, lb, wm, wmt, bt)


def _rows_with_prev(ref, r0, rows, ci):
    p0 = pl.multiple_of(jnp.maximum(r0 - SUBLANES, 0), SUBLANES)
    prev = jnp.where(ci > 0, ref[pl.ds(p0, SUBLANES), :], 0.0)
    return jnp.concatenate([prev, ref[pl.ds(r0, rows), :]], axis=0)


def _rows_with_next(ref, r0, rows, ci, n_chunks, total):
    n0 = pl.multiple_of(jnp.minimum(r0 + rows, total - SUBLANES), SUBLANES)
    nxt = jnp.where(ci < n_chunks - 1, ref[pl.ds(n0, SUBLANES), :], 0.0)
    return jnp.concatenate([ref[pl.ds(r0, rows), :], nxt], axis=0)


def _delayed(xx, k, rows):
    if k == 0:
        return xx[SUBLANES:, :]
    return pltpu.roll(xx, k, 0)[SUBLANES:, :]


def _advanced(xx, k, rows):
    if k == 0:
        return xx[:rows, :]
    return pltpu.roll(xx, rows + SUBLANES - k, 0)[:rows, :]


def _conv_chunk(x_ref, w_ref, b_ref, r0, rows, ci):
    K = w_ref.shape[0]
    xx = _rows_with_prev(x_ref, r0, rows, ci)
    out = _delayed(xx, K - 1, rows) * w_ref[0:1, :]
    for k in range(1, K):
        out = out + _delayed(xx, K - 1 - k, rows) * w_ref[k:k + 1, :]
    return out + b_ref[...]


def _ffn_act_fwd(up0, cw, cb, *, batch, name, carry=None):
    T, F2 = up0.shape
    F = F2 // 2
    S = T // batch
    K = cw.shape[0]
    cbk = _blk(F, 512)
    nj = F // cbk
    R = min(64, S // 2)
    n_chunks = S // R

    def body(ug_ref, uv_ref, wg_ref, wv_ref, bg_ref, bv_ref, act_ref):
        def chunk(ci, carry):
            r0 = pl.multiple_of(ci * R, R)
            cg = _conv_chunk(ug_ref, wg_ref, bg_ref, r0, R, ci)
            cv = _conv_chunk(uv_ref, wv_ref, bv_ref, r0, R, ci)
            act_ref[pl.ds(r0, R), :] = (_gelu(cg) * cv).astype(BF16)
            return carry

        lax.fori_loop(0, n_chunks, chunk, 0)

    outs, carried = _carried_call(
        body, name=name, grid=(nj, batch),
        in_specs=[pl.BlockSpec((S, cbk), lambda j, b: (b, j)), pl.BlockSpec((S, cbk), lambda j, b: (b, nj + j)),
                  pl.BlockSpec((K, cbk), lambda j, b: (0, j)), pl.BlockSpec((K, cbk), lambda j, b: (0, nj + j)),
                  pl.BlockSpec((1, cbk), lambda j, b: (0, j)), pl.BlockSpec((1, cbk), lambda j, b: (0, nj + j))],
        out_specs=[pl.BlockSpec((S, cbk), lambda j, b: (b, j))],
        out_shape=[jax.ShapeDtypeStruct((T, F), BF16)], scratch_shapes=[],
        args=[up0, up0, cw, cw, cb, cb], semantics=("parallel", "parallel"), carry=carry)
    return outs[0] if carry is None else (outs[0], carried)


def _ffn_act_bwd(up0, cw, cb, dact, *, batch, name):
    T, F2 = up0.shape
    F = F2 // 2
    S = T // batch
    K = cw.shape[0]
    cbk = _blk(F, 512)
    nj = F // cbk
    R = min(64, S // 2)
    n_chunks = S // R

    def body(ug_ref, uv_ref, wg_ref, wv_ref, bg_ref, bv_ref, da_ref,
             dug_ref, duv_ref, dw_g_ref, dw_v_ref, db_g_ref, db_v_ref, dcg_scr, dcv_scr):
        b = pl.program_id(1)

        def chunk_a(ci, acc):
            r0 = pl.multiple_of(ci * R, R)
            xg = _rows_with_prev(ug_ref, r0, R, ci)
            xv = _rows_with_prev(uv_ref, r0, R, ci)
            dg_taps = [_delayed(xg, K - 1 - k, R) for k in range(K)]
            dv_taps = [_delayed(xv, K - 1 - k, R) for k in range(K)]
            cg = dg_taps[0] * wg_ref[0:1, :]
            cv = dv_taps[0] * wv_ref[0:1, :]
            for k in range(1, K):
                cg = cg + dg_taps[k] * wg_ref[k:k + 1, :]
                cv = cv + dv_taps[k] * wv_ref[k:k + 1, :]
            cg = cg + bg_ref[...]
            cv = cv + bv_ref[...]
            gl, dgl = _gelu_and_grad(cg)
            da = da_ref[pl.ds(r0, R), :]
            dcg = da * cv * dgl
            dcv = da * gl
            dcg_scr[pl.ds(r0, R), :] = dcg
            dcv_scr[pl.ds(r0, R), :] = dcv
            new = []
            for k in range(K):
                new.append(acc[k] + jnp.sum(dcg * dg_taps[k], axis=0, keepdims=True))
            for k in range(K):
                new.append(acc[K + k] + jnp.sum(dcv * dv_taps[k], axis=0, keepdims=True))
            new.append(acc[2 * K] + jnp.sum(dcg, axis=0, keepdims=True))
            new.append(acc[2 * K + 1] + jnp.sum(dcv, axis=0, keepdims=True))
            return tuple(new)

        zero = jnp.zeros((1, cbk), F32)
        acc = lax.fori_loop(0, n_chunks, chunk_a, (zero,) * (2 * K + 2))

        def chunk_b(ci, carry):
            r0 = pl.multiple_of(ci * R, R)
            dg = _rows_with_next(dcg_scr, r0, R, ci, n_chunks, S)
            dv = _rows_with_next(dcv_scr, r0, R, ci, n_chunks, S)
            og = _advanced(dg, 0, R) * wg_ref[K - 1:K, :]
            ov = _advanced(dv, 0, R) * wv_ref[K - 1:K, :]
            for j in range(1, K):
                og = og + _advanced(dg, j, R) * wg_ref[K - 1 - j:K - j, :]
                ov = ov + _advanced(dv, j, R) * wv_ref[K - 1 - j:K - j, :]
            dug_ref[pl.ds(r0, R), :] = og.astype(BF16)
            duv_ref[pl.ds(r0, R), :] = ov.astype(BF16)
            return carry

        lax.fori_loop(0, n_chunks, chunk_b, 0)

        @pl.when(b == 0)
        def _():
            for k in range(K):
                dw_g_ref[k:k + 1, :] = acc[k]
                dw_v_ref[k:k + 1, :] = acc[K + k]
            db_g_ref[...] = acc[2 * K]
            db_v_ref[...] = acc[2 * K + 1]

        @pl.when(b > 0)
        def _():
            for k in range(K):
                dw_g_ref[k:k + 1, :] += acc[k]
                dw_v_ref[k:k + 1, :] += acc[K + k]
            db_g_ref[...] += acc[2 * K]
            db_v_ref[...] += acc[2 * K + 1]

    seq = pl.BlockSpec((S, cbk), lambda j, b: (b, j))
    wk = pl.BlockSpec((K, cbk), lambda j, b: (0, j))
    w1 = pl.BlockSpec((1, cbk), lambda j, b: (0, j))
    outs = pl.pallas_call(
        body, name=name, grid=(nj, batch),
        in_specs=[seq, pl.BlockSpec((S, cbk), lambda j, b: (b, nj + j)),
                  wk, pl.BlockSpec((K, cbk), lambda j, b: (0, nj + j)),
                  w1, pl.BlockSpec((1, cbk), lambda j, b: (0, nj + j)), seq],
        out_specs=[seq, seq, wk, wk, w1, w1],
        out_shape=[jax.ShapeDtypeStruct((T, F), BF16), jax.ShapeDtypeStruct((T, F), BF16),
                   jax.ShapeDtypeStruct((K, F), F32), jax.ShapeDtypeStruct((K, F), F32),
                   jax.ShapeDtypeStruct((1, F), F32), jax.ShapeDtypeStruct((1, F), F32)],
        scratch_shapes=[pltpu.VMEM((S, cbk), F32), pltpu.VMEM((S, cbk), F32)],
        compiler_params=_cp("parallel", "arbitrary"),
    )(up0, up0, cw, cw, cb, cb, dact)
    dug, duv, dwg, dwv, dbg, dbv = outs
    return dug, duv, jnp.concatenate([dwg, dwv], axis=1), jnp.concatenate([dbg, dbv], axis=1)


def _lru_gate_rows(xr_ref, cw_ref, cb_ref, wa_ref, wx_ref, ba_ref, bx_ref, xc_scr, za_scr, zx_scr, S, R):
    def chunk(ci, carry):
        r0 = pl.multiple_of(ci * R, R)
        xc = _conv_chunk(xr_ref, cw_ref, cb_ref, r0, R, ci)
        xc_scr[pl.ds(r0, R), :] = xc
        xb = xc.astype(BF16)
        za_scr[pl.ds(r0, R), :] = jnp.dot(xb, wa_ref[...], preferred_element_type=F32) + ba_ref[...]
        zx_scr[pl.ds(r0, R), :] = jnp.dot(xb, wx_ref[...], preferred_element_type=F32) + bx_ref[...]
        return carry

    lax.fori_loop(0, S // R, chunk, 0)


def _lru_gates(za, zx, sp):
    ra = _sigmoid(za)
    ig = _sigmoid(zx)
    la = -LRU_C * ra * sp
    a = jnp.exp(la)
    s = jnp.sqrt(_neg_expm1(2.0 * la))
    return ra, ig, a, s


def _lru_fwd(proj, cw, cb, wa, wx, ba, bx, lam, *, batch, off_x, name, carry=None):
    T = proj.shape[0]
    H, Dh, _ = wa.shape
    W = H * Dh
    S = T // batch
    K = cw.shape[0]
    ox, oy = off_x // Dh, (off_x + W) // Dh
    R = min(256, S // 2)
    n16 = S // BF16_ROWS

    def body(xr_ref, yr_ref, cw_ref, cb_ref, wa_ref, wx_ref, ba_ref, bx_ref, lam_ref,
             yb_ref, h_ref, xc_scr, za_scr, zx_scr):
        _lru_gate_rows(xr_ref, cw_ref, cb_ref, wa_ref, wx_ref, ba_ref, bx_ref, xc_scr, za_scr, zx_scr, S, R)
        sp = _softplus(-lam_ref[...])
        row = lax.broadcasted_iota(jnp.int32, (SUBLANES, Dh), 0)

        def tile(r0, carry):
            rows = pl.ds(r0, SUBLANES)
            xc = xc_scr[rows, :]
            _, ig, a, s = _lru_gates(za_scr[rows, :], zx_scr[rows, :], sp)
            A, B = a, s * (ig * xc)
            for d in (1, 2, 4):
                m = row >= d
                Bs = pltpu.roll(B, d, 0)
                As = pltpu.roll(A, d, 0)
                B = jnp.where(m, B + A * Bs, B)
                A = jnp.where(m, A * As, A)
            hh = B + A * carry
            h_ref[rows, :] = hh
            return hh, hh[SUBLANES - 1:SUBLANES, :]

        def step(i, carry):
            r0 = pl.multiple_of(i * BF16_ROWS, BF16_ROWS)
            h0, carry = tile(r0, carry)
            h1, carry = tile(r0 + SUBLANES, carry)
            hh = jnp.concatenate([h0, h1], axis=0)
            yb_ref[pl.ds(r0, BF16_ROWS), :] = (hh * _gelu(yr_ref[pl.ds(r0, BF16_ROWS), :])).astype(BF16)
            return carry

        lax.fori_loop(0, n16, step, jnp.zeros((1, Dh), F32))

    vec = pl.BlockSpec((1, Dh), lambda b, h: (0, h))
    wsp = pl.BlockSpec((None, Dh, Dh), lambda b, h: (h, 0, 0))
    seq = pl.BlockSpec((S, Dh), lambda b, h: (b, h))
    outs, carried = _carried_call(
        body, name=name, grid=(batch, H),
        in_specs=[pl.BlockSpec((S, Dh), lambda b, h: (b, ox + h)), pl.BlockSpec((S, Dh), lambda b, h: (b, oy + h)),
                  pl.BlockSpec((K, Dh), lambda b, h: (0, h)), vec, wsp, wsp, vec, vec, vec],
        out_specs=[seq, seq],
        out_shape=[jax.ShapeDtypeStruct((T, W), BF16), jax.ShapeDtypeStruct((T, W), F32)],
        scratch_shapes=[pltpu.VMEM((S, Dh), F32)] * 3,
        args=[proj, proj, cw, cb, wa, wx, ba, bx, lam], semantics=("parallel", "parallel"), carry=carry)
    return (outs[0], outs[1]) if carry is None else (outs[0], outs[1], carried)


def _lru_bwd(proj, hseq, dyb, cw, cb, wa, wx, wat, wxt, ba, bx, lam, *, batch, off_x, name):
    T = proj.shape[0]
    H, Dh, _ = wa.shape
    W = H * Dh
    S = T // batch
    K = cw.shape[0]
    ox, oy = off_x // Dh, (off_x + W) // Dh
    R = min(256, S // 2)
    n_chunks = S // R
    n16 = S // BF16_ROWS

    def body(xr_ref, yr_ref, h_ref, dyb_ref, cw_ref, cb_ref, wa_ref, wx_ref, wat_ref, wxt_ref,
             ba_ref, bx_ref, lam_ref,
             dxr_ref, dyr_ref, dwa_ref, dwx_ref, dcw_ref, dcb_ref, dba_ref, dbx_ref, dlam_ref,
             xc_scr, za_scr, zx_scr, dza_scr, dzx_scr, dxc_scr):
        b = pl.program_id(1)
        _lru_gate_rows(xr_ref, cw_ref, cb_ref, wa_ref, wx_ref, ba_ref, bx_ref, xc_scr, za_scr, zx_scr, S, R)
        lam_v = lam_ref[...]
        sp = _softplus(-lam_v)
        row = lax.broadcasted_iota(jnp.int32, (SUBLANES, Dh), 0)

        def tile(r0, carry):
            a_next, g_next, s_ba, s_bx, s_lam = carry
            rows = pl.ds(r0, SUBLANES)
            xc = xc_scr[rows, :]
            ra, ig, a, s = _lru_gates(za_scr[rows, :], zx_scr[rows, :], sp)
            hh = h_ref[rows, :]
            gy, dgy = _gelu_and_grad(yr_ref[rows, :])
            dy = dyb_ref[rows, :]
            dyr = dy * hh * dgy
            C = jnp.where(row == SUBLANES - 1, a_next, pltpu.roll(a, SUBLANES - 1, 0))
            B = dy * gy
            for d in (1, 2, 4):
                m = row < SUBLANES - d
                Bs = pltpu.roll(B, SUBLANES - d, 0)
                Cs = pltpu.roll(C, SUBLANES - d, 0)
                B = jnp.where(m, B + C * Bs, B)
                C = jnp.where(m, C * Cs, C)
            G = B + C * g_next
            p0 = pl.multiple_of(jnp.maximum(r0 - SUBLANES, 0), SUBLANES)
            h_before = jnp.where(r0 > 0, h_ref[pl.ds(p0, SUBLANES), :][SUBLANES - 1:SUBLANES, :], 0.0)
            h_prev = jnp.where(row == 0, h_before, pltpu.roll(hh, 1, 0))
            da = G * h_prev
            dig = G * s * xc
            ds = G * ig * xc
            dxc_scr[rows, :] = G * s * ig
            dla = da * a - ds * (a * a) / s
            dza = dla * (-LRU_C * sp) * ra * (1.0 - ra)
            dzx = dig * ig * (1.0 - ig)
            dza_scr[rows, :] = dza
            dzx_scr[rows, :] = dzx
            carry = (a[0:1, :], G[0:1, :], s_ba + dza, s_bx + dzx, s_lam + dla * ra)
            return dyr, carry

        def step(it, carry):
            r0 = pl.multiple_of((n16 - 1 - it) * BF16_ROWS, BF16_ROWS)
            d1, carry = tile(r0 + SUBLANES, carry)
            d0, carry = tile(r0, carry)
            dyr_ref[pl.ds(r0, BF16_ROWS), :] = jnp.concatenate([d0, d1], axis=0).astype(BF16)
            return carry

        z1 = jnp.zeros((1, Dh), F32)
        z8 = jnp.zeros((SUBLANES, Dh), F32)
        _, _, s_ba, s_bx, s_lam = lax.fori_loop(0, n16, step, (z1, z1, z8, z8, z8))
        dba = jnp.sum(s_ba, axis=0, keepdims=True)
        dbx = jnp.sum(s_bx, axis=0, keepdims=True)
        dlam = jnp.sum(s_lam, axis=0, keepdims=True) * (LRU_C * _sigmoid(-lam_v))

        @pl.when(b == 0)
        def _():
            dwa_ref[...] = jnp.zeros_like(dwa_ref)
            dwx_ref[...] = jnp.zeros_like(dwx_ref)

        def chunk_c(ci, carry):
            r0 = pl.multiple_of(ci * R, R)
            rows = pl.ds(r0, R)
            xb = xc_scr[rows, :].astype(BF16)
            dzab = dza_scr[rows, :].astype(BF16)
            dzxb = dzx_scr[rows, :].astype(BF16)
            dwa_ref[...] += lax.dot_general(xb, dzab, (((0,), (0,)), ((), ())), preferred_element_type=F32)
            dwx_ref[...] += lax.dot_general(xb, dzxb, (((0,), (0,)), ((), ())), preferred_element_type=F32)
            dxc_scr[rows, :] += (jnp.dot(dzab, wat_ref[...], preferred_element_type=F32)
                                 + jnp.dot(dzxb, wxt_ref[...], preferred_element_type=F32))
            return carry

        lax.fori_loop(0, n_chunks, chunk_c, 0)

        def chunk_d(ci, acc):
            r0 = pl.multiple_of(ci * R, R)
            dd = _rows_with_next(dxc_scr, r0, R, ci, n_chunks, S)
            xx = _rows_with_prev(xr_ref, r0, R, ci)
            dxc = dd[:R, :]
            out = dxc * cw_ref[K - 1:K, :]
            for j in range(1, K):
                out = out + _advanced(dd, j, R) * cw_ref[K - 1 - j:K - j, :]
            dxr_ref[pl.ds(r0, R), :] = out.astype(BF16)
            new = [acc[k] + jnp.sum(dxc * _delayed(xx, K - 1 - k, R), axis=0, keepdims=True) for k in range(K)]
            new.append(acc[K] + jnp.sum(dxc, axis=0, keepdims=True))
            return tuple(new)

        acc = lax.fori_loop(0, n_chunks, chunk_d, (z1,) * (K + 1))

        @pl.when(b == 0)
        def _():
            for k in range(K):
                dcw_ref[k:k + 1, :] = acc[k]
            dcb_ref[...] = acc[K]
            dba_ref[...] = dba
            dbx_ref[...] = dbx
            dlam_ref[...] = dlam

        @pl.when(b > 0)
        def _():
            for k in range(K):
                dcw_ref[k:k + 1, :] += acc[k]
            dcb_ref[...] += acc[K]
            dba_ref[...] += dba
            dbx_ref[...] += dbx
            dlam_ref[...] += dlam

    vec = pl.BlockSpec((1, Dh), lambda h, b: (0, h))
    wsp = pl.BlockSpec((None, Dh, Dh), lambda h, b: (h, 0, 0))
    seq = pl.BlockSpec((S, Dh), lambda h, b: (b, h))
    ck = pl.BlockSpec((K, Dh), lambda h, b: (0, h))
    row_out = jax.ShapeDtypeStruct((1, W), F32)
    return pl.pallas_call(
        body, name=name, grid=(H, batch),
        in_specs=[pl.BlockSpec((S, Dh), lambda h, b: (b, ox + h)), pl.BlockSpec((S, Dh), lambda h, b: (b, oy + h)),
                  seq, seq, ck, vec, wsp, wsp, wsp, wsp, vec, vec, vec],
        out_specs=[seq, seq, wsp, wsp, ck, vec, vec, vec, vec],
        out_shape=[jax.ShapeDtypeStruct((T, W), BF16), jax.ShapeDtypeStruct((T, W), BF16),
                   jax.ShapeDtypeStruct((H, Dh, Dh), F32), jax.ShapeDtypeStruct((H, Dh, Dh), F32),
                   jax.ShapeDtypeStruct((K, W), F32), row_out, row_out, row_out, row_out],
        scratch_shapes=[pltpu.VMEM((S, Dh), F32)] * 6,
        compiler_params=_cp("parallel", "arbitrary"),
    )(proj, proj, hseq, dyb, cw, cb, wa, wx, wat, wxt, ba, bx, lam)


def _adamw(w, g, m, v, *, name, pass_grad=False):
    R, C = w.shape
    tr, tc = _blk(R, 256), _blk(C, 1024)

    def body(w_ref, g_ref, m_ref, v_ref, *refs):
        d_ref, nm_ref, nv_ref = refs[-3:] if not pass_grad else refs[1:4]
        gv = g_ref[...]
        nm = ADAM_B1 * m_ref[...] + (1.0 - ADAM_B1) * gv
        nv = ADAM_B2 * v_ref[...] + (1.0 - ADAM_B2) * (gv * gv)
        m_hat = nm / (1.0 - ADAM_B1 ** ADAM_STEP)
        v_hat = nv / (1.0 - ADAM_B2 ** ADAM_STEP)
        d_ref[...] = -ADAM_LR * (m_hat / (jnp.sqrt(v_hat) + ADAM_EPS) + ADAM_WD * w_ref[...])
        nm_ref[...] = nm
        nv_ref[...] = nv

    blk = pl.BlockSpec((tr, tc), lambda i, j: (i, j))
    out = jax.ShapeDtypeStruct((R, C), F32)
    if not pass_grad:
        return pl.pallas_call(
            body, name=name, grid=(R // tr, C // tc), in_specs=[blk] * 4, out_specs=[blk] * 3,
            out_shape=[out] * 3, compiler_params=_cp("parallel", "parallel"),
        )(w, g, m, v)
    return pl.pallas_call(
        body, name=name, grid=(R // tr, C // tc), in_specs=[blk] * 4 + [ANY], out_specs=[blk] * 3 + [ANY],
        out_shape=[out] * 4, input_output_aliases={4: 3}, compiler_params=_cp("parallel", "parallel"),
    )(w, g, m, v, g)


def _to_slab(a, pos, dtype, *, name, b=None):
    R, C = a.shape
    tr, tc = _blk(R, 512), _blk(C, 1024)

    def body(p_ref, *refs):
        v = refs[0][...]
        if b is not None:
            v = v + refs[1][...]
        refs[-1][...] = v.astype(dtype)

    blk = pl.BlockSpec((tr, tc), lambda i, j, p_ref: (i, j))
    return pl.pallas_call(
        body, name=name,
        grid_spec=pltpu.PrefetchScalarGridSpec(
            num_scalar_prefetch=1, grid=(R // tr, C // tc), in_specs=[blk] * (1 if b is None else 2),
            out_specs=pl.BlockSpec((None, tr, tc), lambda i, j, p_ref: (p_ref[0], i, j))),
        out_shape=jax.ShapeDtypeStruct((N_CHIPS, R, C), dtype),
        compiler_params=_cp("parallel", "parallel"),
    )(pos, a, *([] if b is None else [b]))


def _sum_chips(q, *, name):
    _, R, C = q.shape
    tr = _blk(R, 1024)

    def body(q_ref, o_ref):
        o_ref[...] = ((q_ref[0] + q_ref[1]) + q_ref[2]) + q_ref[3]

    return pl.pallas_call(body, name=name, grid=(R // tr,),
                          in_specs=[pl.BlockSpec((N_CHIPS, tr, C), lambda i: (0, i, 0))],
                          out_specs=pl.BlockSpec((tr, C), lambda i: (i, 0)),
                          out_shape=jax.ShapeDtypeStruct((R, C), q.dtype), compiler_params=_cp("parallel"))(q)


def _pair_swap_halves(g, *, name):
    n, R, C = g.shape
    hr = R // 2

    def body(g_ref, o_ref, send_sem, recv_sem):
        x, y, c = _mesh_pos()
        cp = pltpu.make_async_remote_copy(
            src_ref=g_ref.at[:, pl.ds((1 - c) * hr, hr), :], dst_ref=o_ref, send_sem=send_sem, recv_sem=recv_sem,
            device_id=(x, y, 1 - c), device_id_type=MESH)
        cp.start()
        cp.wait()

    return pl.pallas_call(
        body, name=name, in_specs=[ANY], out_specs=ANY,
        out_shape=jax.ShapeDtypeStruct((n, hr, C), g.dtype),
        scratch_shapes=[pltpu.SemaphoreType.DMA, pltpu.SemaphoreType.DMA],
    )(g)


def _pair_add_halves(g, rb, cpos, *, name):
    n, R, C = g.shape
    hr = R // 2
    tr, tc = _blk(hr, 512), _blk(C, 1024)
    nrb = hr // tr

    def body(c_ref, g_ref, r_ref, o_ref):
        o_ref[...] = (g_ref[...].astype(F32) + r_ref[...].astype(F32)).astype(o_ref.dtype)

    return pl.pallas_call(
        body, name=name,
        grid_spec=pltpu.PrefetchScalarGridSpec(
            num_scalar_prefetch=1, grid=(n, nrb, C // tc),
            in_specs=[pl.BlockSpec((None, tr, tc), lambda s, i, j, c_ref: (s, c_ref[0] * nrb + i, j)),
                      pl.BlockSpec((None, tr, tc), lambda s, i, j, c_ref: (s, i, j))],
            out_specs=pl.BlockSpec((None, tr, tc), lambda s, i, j, c_ref: (s, i, j))),
        out_shape=jax.ShapeDtypeStruct((n, hr, C), g.dtype),
        compiler_params=_cp("parallel", "parallel", "parallel"),
    )(cpos, g, rb)


def _chip_final_add(p, q, pos, *, name):
    _, hr, C = p.shape
    tr, tc = _blk(hr, 512), _blk(C, 1024)
    nrb = hr // tr

    def body(k_ref, p_ref, q_ref, o_ref):
        o_ref[...] = ((p_ref[...].astype(F32) + q_ref[0].astype(F32)) + q_ref[1].astype(F32)) + q_ref[2].astype(F32)

    return pl.pallas_call(
        body, name=name,
        grid_spec=pltpu.PrefetchScalarGridSpec(
            num_scalar_prefetch=1, grid=(nrb, C // tc),
            in_specs=[pl.BlockSpec((None, tr, tc), lambda i, j, k_ref: (k_ref[0], i, j)),
                      pl.BlockSpec((N_CHIPS - 1, tr, tc), lambda i, j, k_ref: (0, i, j))],
            out_specs=pl.BlockSpec((tr, tc), lambda i, j, k_ref: (k_ref[1] * nrb + i, j))),
        out_shape=jax.ShapeDtypeStruct((2 * hr, C), F32),
        compiler_params=_cp("parallel", "parallel"),
    )(pos, p, q)


def _pair_swap(v, *, name):
    def body(v_ref, o_ref, send_sem, recv_sem):
        x, y, c = _mesh_pos()
        cp = pltpu.make_async_remote_copy(src_ref=v_ref, dst_ref=o_ref, send_sem=send_sem, recv_sem=recv_sem,
                                          device_id=(x, y, 1 - c), device_id_type=MESH)
        cp.start()
        cp.wait()

    return pl.pallas_call(
        body, name=name, in_specs=[ANY], out_specs=ANY, out_shape=jax.ShapeDtypeStruct(v.shape, v.dtype),
        scratch_shapes=[pltpu.SemaphoreType.DMA, pltpu.SemaphoreType.DMA],
    )(v)


def _pair_sum(g, cpos, *, tag):
    rb = _pair_swap_halves(g, name=f"rs_pair_swap_{tag}")
    return _pair_add_halves(g, rb, cpos, name=f"rs_pair_add_{tag}")


def _all_reduce(v, pos, *, tag):
    other = _pair_swap(v, name=f"ar_pair_swap_{tag}")
    slabs = _to_slab(v, pos, F32, b=other, name=f"ar_pair_add_{tag}")
    slabs, = _comm_call(_Carry(_GatherSlabs(slabs)), name=f"ar_allgather_{tag}")
    return _sum_chips(slabs, name=f"ar_sum_{tag}")


def _pack(arrays, unit, total_unit=None):
    parts, n = [], 0
    for a in arrays:
        flat = a.reshape(-1)
        pad = (-flat.shape[0]) % unit
        parts.append(jnp.pad(flat, (0, pad)) if pad else flat)
        n += flat.shape[0] + pad
    if total_unit and n % total_unit:
        parts.append(jnp.zeros((-n) % total_unit, arrays[0].dtype))
    return jnp.concatenate(parts).reshape(-1, LANES)


def _unpack(packed, shapes, unit):
    lead = packed.shape[:-2]
    flat = packed.reshape(lead + (-1,))
    out, pos = [], 0
    for shp in shapes:
        n = math.prod(shp)
        out.append(flat[..., pos:pos + n].reshape(lead + tuple(shp)))
        pos += n + (-n) % unit
    return out


def kernel(x, g_mix, w_in, sg_ln_g, sg_ln_b, sg_w, sg_b, lru_conv_w, lru_conv_b, lru_wa, lru_ba, lru_wx, lru_bx, lru_lam, p_sg, p_lru, w_out, g_ffn, w_up, ffn_conv_w, ffn_conv_b, w_down, g_final, loss_target, m_g_mix, m_w_in, m_sg_ln_g, m_sg_ln_b, m_sg_w, m_sg_b, m_lru_conv_w, m_lru_conv_b, m_lru_wa, m_lru_ba, m_lru_wx, m_lru_bx, m_lru_lam, m_p_sg, m_p_lru, m_w_out, m_g_ffn, m_w_up, m_ffn_conv_w, m_ffn_conv_b, m_w_down, m_g_final, v_g_mix, v_w_in, v_sg_ln_g, v_sg_ln_b, v_sg_w, v_sg_b, v_lru_conv_w, v_lru_conv_b, v_lru_wa, v_lru_ba, v_lru_wx, v_lru_bx, v_lru_lam, v_p_sg, v_p_lru, v_w_out, v_g_ffn, v_w_up, v_ffn_conv_w, v_ffn_conv_b, v_w_down, v_g_final):
    params = dict(g_mix=g_mix, w_in=w_in, sg_ln_g=sg_ln_g, sg_ln_b=sg_ln_b, sg_w=sg_w, sg_b=sg_b,
                  lru_conv_w=lru_conv_w, lru_conv_b=lru_conv_b, lru_wa=lru_wa, lru_ba=lru_ba, lru_wx=lru_wx,
                  lru_bx=lru_bx, lru_lam=lru_lam, p_sg=p_sg, p_lru=p_lru, w_out=w_out, g_ffn=g_ffn, w_up=w_up,
                  ffn_conv_w=ffn_conv_w, ffn_conv_b=ffn_conv_b, w_down=w_down, g_final=g_final)
    mom1 = dict(g_mix=m_g_mix, w_in=m_w_in, sg_ln_g=m_sg_ln_g, sg_ln_b=m_sg_ln_b, sg_w=m_sg_w, sg_b=m_sg_b,
                lru_conv_w=m_lru_conv_w, lru_conv_b=m_lru_conv_b, lru_wa=m_lru_wa, lru_ba=m_lru_ba,
                lru_wx=m_lru_wx, lru_bx=m_lru_bx, lru_lam=m_lru_lam, p_sg=m_p_sg, p_lru=m_p_lru, w_out=m_w_out,
                g_ffn=m_g_ffn, w_up=m_w_up, ffn_conv_w=m_ffn_conv_w, ffn_conv_b=m_ffn_conv_b, w_down=m_w_down,
                g_final=m_g_final)
    mom2 = dict(g_mix=v_g_mix, w_in=v_w_in, sg_ln_g=v_sg_ln_g, sg_ln_b=v_sg_ln_b, sg_w=v_sg_w, sg_b=v_sg_b,
                lru_conv_w=v_lru_conv_w, lru_conv_b=v_lru_conv_b, lru_wa=v_lru_wa, lru_ba=v_lru_ba,
                lru_wx=v_lru_wx, lru_bx=v_lru_bx, lru_lam=v_lru_lam, p_sg=v_p_sg, p_lru=v_p_lru, w_out=v_w_out,
                g_ffn=v_g_ffn, w_up=v_w_up, ffn_conv_w=v_ffn_conv_w, ffn_conv_b=v_ffn_conv_b, w_down=v_w_down,
                g_final=v_g_final)
    names = list(params)
    big = ["w_in", "p_sg", "p_lru", "w_out", "w_up", "w_down"]
    col_sharded = {"w_in", "p_sg", "w_up"}
    small = [n for n in names if n not in big]

    batch, S, D = x.shape
    T = batch * S
    W_sg = sg_ln_g.shape[-1]
    H, _, Dh = lru_wa.shape[1:]
    W_lru = H * Dh
    K_lru = lru_conv_w.shape[1]
    K_ffn = ffn_conv_w.shape[1]
    F2 = ffn_conv_b.shape[-1]
    off_lru = 2 * W_sg
    off_gate = off_lru + 2 * W_lru

    cx, cy, cc = _mesh_pos()
    chip = 2 * cx + cy
    cpos = jnp.reshape(cc, (1,)).astype(jnp.int32)
    pos = jnp.stack([chip, cc]).astype(jnp.int32)

    xf = x.reshape(T, D)
    tgt = loss_target.reshape(T, D)

    wb = {n: _to_slab(params[n][0], pos, BF16, name=f"cast_{n}") for n in big}
    rows = {n: wb[n].shape[1] for n in big}
    sharded_small = ["lru_conv_w", "ffn_conv_w", "lru_wa", "lru_wx"]
    unit_g = 2 * BF16_ROWS * LANES
    pack_g = 256 * LANES
    sm_shapes = [params[n][0].shape for n in sharded_small]
    sm = _to_slab(_pack([params[n][0] for n in sharded_small], unit_g, pack_g), pos, F32, name="slab_small")

    parts = 32

    def ici(n, lo=0, hi=parts):
        return _GatherIci(wb[n], rows[n] * lo // parts, rows[n] * hi // parts)

    def fwd(n, lo=0, hi=parts):
        return _GatherFwd(wb[n], rows[n] * lo // parts, rows[n] * hi // parts)

    w_in_g, sm = _comm_call(_Carry(_GatherRows(wb["w_in"], 0, rows["w_in"]), _GatherRows(sm, 0, sm.shape[1])),
                            name="gather_first")
    cwl_s, cwf_s, wa_s, wx_s = _unpack(sm, sm_shapes, unit_g)
    lru_cw = jnp.transpose(cwl_s, (1, 0, 2)).reshape(K_lru, W_lru)
    ffn_cw = jnp.transpose(cwf_s, (1, 0, 2)).reshape(K_ffn, F2)
    wa_full = jnp.transpose(wa_s, (1, 0, 2, 3)).reshape(H, Dh, Dh)
    wx_full = jnp.transpose(wx_s, (1, 0, 2, 3)).reshape(H, Dh, Dh)
    wa_b, wx_b = wa_full.astype(BF16), wx_full.astype(BF16)
    wat_b, wxt_b = jnp.swapaxes(wa_b, 1, 2), jnp.swapaxes(wx_b, 1, 2)

    wm, wmt = _sg_mask(sg_w[0], name="sg_mask")
    bt = sg_b[0].T

    h1 = _rms_fwd(xf, g_mix, name="rms1_fwd")
    proj, (wb["p_sg"], wb["p_lru"], wb["w_out"], wb["w_up"]) = _mm_nn(
        h1, w_in_g, out_dtype=F32, name="mm_proj",
        carry=_Carry(ici("p_sg"), ici("p_lru"), ici("w_out"), ici("w_up", 0, 8)))
    y_a, (wb["w_up"],) = _sg_fwd(proj, sg_ln_g, sg_ln_b, wm, bt, name="sg_fwd", carry=_Carry(ici("w_up", 8, 9)))
    y_b, hseq, (p_sg_g, p_lru_g, w_out_g, wb["w_up"]) = _lru_fwd(
        proj, lru_cw, lru_conv_b, wa_b, wx_b, lru_ba, lru_bx, lru_lam, batch=batch, off_x=off_lru, name="lru_fwd",
        carry=_Carry(fwd("p_sg"), fwd("p_lru"), fwd("w_out"), fwd("w_up", 0, 8), fwd("w_up", 8, 9),
                     ici("w_up", 9, 18)))
    p_lru_g = p_lru_g.reshape(-1, D)
    w_out_g = w_out_g.reshape(-1, D)
    pa, (wb["w_up"],) = _mm_nn(y_a, p_sg_g, out_dtype=F32, name="mm_pa",
                               carry=_Carry(fwd("w_up", 9, 18), ici("w_up", 18, 20)))
    pb, (wb["w_up"],) = _mm_nn(y_b, p_lru_g, out_dtype=F32, name="mm_pb",
                               carry=_Carry(fwd("w_up", 18, 20), ici("w_up", 20, 25)))
    merged, (wb["w_up"],) = _merge_fwd(proj, pa, pb, off_a=off_gate, name="merge_fwd",
                                       carry=_Carry(fwd("w_up", 20, 25), ici("w_up", 25, 28)))
    x1, (wb["w_up"],) = _mm_nn(merged, w_out_g, out_dtype=F32, res=xf, name="mm_out",
                               carry=_Carry(fwd("w_up", 25, 28), ici("w_up", 28, 32)))
    h2, (w_up_g,) = _rms_fwd(x1, g_ffn, name="rms2_fwd", carry=_Carry(fwd("w_up", 28, 32)))
    up0, (wb["w_down"],) = _mm_nn(h2, w_up_g, out_dtype=F32, name="mm_up", carry=_Carry(ici("w_down")))
    act, (w_down_g,) = _ffn_act_fwd(up0, ffn_cw, ffn_conv_b, batch=batch, name="ffn_act_fwd",
                                    carry=_Carry(fwd("w_down")))
    w_down_g = w_down_g.reshape(-1, D)
    x2 = _mm_nn(act, w_down_g, out_dtype=F32, res=x1, name="mm_down")
    lvec, dx2, dx2_b, dg_final = _loss_head(x2, tgt, g_final.reshape(1, D), name="loss_head")
    loss = lax.psum(jnp.sum(lvec) * (0.5 / D), ("x", "y", "c"))

    ps, qs = {}, {}
    g = _mm_tn(act, dx2_b, out_dtype=BF16, name="mm_dw_down").reshape(N_CHIPS, -1, D)
    ps["w_down"] = _pair_sum(g, cpos, tag="w_down")
    dact, (qs["w_down"],) = _mm_nt(dx2_b, w_down_g, out_dtype=F32, name="mm_dact",
                                   carry=_Carry(_ChipExchange(ps["w_down"])))
    dug, duv, d_ffn_cw, d_ffn_cb = _ffn_act_bwd(up0, ffn_cw, ffn_conv_b, dact, batch=batch, name="ffn_act_bwd")
    dup0 = jnp.concatenate([dug, duv], axis=1)
    g = _mm_tn(h2, dup0, out_dtype=BF16, col_shards=N_CHIPS, name="mm_dw_up")
    ps["w_up"] = _pair_sum(g, cpos, tag="w_up")
    dh2, (qs["w_up"],) = _mm_nt(dup0, w_up_g, out_dtype=F32, name="mm_dh2", carry=_Carry(_ChipExchange(ps["w_up"])))
    dx1, dx1_b, dg_ffn = _rms_bwd(x1, g_ffn, dh2, dx2, name="rms2_bwd")
    g = _mm_tn(merged, dx1_b, out_dtype=BF16, name="mm_dw_out").reshape(N_CHIPS, -1, D)
    ps["w_out"] = _pair_sum(g, cpos, tag="w_out")
    dmerged, (qs["w_out"],) = _mm_nt(dx1_b, w_out_g, out_dtype=F32, name="mm_dmerged",
                                     carry=_Carry(_ChipExchange(ps["w_out"])))
    dga, dgb, dpa, dpb = _merge_bwd(proj, pa, pb, dmerged, off_a=off_gate, name="merge_bwd")
    g = _mm_tn(y_a, dpa, out_dtype=BF16, col_shards=N_CHIPS, name="mm_dp_sg")
    ps["p_sg"] = _pair_sum(g, cpos, tag="p_sg")
    g = _mm_tn(y_b, dpb, out_dtype=BF16, name="mm_dp_lru").reshape(N_CHIPS, -1, D)
    ps["p_lru"] = _pair_sum(g, cpos, tag="p_lru")
    dya, (qs["p_sg"],) = _mm_nt(dpa, p_sg_g, out_dtype=F32, name="mm_dya", carry=_Carry(_ChipExchange(ps["p_sg"])))
    dyb, (qs["p_lru"],) = _mm_nt(dpb, p_lru_g, out_dtype=F32, name="mm_dyb",
                                 carry=_Carry(_ChipExchange(ps["p_lru"])))
    dxr, dyr, d_wa, d_wx, d_lru_cw, d_lru_cb, d_ba, d_bx, d_lam = _lru_bwd(
        proj, hseq, dyb, lru_cw, lru_conv_b, wa_b, wx_b, wat_b, wxt_b, lru_ba, lru_bx, lru_lam,
        batch=batch, off_x=off_lru, name="lru_bwd")
    g = jnp.stack([d_wa, d_wx]).reshape(2, H, N_CHIPS, Dh // N_CHIPS, Dh)
    g = jnp.transpose(g, (2, 0, 1, 3, 4)).reshape(N_CHIPS, -1, Dh).astype(BF16)
    ps["gates"] = _pair_sum(g, cpos, tag="gates")
    dzuv, d_wm, d_bt, d_lg, d_lb = _sg_bwd(proj, dya, sg_ln_g, sg_ln_b, wm, wmt, bt, name="sg_bwd")
    dproj = jnp.concatenate([dzuv, dxr, dyr, dga, dgb], axis=1)
    g, (qs["gates"],) = _mm_tn(h1, dproj, out_dtype=BF16, col_shards=N_CHIPS, name="mm_dw_in",
                               carry=_Carry(_ChipExchange(ps["gates"])))
    ps["w_in"] = _pair_sum(g, cpos, tag="w_in")
    dh1, (qs["w_in"],) = _mm_nt(dproj, w_in_g, out_dtype=F32, name="mm_dh1", carry=_Carry(_ChipExchange(ps["w_in"])))
    dx, _, dg_mix = _rms_bwd(xf, g_mix, dh1, dx1, name="rms1_bwd")

    scattered = big + ["gates"]
    halves = [_chip_final_add(ps[n], qs[n], pos, name=f"rs_final_add_{n}") for n in scattered]
    grads = dict(zip(scattered, _comm_call(_Carry(*[_ShareHalves(h) for h in halves]), name="rs_share")))
    d_gates = grads.pop("gates").reshape(2, H, Dh // N_CHIPS, Dh)
    grads["lru_wa"], grads["lru_wx"] = d_gates[0].reshape(lru_wa.shape), d_gates[1].reshape(lru_wx.shape)
    small_full = dict(g_mix=dg_mix, sg_ln_g=d_lg, sg_ln_b=d_lb, sg_w=d_wm, sg_b=d_bt.T, lru_conv_w=d_lru_cw,
                      lru_conv_b=d_lru_cb, lru_ba=d_ba, lru_bx=d_bx, lru_lam=d_lam,
                      g_ffn=dg_ffn, ffn_conv_w=d_ffn_cw, ffn_conv_b=d_ffn_cb, g_final=dg_final)
    reduced = list(small_full)
    unit_s = SUBLANES * LANES
    pack_s = 512 * LANES
    red = _all_reduce(_pack([small_full[n] for n in reduced], unit_s, pack_s), pos, tag="small")
    red = dict(zip(reduced, _unpack(red, [small_full[n].shape for n in reduced], unit_s)))
    cs_lru = W_lru // N_CHIPS
    cs_ffn = F2 // N_CHIPS
    red["lru_conv_w"] = lax.dynamic_slice_in_dim(red["lru_conv_w"], chip * cs_lru, cs_lru, axis=1)
    red["ffn_conv_w"] = lax.dynamic_slice_in_dim(red["ffn_conv_w"], chip * cs_ffn, cs_ffn, axis=1)
    for n in reduced:
        grads[n] = red[n].reshape(params[n].shape)

    delta, new_m, new_v = {}, {}, {}
    for n in big:
        shp = params[n].shape
        two_d = (-1, shp[-1])
        d, nm, nv, gr = _adamw(params[n].reshape(two_d), grads[n], mom1[n].reshape(two_d),
                               mom2[n].reshape(two_d), name=f"adamw_{n}", pass_grad=True)
        delta[n], new_m[n], new_v[n], grads[n] = d.reshape(shp), nm.reshape(shp), nv.reshape(shp), gr.reshape(shp)
    packs = [_pack([src[n] for n in small], unit_s, pack_s) for src in (params, grads, mom1, mom2)]
    outs = _adamw(*packs, name="adamw_small")
    shapes = [params[n].shape for n in small]
    for dst, packed in zip((delta, new_m, new_v), outs):
        dst.update(dict(zip(small, _unpack(packed, shapes, unit_s))))

    return (loss, dx.reshape(x.shape), *[grads[n] for n in names], *[delta[n] for n in names],
            *[new_m[n] for n in names], *[new_v[n] for n in names])
```

```python
import math

import jax
import jax.numpy as jnp
from jax import lax
from jax.experimental import pallas as pl
from jax.experimental.pallas import tpu as pltpu

F32 = jnp.float32
BF16 = jnp.bfloat16
MESH = pl.DeviceIdType.MESH
ANY = pl.BlockSpec(memory_space=pl.ANY)

EPS = 1e-6
LRU_C = 8.0
ADAM_LR = 0.001
ADAM_B1 = 0.9
ADAM_B2 = 0.999
ADAM_EPS = 1e-08
ADAM_WD = 0.01
ADAM_STEP = 10

N_CHIPS = 4
SUBLANES = 8
BF16_ROWS = 16
LANES = 128
VMEM_LIMIT = 56 * 1024 * 1024
DPROJ_PIECES = 5
DPROJ_SHIFT = 4
GELU_C = math.sqrt(2.0 / math.pi)
GELU_K = 0.044715


def _cp(*sem):
    return pltpu.CompilerParams(dimension_semantics=sem, vmem_limit_bytes=VMEM_LIMIT)


def _blk(dim, pref):
    if dim <= pref:
        return dim
    b = pref
    while dim % b:
        b //= 2
    return b


def _gelu(x):
    t = jnp.tanh(GELU_C * (x + GELU_K * x * x * x))
    return 0.5 * x * (1.0 + t)


def _gelu_and_grad(x):
    x2 = x * x
    t = jnp.tanh(GELU_C * (x + GELU_K * x * x2))
    g = 0.5 * x * (1.0 + t)
    dg = 0.5 * (1.0 + t) + 0.5 * x * (1.0 - t * t) * (GELU_C * (1.0 + 3.0 * GELU_K * x2))
    return g, dg


def _sigmoid(x):
    return 1.0 / (1.0 + jnp.exp(-x))


def _softplus(x):
    e = jnp.exp(-jnp.abs(x))
    series = e * (1.0 - e * (0.5 - e * (1.0 / 3.0 - e * (0.25 - e * 0.2))))
    return jnp.where(e < 0.01, series, jnp.log(1.0 + e)) + jnp.maximum(x, 0.0)


def _neg_expm1(x):
    series = -(x * (1.0 + x * (0.5 + x * (1.0 / 6.0 + x * (1.0 / 24.0)))))
    return jnp.where(x > -0.01, series, 1.0 - jnp.exp(x))


def _mesh_pos():
    return lax.axis_index("x"), lax.axis_index("y"), lax.axis_index("c")


def _other_chips(x, y):
    return [(1 - x, y), (x, 1 - y), (1 - x, 1 - y)]


def _remote(k, src, dst, to, send_sems, recv_sems):
    return pltpu.make_async_remote_copy(src_ref=src, dst_ref=dst, send_sem=send_sems.at[k],
                                        recv_sem=recv_sems.at[k], device_id=to, device_id_type=MESH)


class _GatherRows:
    n_sems = 6

    def __init__(self, buf, r0, r1):
        self.args = [buf]
        self.out_shape = [jax.ShapeDtypeStruct(buf.shape, buf.dtype)]
        self.aliases = {0: 0}
        self.r0, self.h = r0, (r1 - r0) // 2

    def _rows(self, half):
        return pl.ds(self.r0 + half * self.h, self.h)

    def start(self, ins, outs, ss, rs, base):
        x, y, c = _mesh_pos()
        mine = ins[0].at[2 * x + y, self._rows(c), :]
        for j, (px, py) in enumerate(_other_chips(x, y)):
            _remote(base + j, mine, outs[0].at[2 * x + y, self._rows(c), :], (px, py, c), ss, rs).start()

    def finish(self, ins, outs, ss, rs, base):
        x, y, c = _mesh_pos()
        sibling = (x, y, 1 - c)
        chips = _other_chips(x, y)
        mine = ins[0].at[2 * x + y, self._rows(c), :]
        for j, (px, py) in enumerate(chips):
            got = outs[0].at[2 * px + py, self._rows(c), :]
            _remote(base + j, got, got, (px, py, c), ss, rs).wait_recv()
            _remote(base + 3 + j, got, got, sibling, ss, rs).start()
        for j, (px, py) in enumerate(chips):
            fwd = outs[0].at[2 * px + py, self._rows(1 - c), :]
            _remote(base + 3 + j, fwd, fwd, sibling, ss, rs).wait_recv()
        for j, (px, py) in enumerate(chips):
            got = outs[0].at[2 * px + py, self._rows(c), :]
            _remote(base + j, mine, mine, (px, py, c), ss, rs).wait_send()
            _remote(base + 3 + j, got, got, sibling, ss, rs).wait_send()


class _GatherIci(_GatherRows):
    n_sems = 3

    def finish(self, ins, outs, ss, rs, base):
        x, y, c = _mesh_pos()
        mine = ins[0].at[2 * x + y, self._rows(c), :]
        for j, (px, py) in enumerate(_other_chips(x, y)):
            got = outs[0].at[2 * px + py, self._rows(c), :]
            _remote(base + j, got, got, (px, py, c), ss, rs).wait_recv()
            _remote(base + j, mine, mine, (px, py, c), ss, rs).wait_send()


class _GatherFwd(_GatherRows):
    n_sems = 3

    def start(self, ins, outs, ss, rs, base):
        x, y, c = _mesh_pos()
        for j, (px, py) in enumerate(_other_chips(x, y)):
            _remote(base + j, ins[0].at[2 * px + py, self._rows(c), :], outs[0].at[2 * px + py, self._rows(c), :],
                    (x, y, 1 - c), ss, rs).start()

    def finish(self, ins, outs, ss, rs, base):
        x, y, c = _mesh_pos()
        for j, (px, py) in enumerate(_other_chips(x, y)):
            got = ins[0].at[2 * px + py, self._rows(c), :]
            fwd = outs[0].at[2 * px + py, self._rows(1 - c), :]
            _remote(base + j, got, got, (x, y, 1 - c), ss, rs).wait_send()
            _remote(base + j, fwd, fwd, (x, y, 1 - c), ss, rs).wait_recv()


class _ChipExchange:
    n_sems = 3

    def __init__(self, p):
        self.args = [p]
        self.out_shape = [jax.ShapeDtypeStruct((N_CHIPS - 1,) + p.shape[1:], p.dtype)]
        self.aliases = {}

    def _copies(self, ins, outs, ss, rs, base):
        x, y, c = _mesh_pos()
        return [_remote(base + j, ins[0].at[2 * px + py], outs[0].at[j], (px, py, c), ss, rs)
                for j, (px, py) in enumerate(_other_chips(x, y))]

    def start(self, ins, outs, ss, rs, base):
        for cp in self._copies(ins, outs, ss, rs, base):
            cp.start()

    def finish(self, ins, outs, ss, rs, base):
        for cp in self._copies(ins, outs, ss, rs, base):
            cp.wait()


class _PairSwap:
    n_sems = 1

    def __init__(self, g):
        n, R, C = g.shape
        self.args = [g]
        self.out_shape = [jax.ShapeDtypeStruct((n, R // 2, C), g.dtype)]
        self.aliases = {}
        self.hr = R // 2

    def _copy(self, ins, outs, ss, rs, base):
        x, y, c = _mesh_pos()
        return _remote(base, ins[0].at[:, pl.ds((1 - c) * self.hr, self.hr), :], outs[0], (x, y, 1 - c), ss, rs)

    def start(self, ins, outs, ss, rs, base):
        self._copy(ins, outs, ss, rs, base).start()

    def finish(self, ins, outs, ss, rs, base):
        self._copy(ins, outs, ss, rs, base).wait()


class _ShareHalves:
    n_sems = 1

    def __init__(self, buf):
        self.args = [buf]
        self.out_shape = [jax.ShapeDtypeStruct(buf.shape, buf.dtype)]
        self.aliases = {0: 0}
        self.hr = buf.shape[0] // 2

    def start(self, ins, outs, ss, rs, base):
        x, y, c = _mesh_pos()
        rows = pl.ds(c * self.hr, self.hr)
        _remote(base, ins[0].at[rows, :], outs[0].at[rows, :], (x, y, 1 - c), ss, rs).start()

    def finish(self, ins, outs, ss, rs, base):
        x, y, c = _mesh_pos()
        mine = ins[0].at[pl.ds(c * self.hr, self.hr), :]
        theirs = outs[0].at[pl.ds((1 - c) * self.hr, self.hr), :]
        _remote(base, mine, mine, (x, y, 1 - c), ss, rs).wait_send()
        _remote(base, theirs, theirs, (x, y, 1 - c), ss, rs).wait_recv()


class _GatherSlabs:
    n_sems = 3

    def __init__(self, buf):
        self.args = [buf]
        self.out_shape = [jax.ShapeDtypeStruct(buf.shape, buf.dtype)]
        self.aliases = {0: 0}

    def start(self, ins, outs, ss, rs, base):
        x, y, c = _mesh_pos()
        for j, (px, py) in enumerate(_other_chips(x, y)):
            _remote(base + j, ins[0].at[2 * x + y], outs[0].at[2 * x + y], (px, py, c), ss, rs).start()

    def finish(self, ins, outs, ss, rs, base):
        x, y, c = _mesh_pos()
        mine = ins[0].at[2 * x + y]
        for j, (px, py) in enumerate(_other_chips(x, y)):
            got = outs[0].at[2 * px + py]
            _remote(base + j, mine, mine, (px, py, c), ss, rs).wait_send()
            _remote(base + j, got, got, (px, py, c), ss, rs).wait_recv()


class _Carry:
    def __init__(self, *items):
        self.items = items
        self.args, self.out_shape, self._alias, self._slots = [], [], {}, []
        seen = {}
        for it in items:
            key = id(it.args[0]) if it.aliases else None
            if key is None or key not in seen:
                slot = (len(self.args), len(self.out_shape))
                if it.aliases:
                    seen[key] = slot
                    self._alias[slot[0]] = slot[1]
                self.args.append(it.args[0])
                self.out_shape.append(it.out_shape[0])
            else:
                slot = seen[key]
            self._slots.append(slot)
        self.n_sems = sum(it.n_sems for it in items)

    def aliases(self, in_base, out_base):
        return {in_base + i: out_base + o for i, o in self._alias.items()}

    def _each(self, method, ins, outs, ss, rs):
        base = 0
        for it, (i, o) in zip(self.items, self._slots):
            getattr(it, method)([ins[i]], [outs[o]], ss, rs, base)
            base += it.n_sems

    def start(self, ins, outs, ss, rs):
        self._each("start", ins, outs, ss, rs)

    def finish(self, ins, outs, ss, rs):
        self._each("finish", ins, outs, ss, rs)


def _carried_call(body, *, name, grid, in_specs, out_specs, out_shape, scratch_shapes, args, semantics, carry=None):
    if carry is None:
        outs = pl.pallas_call(body, name=name, grid=grid, in_specs=in_specs, out_specs=out_specs,
                              out_shape=out_shape, scratch_shapes=scratch_shapes,
                              compiler_params=_cp(*semantics))(*args)
        return list(outs), []
    n_in, n_out, n_scr = len(in_specs), len(out_specs), len(scratch_shapes)
    n_cin, n_cout = len(carry.args), len(carry.out_shape)

    def full(*refs):
        ins = refs[:n_in]
        cins = refs[n_in:n_in + n_cin]
        outs = refs[n_in + n_cin:n_in + n_cin + n_out]
        couts = refs[n_in + n_cin + n_out:n_in + n_cin + n_out + n_cout]
        scr = refs[n_in + n_cin + n_out + n_cout:n_in + n_cin + n_out + n_cout + n_scr]
        ss, rs = refs[-2], refs[-1]
        first = pl.program_id(0) == 0
        last = pl.program_id(0) == grid[0] - 1
        for d in range(1, len(grid)):
            first = jnp.logical_and(first, pl.program_id(d) == 0)
            last = jnp.logical_and(last, pl.program_id(d) == grid[d] - 1)

        @pl.when(first)
        def _():
            carry.start(cins, couts, ss, rs)

        body(*ins, *outs, *scr)

        @pl.when(last)
        def _():
            carry.finish(cins, couts, ss, rs)

    outs = pl.pallas_call(
        full, name=name, grid=grid, in_specs=list(in_specs) + [ANY] * n_cin,
        out_specs=list(out_specs) + [ANY] * n_cout, out_shape=list(out_shape) + carry.out_shape,
        scratch_shapes=list(scratch_shapes) + [pltpu.SemaphoreType.DMA((carry.n_sems,))] * 2,
        input_output_aliases=carry.aliases(n_in, n_out),
        compiler_params=_cp(*(("arbitrary",) * len(grid))),
    )(*args, *carry.args)
    return list(outs[:n_out]), list(outs[n_out:])


def _comm_call(carry, *, name):
    n_cin = len(carry.args)

    def body(*refs):
        cins, couts = refs[:n_cin], refs[n_cin:-2]
        carry.start(cins, couts, refs[-2], refs[-1])
        carry.finish(cins, couts, refs[-2], refs[-1])

    outs = pl.pallas_call(
        body, name=name, in_specs=[ANY] * n_cin, out_specs=[ANY] * len(carry.out_shape), out_shape=carry.out_shape,
        scratch_shapes=[pltpu.SemaphoreType.DMA((carry.n_sems,))] * 2,
        input_output_aliases=carry.aliases(0, 0),
    )(*carry.args)
    return list(outs)


def _mm_nn(a, b, *, out_dtype, name, res=None, carry=None):
    M, K = a.shape
    cs = b.shape[-1]
    N = cs * (b.shape[0] if b.ndim == 3 else 1)
    tm, tn, tk = _blk(M, 1024 if res is None else 512), _blk(cs, 1024), _blk(K, 4096)
    nbs, nk = cs // tn, K // tk
    if b.ndim == 3:
        b_spec = pl.BlockSpec((None, tk, tn), lambda i, j, k: (j // nbs, k, j % nbs))
    else:
        b_spec = pl.BlockSpec((tk, tn), lambda i, j, k: (k, j))
    in_specs = [pl.BlockSpec((tm, tk), lambda i, j, k: (i, k)), b_spec]
    args = [a, b]
    if res is not None:
        in_specs.append(pl.BlockSpec((tm, tn), lambda i, j, k: (i, j)))
        args.append(res)

    def body(*refs):
        a_ref, b_ref = refs[0], refs[1]
        r_ref = refs[2] if res is not None else None
        p = jnp.dot(a_ref[...], b_ref[...], preferred_element_type=F32)
        if nk == 1:
            o_ref = refs[-1]
            o_ref[...] = (p if r_ref is None else p + r_ref[...]).astype(out_dtype)
            return
        o_ref, acc = refs[-2], refs[-1]
        k = pl.program_id(2)

        @pl.when(k == 0)
        def _():
            acc[...] = p

        @pl.when(k > 0)
        def _():
            acc[...] += p

        @pl.when(k == nk - 1)
        def _():
            r = acc[...]
            if r_ref is not None:
                r = r + r_ref[...]
            o_ref[...] = r.astype(out_dtype)

    outs, carried = _carried_call(
        body, name=name, grid=(M // tm, N // tn, nk), in_specs=in_specs,
        out_specs=[pl.BlockSpec((tm, tn), lambda i, j, k: (i, j))],
        out_shape=[jax.ShapeDtypeStruct((M, N), out_dtype)],
        scratch_shapes=[pltpu.VMEM((tm, tn), F32)] if nk > 1 else [], args=args,
        semantics=("parallel", "parallel", "arbitrary"), carry=carry)
    return outs[0] if carry is None else (outs[0], carried)


def _mm_nt(a, b, *, out_dtype, name, carry=None, a_shift=0):
    M = a.shape[-2]
    wp = a.shape[-1]
    Kc = wp * (a.shape[0] if a.ndim == 3 else 1)
    cs = b.shape[-1]
    N = b.shape[-2]
    tm, tn, tk = _blk(M, 1024), _blk(N, 1024), _blk(math.gcd(cs, wp), 4096)
    nks, nka, nk = cs // tk, wp // tk, Kc // tk
    if b.ndim == 3:
        b_spec = pl.BlockSpec((None, tn, tk), lambda i, j, k: (k // nks, j, k % nks))
    else:
        b_spec = pl.BlockSpec((tn, tk), lambda i, j, k: (j, k))
    if a.ndim == 3:
        n_pieces = a.shape[0]
        a_spec = pl.BlockSpec((None, tm, tk), lambda i, j, k: ((k // nka + a_shift) % n_pieces, i, k % nka))
    else:
        a_spec = pl.BlockSpec((tm, tk), lambda i, j, k: (i, k))

    def body(a_ref, b_ref, o_ref, *scr):
        p = lax.dot_general(a_ref[...], b_ref[...], (((1,), (1,)), ((), ())), preferred_element_type=F32)
        if nk == 1:
            o_ref[...] = p.astype(out_dtype)
            return
        acc = scr[0]
        k = pl.program_id(2)

        @pl.when(k == 0)
        def _():
            acc[...] = p

        @pl.when(k > 0)
        def _():
            acc[...] += p

        @pl.when(k == nk - 1)
        def _():
            o_ref[...] = acc[...].astype(out_dtype)

    outs, carried = _carried_call(
        body, name=name, grid=(M // tm, N // tn, nk),
        in_specs=[a_spec, b_spec],
        out_specs=[pl.BlockSpec((tm, tn), lambda i, j, k: (i, j))],
        out_shape=[jax.ShapeDtypeStruct((M, N), out_dtype)],
        scratch_shapes=[pltpu.VMEM((tm, tn), F32)] if nk > 1 else [], args=[a, b],
        semantics=("parallel", "parallel", "arbitrary"), carry=carry)
    return outs[0] if carry is None else (outs[0], carried)


def _mm_tn(a, b, *, out_dtype, name, col_shards=None, carry=None, b_shift=0):
    T, K1 = a.shape
    wp = b.shape[-1]
    N = wp * (b.shape[0] if b.ndim == 3 else 1)
    cs = N // col_shards if col_shards else N
    tm, tn, tk = _blk(K1, 1024), _blk(math.gcd(cs, wp), 1024), _blk(T, 4096)
    nbs, njb, nk = cs // tn, wp // tn, T // tk
    if b.ndim == 3:
        n_pieces = b.shape[0]
        b_spec = pl.BlockSpec((None, tk, tn), lambda i, j, k: ((j // njb + b_shift) % n_pieces, k, j % njb))
    else:
        b_spec = pl.BlockSpec((tk, tn), lambda i, j, k: (k, j))
    if col_shards:
        o_spec = pl.BlockSpec((None, tm, tn), lambda i, j, k: (j // nbs, i, j % nbs))
        o_shape = jax.ShapeDtypeStruct((col_shards, K1, cs), out_dtype)
    else:
        o_spec = pl.BlockSpec((tm, tn), lambda i, j, k: (i, j))
        o_shape = jax.ShapeDtypeStruct((K1, N), out_dtype)

    def body(a_ref, b_ref, o_ref, *scr):
        p = lax.dot_general(a_ref[...], b_ref[...], (((0,), (0,)), ((), ())), preferred_element_type=F32)
        if nk == 1:
            o_ref[...] = p.astype(out_dtype)
            return
        acc = scr[0]
        k = pl.program_id(2)

        @pl.when(k == 0)
        def _():
            acc[...] = p

        @pl.when(k > 0)
        def _():
            acc[...] += p

        @pl.when(k == nk - 1)
        def _():
            o_ref[...] = acc[...].astype(out_dtype)

    outs, carried = _carried_call(
        body, name=name, grid=(K1 // tm, N // tn, nk),
        in_specs=[pl.BlockSpec((tk, tm), lambda i, j, k: (k, i)), b_spec],
        out_specs=[o_spec], out_shape=[o_shape],
        scratch_shapes=[pltpu.VMEM((tm, tn), F32)] if nk > 1 else [], args=[a, b],
        semantics=("parallel", "parallel", "arbitrary"), carry=carry)
    return outs[0] if carry is None else (outs[0], carried)


def _rms_fwd(x, g, *, name, carry=None):
    T, D = x.shape
    tm = _blk(T, 256)

    def body(x_ref, g_ref, o_ref):
        xv = x_ref[...]
        r = lax.rsqrt(jnp.mean(xv * xv, axis=-1, keepdims=True) + EPS)
        o_ref[...] = (xv * r * g_ref[...]).astype(BF16)

    outs, carried = _carried_call(
        body, name=name, grid=(T // tm,),
        in_specs=[pl.BlockSpec((tm, D), lambda i: (i, 0)), pl.BlockSpec((1, D), lambda i: (0, 0))],
        out_specs=[pl.BlockSpec((tm, D), lambda i: (i, 0))],
        out_shape=[jax.ShapeDtypeStruct((T, D), BF16)], scratch_shapes=[], args=[x, g],
        semantics=("parallel",), carry=carry)
    return outs[0] if carry is None else (outs[0], carried)


def _rms_bwd(x, g, dh, dres, *, name):
    T, D = x.shape
    tm = _blk(T, 256)

    def body(x_ref, g_ref, dh_ref, dres_ref, dx_ref, dxb_ref, dg_ref):
        i = pl.program_id(0)
        xv = x_ref[...]
        r = lax.rsqrt(jnp.mean(xv * xv, axis=-1, keepdims=True) + EPS)
        n = xv * r
        dh_v = dh_ref[...]
        dn = dh_v * g_ref[...]
        dx = dres_ref[...] + r * (dn - n * jnp.mean(dn * n, axis=-1, keepdims=True))
        dx_ref[...] = dx
        dxb_ref[...] = dx.astype(BF16)
        part = jnp.sum(dh_v * n, axis=0, keepdims=True)

        @pl.when(i == 0)
        def _():
            dg_ref[...] = part

        @pl.when(i > 0)
        def _():
            dg_ref[...] += part

    row = pl.BlockSpec((tm, D), lambda i: (i, 0))
    vec = pl.BlockSpec((1, D), lambda i: (0, 0))
    return pl.pallas_call(
        body, name=name, grid=(T // tm,),
        in_specs=[row, vec, row, row], out_specs=[row, row, vec],
        out_shape=[jax.ShapeDtypeStruct((T, D), F32), jax.ShapeDtypeStruct((T, D), BF16),
                   jax.ShapeDtypeStruct((1, D), F32)],
        compiler_params=_cp("arbitrary"),
    )(x, g, dh, dres)


def _loss_head(x2, tgt, g, *, name):
    T, D = x2.shape
    tm = _blk(T, 256)

    def body(x_ref, t_ref, g_ref, l_ref, dx_ref, dxb_ref, dg_ref):
        i = pl.program_id(0)
        xv = x_ref[...]
        gv = g_ref[...]
        r = lax.rsqrt(jnp.mean(xv * xv, axis=-1, keepdims=True) + EPS)
        n = xv * r
        err = n * gv - t_ref[...]
        dy = err * (1.0 / D)
        dn = dy * gv
        dx = r * (dn - n * jnp.mean(dn * n, axis=-1, keepdims=True))
        dx_ref[...] = dx
        dxb_ref[...] = dx.astype(BF16)
        lpart = jnp.sum(err * err, axis=0, keepdims=True)
        gpart = jnp.sum(dy * n, axis=0, keepdims=True)

        @pl.when(i == 0)
        def _():
            l_ref[...] = lpart
            dg_ref[...] = gpart

        @pl.when(i > 0)
        def _():
            l_ref[...] += lpart
            dg_ref[...] += gpart

    row = pl.BlockSpec((tm, D), lambda i: (i, 0))
    vec = pl.BlockSpec((1, D), lambda i: (0, 0))
    return pl.pallas_call(
        body, name=name, grid=(T // tm,),
        in_specs=[row, row, vec], out_specs=[vec, row, row, vec],
        out_shape=[jax.ShapeDtypeStruct((1, D), F32), jax.ShapeDtypeStruct((T, D), F32),
                   jax.ShapeDtypeStruct((T, D), BF16), jax.ShapeDtypeStruct((1, D), F32)],
        compiler_params=_cp("arbitrary"),
    )(x2, tgt, g)


def _merge_fwd(proj, pa, pb, *, off_a, name, carry=None):
    T, D = pa.shape
    tm, tn = _blk(T, 256), _blk(D, 1024)
    oa, ob = off_a // tn, (off_a + D) // tn

    def body(ga_ref, gb_ref, pa_ref, pb_ref, o_ref):
        o_ref[...] = (_sigmoid(ga_ref[...]) * pa_ref[...] + _sigmoid(gb_ref[...]) * pb_ref[...]).astype(BF16)

    blk = pl.BlockSpec((tm, tn), lambda i, j: (i, j))
    outs, carried = _carried_call(
        body, name=name, grid=(T // tm, D // tn),
        in_specs=[pl.BlockSpec((tm, tn), lambda i, j: (i, oa + j)),
                  pl.BlockSpec((tm, tn), lambda i, j: (i, ob + j)), blk, blk],
        out_specs=[blk], out_shape=[jax.ShapeDtypeStruct((T, D), BF16)], scratch_shapes=[],
        args=[proj, proj, pa, pb], semantics=("parallel", "parallel"), carry=carry)
    return outs[0] if carry is None else (outs[0], carried)


def _merge_bwd(proj, pa, pb, dm, *, off_a, name):
    T, D = pa.shape
    tm, tn = _blk(T, 256), _blk(D, 1024)
    oa, ob = off_a // tn, (off_a + D) // tn

    def body(ga_ref, gb_ref, pa_ref, pb_ref, dm_ref, dg_ref, dpa_ref, dpb_ref):
        dmv = dm_ref[...]
        sa = _sigmoid(ga_ref[...])
        sb = _sigmoid(gb_ref[...])
        dg_ref[0] = (dmv * pa_ref[...] * sa * (1.0 - sa)).astype(BF16)
        dg_ref[1] = (dmv * pb_ref[...] * sb * (1.0 - sb)).astype(BF16)
        dpa_ref[...] = (dmv * sa).astype(BF16)
        dpb_ref[...] = (dmv * sb).astype(BF16)

    blk = pl.BlockSpec((tm, tn), lambda i, j: (i, j))
    out = jax.ShapeDtypeStruct((T, D), BF16)
    return pl.pallas_call(
        body, name=name, grid=(T // tm, D // tn),
        in_specs=[pl.BlockSpec((tm, tn), lambda i, j: (i, oa + j)),
                  pl.BlockSpec((tm, tn), lambda i, j: (i, ob + j)), blk, blk, blk],
        out_specs=[pl.BlockSpec((2, tm, tn), lambda i, j: (1, i, j)), blk, blk],
        out_shape=[jax.ShapeDtypeStruct((DPROJ_PIECES, T, D), BF16), out, out],
        compiler_params=_cp("parallel", "parallel"),
    )(proj, proj, pa, pb, dm)


def _sg_mask(sg_w, *, name):
    G, C, _ = sg_w.shape

    def body(w_ref, m_ref, mt_ref):
        row = lax.broadcasted_iota(jnp.int32, (C, C), 0)
        col = lax.broadcasted_iota(jnp.int32, (C, C), 1)
        for g in range(G):
            w = jnp.where(row >= col, w_ref[g], 0.0)
            m_ref[g] = w.astype(BF16)
            mt_ref[g] = w.T.astype(BF16)

    out = jax.ShapeDtypeStruct((G, C, C), BF16)
    return pl.pallas_call(body, name=name, out_shape=[out, out])(sg_w)


def _sg_layernorm(zv, lg, lb):
    v = _gelu(zv)
    mu = jnp.mean(v, axis=-1, keepdims=True)
    xc = v - mu
    rstd = lax.rsqrt(jnp.mean(xc * xc, axis=-1, keepdims=True) + EPS)
    vhat = xc * rstd
    return vhat, rstd, vhat * lg + lb


def _sg_fwd(proj, lg, lb, wm, bt, *, name, carry=None):
    T = proj.shape[0]
    G, C, _ = wm.shape
    W = lg.shape[-1]
    gd = W // G

    def body(zu_ref, zv_ref, lg_ref, lb_ref, wm_ref, bt_ref, ya_ref, vn_scr):
        _, _, vn = _sg_layernorm(zv_ref[...], lg_ref[...], lb_ref[...])
        vn_scr[...] = vn.astype(BF16)
        for g in range(G):
            cols = slice(g * gd, (g + 1) * gd)
            mixed = jnp.dot(wm_ref[g], vn_scr[:, cols], preferred_element_type=F32) + bt_ref[:, g:g + 1]
            ya_ref[:, cols] = (_gelu(zu_ref[:, cols]) * mixed).astype(BF16)

    vec = pl.BlockSpec((1, W), lambda i: (0, 0))
    outs, carried = _carried_call(
        body, name=name, grid=(T // C,),
        in_specs=[pl.BlockSpec((C, W), lambda i: (i, 0)), pl.BlockSpec((C, W), lambda i: (i, 1)), vec, vec,
                  pl.BlockSpec((G, C, C), lambda i: (0, 0, 0)), pl.BlockSpec((C, G), lambda i: (0, 0))],
        out_specs=[pl.BlockSpec((C, W), lambda i: (i, 0))],
        out_shape=[jax.ShapeDtypeStruct((T, W), BF16)],
        scratch_shapes=[pltpu.VMEM((C, W), BF16)], args=[proj, proj, lg, lb, wm, bt],
        semantics=("parallel",), carry=carry)
    return outs[0] if carry is None else (outs[0], carried)


def _sg_bwd(proj, dya, dproj, lg, lb, wm, wmt, bt, *, name):
    T = proj.shape[0]
    G, C, _ = wm.shape
    W = lg.shape[-1]
    gd = W // G
    n_steps = T // C

    def body(zu_ref, zv_ref, dya_ref, lg_ref, lb_ref, wm_ref, wmt_ref, bt_ref, dproj_in_ref,
             dz_ref, dwm_ref, dbt_ref, dlg_ref, dlb_ref, vn_scr, dvn_scr):
        i = pl.program_id(0)

        @pl.when(i == 0)
        def _():
            dwm_ref[...] = jnp.zeros_like(dwm_ref)
            dbt_ref[...] = jnp.zeros_like(dbt_ref)
            dlg_ref[...] = jnp.zeros_like(dlg_ref)
            dlb_ref[...] = jnp.zeros_like(dlb_ref)

        lgv = lg_ref[...]
        vhat, rstd, vn = _sg_layernorm(zv_ref[...], lgv, lb_ref[...])
        vn_scr[...] = vn.astype(BF16)
        for g in range(G):
            cols = slice(g * gd, (g + 1) * gd)
            vnb = vn_scr[:, cols]
            mixed = jnp.dot(wm_ref[g], vnb, preferred_element_type=F32) + bt_ref[:, g:g + 1]
            u, du = _gelu_and_grad(zu_ref[:, cols])
            dy = dya_ref[:, cols]
            dz_ref[:, cols] = (dy * mixed * du).astype(BF16)
            dmix = dy * u
            dmb = dmix.astype(BF16)
            dbt_ref[:, g:g + 1] += jnp.sum(dmix, axis=1, keepdims=True)
            dwm_ref[g] += lax.dot_general(dmb, vnb, (((1,), (1,)), ((), ())), preferred_element_type=F32)
            dvn_scr[:, cols] = jnp.dot(wmt_ref[g], dmb, preferred_element_type=F32)
        dvn = dvn_scr[...]
        dlg_ref[...] += jnp.sum(dvn * vhat, axis=0, keepdims=True)
        dlb_ref[...] += jnp.sum(dvn, axis=0, keepdims=True)
        dvh = dvn * lgv
        dv = rstd * (dvh - jnp.mean(dvh, axis=-1, keepdims=True)
                     - vhat * jnp.mean(dvh * vhat, axis=-1, keepdims=True))
        _, dgv = _gelu_and_grad(zv_ref[...])
        dz_ref[:, W:] = (dv * dgv).astype(BF16)

        @pl.when(i == n_steps - 1)
        def _():
            row = lax.broadcasted_iota(jnp.int32, (C, C), 0)
            col = lax.broadcasted_iota(jnp.int32, (C, C), 1)
            for g in range(G):
                dwm_ref[g] = jnp.where(row >= col, dwm_ref[g], 0.0)

    vec = pl.BlockSpec((1, W), lambda i: (0, 0))
    mat = pl.BlockSpec((G, C, C), lambda i: (0, 0, 0))
    bts = pl.BlockSpec((C, G), lambda i: (0, 0))
    return pl.pallas_call(
        body, name=name, grid=(n_steps,),
        in_specs=[pl.BlockSpec((C, W), lambda i: (i, 0)), pl.BlockSpec((C, W), lambda i: (i, 1)),
                  pl.BlockSpec((C, W), lambda i: (i, 0)), vec, vec, mat, mat, bts, ANY],
        out_specs=[pl.BlockSpec((None, C, 2 * W), lambda i: (DPROJ_PIECES - 1, i, 0)), mat, bts, vec, vec],
        out_shape=[jax.ShapeDtypeStruct(dproj.shape, dproj.dtype), jax.ShapeDtypeStruct((G, C, C), F32),
                   jax.ShapeDtypeStruct((C, G), F32), jax.ShapeDtypeStruct((1, W), F32),
                   jax.ShapeDtypeStruct((1, W), F32)],
        scratch_shapes=[pltpu.VMEM((C, W), BF16), pltpu.VMEM((C, W), F32)], input_output_aliases={8: 0},
        compiler_params=_cp("arbitrary"),
    )(proj, proj, dya, lg, lb, wm, wmt, bt, dproj)


def _rows_with_prev(ref, r0, rows, ci):
    p0 = pl.multiple_of(jnp.maximum(r0 - SUBLANES, 0), SUBLANES)
    prev = jnp.where(ci > 0, ref[pl.ds(p0, SUBLANES), :], 0.0)
    return jnp.concatenate([prev, ref[pl.ds(r0, rows), :]], axis=0)


def _rows_with_next(ref, r0, rows, ci, n_chunks, total):
    n0 = pl.multiple_of(jnp.minimum(r0 + rows, total - SUBLANES), SUBLANES)
    nxt = jnp.where(ci < n_chunks - 1, ref[pl.ds(n0, SUBLANES), :], 0.0)
    return jnp.concatenate([ref[pl.ds(r0, rows), :], nxt], axis=0)


def _delayed(xx, k, rows):
    if k == 0:
        return xx[SUBLANES:, :]
    return pltpu.roll(xx, k, 0)[SUBLANES:, :]


def _advanced(xx, k, rows):
    if k == 0:
        return xx[:rows, :]
    return pltpu.roll(xx, rows + SUBLANES - k, 0)[:rows, :]


def _conv_chunk(x_ref, w_ref, b_ref, r0, rows, ci):
    K = w_ref.shape[0]
    xx = _rows_with_prev(x_ref, r0, rows, ci)
    out = _delayed(xx, K - 1, rows) * w_ref[0:1, :]
    for k in range(1, K):
        out = out + _delayed(xx, K - 1 - k, rows) * w_ref[k:k + 1, :]
    return out + b_ref[...]


def _ffn_act_fwd(up0, cw, cb, *, batch, name, carry=None):
    T, F2 = up0.shape
    F = F2 // 2
    S = T // batch
    K = cw.shape[0]
    cbk = _blk(F, 512)
    nj = F // cbk
    R = min(64, S // 2)
    n_chunks = S // R

    def body(ug_ref, uv_ref, wg_ref, wv_ref, bg_ref, bv_ref, act_ref):
        def chunk(ci, carry):
            r0 = pl.multiple_of(ci * R, R)
            cg = _conv_chunk(ug_ref, wg_ref, bg_ref, r0, R, ci)
            cv = _conv_chunk(uv_ref, wv_ref, bv_ref, r0, R, ci)
            act_ref[pl.ds(r0, R), :] = (_gelu(cg) * cv).astype(BF16)
            return carry

        lax.fori_loop(0, n_chunks, chunk, 0)

    outs, carried = _carried_call(
        body, name=name, grid=(nj, batch),
        in_specs=[pl.BlockSpec((S, cbk), lambda j, b: (b, j)), pl.BlockSpec((S, cbk), lambda j, b: (b, nj + j)),
                  pl.BlockSpec((K, cbk), lambda j, b: (0, j)), pl.BlockSpec((K, cbk), lambda j, b: (0, nj + j)),
                  pl.BlockSpec((1, cbk), lambda j, b: (0, j)), pl.BlockSpec((1, cbk), lambda j, b: (0, nj + j))],
        out_specs=[pl.BlockSpec((S, cbk), lambda j, b: (b, j))],
        out_shape=[jax.ShapeDtypeStruct((T, F), BF16)], scratch_shapes=[],
        args=[up0, up0, cw, cw, cb, cb], semantics=("parallel", "parallel"), carry=carry)
    return outs[0] if carry is None else (outs[0], carried)


def _ffn_act_bwd(up0, cw, cb, dact, *, batch, name):
    T, F2 = up0.shape
    F = F2 // 2
    S = T // batch
    K = cw.shape[0]
    cbk = _blk(F, 512)
    nj = F // cbk
    R = min(64, S // 2)
    n_chunks = S // R

    def body(ug_ref, uv_ref, wg_ref, wv_ref, bg_ref, bv_ref, da_ref,
             du_ref, dw_g_ref, dw_v_ref, db_g_ref, db_v_ref, dcg_scr, dcv_scr):
        b = pl.program_id(1)

        def chunk_a(ci, acc):
            r0 = pl.multiple_of(ci * R, R)
            xg = _rows_with_prev(ug_ref, r0, R, ci)
            xv = _rows_with_prev(uv_ref, r0, R, ci)
            dg_taps = [_delayed(xg, K - 1 - k, R) for k in range(K)]
            dv_taps = [_delayed(xv, K - 1 - k, R) for k in range(K)]
            cg = dg_taps[0] * wg_ref[0:1, :]
            cv = dv_taps[0] * wv_ref[0:1, :]
            for k in range(1, K):
                cg = cg + dg_taps[k] * wg_ref[k:k + 1, :]
                cv = cv + dv_taps[k] * wv_ref[k:k + 1, :]
            cg = cg + bg_ref[...]
            cv = cv + bv_ref[...]
            gl, dgl = _gelu_and_grad(cg)
            da = da_ref[pl.ds(r0, R), :]
            dcg = da * cv * dgl
            dcv = da * gl
            dcg_scr[pl.ds(r0, R), :] = dcg
            dcv_scr[pl.ds(r0, R), :] = dcv
            new = []
            for k in range(K):
                new.append(acc[k] + jnp.sum(dcg * dg_taps[k], axis=0, keepdims=True))
            for k in range(K):
                new.append(acc[K + k] + jnp.sum(dcv * dv_taps[k], axis=0, keepdims=True))
            new.append(acc[2 * K] + jnp.sum(dcg, axis=0, keepdims=True))
            new.append(acc[2 * K + 1] + jnp.sum(dcv, axis=0, keepdims=True))
            return tuple(new)

        zero = jnp.zeros((1, cbk), F32)
        acc = lax.fori_loop(0, n_chunks, chunk_a, (zero,) * (2 * K + 2))

        def chunk_b(ci, carry):
            r0 = pl.multiple_of(ci * R, R)
            dg = _rows_with_next(dcg_scr, r0, R, ci, n_chunks, S)
            dv = _rows_with_next(dcv_scr, r0, R, ci, n_chunks, S)
            og = _advanced(dg, 0, R) * wg_ref[K - 1:K, :]
            ov = _advanced(dv, 0, R) * wv_ref[K - 1:K, :]
            for j in range(1, K):
                og = og + _advanced(dg, j, R) * wg_ref[K - 1 - j:K - j, :]
                ov = ov + _advanced(dv, j, R) * wv_ref[K - 1 - j:K - j, :]
            du_ref[0, pl.ds(r0, R), :] = og.astype(BF16)
            du_ref[1, pl.ds(r0, R), :] = ov.astype(BF16)
            return carry

        lax.fori_loop(0, n_chunks, chunk_b, 0)

        @pl.when(b == 0)
        def _():
            for k in range(K):
                dw_g_ref[k:k + 1, :] = acc[k]
                dw_v_ref[k:k + 1, :] = acc[K + k]
            db_g_ref[...] = acc[2 * K]
            db_v_ref[...] = acc[2 * K + 1]

        @pl.when(b > 0)
        def _():
            for k in range(K):
                dw_g_ref[k:k + 1, :] += acc[k]
                dw_v_ref[k:k + 1, :] += acc[K + k]
            db_g_ref[...] += acc[2 * K]
            db_v_ref[...] += acc[2 * K + 1]

    seq = pl.BlockSpec((S, cbk), lambda j, b: (b, j))
    wk = pl.BlockSpec((K, cbk), lambda j, b: (0, j))
    w1 = pl.BlockSpec((1, cbk), lambda j, b: (0, j))
    outs = pl.pallas_call(
        body, name=name, grid=(nj, batch),
        in_specs=[seq, pl.BlockSpec((S, cbk), lambda j, b: (b, nj + j)),
                  wk, pl.BlockSpec((K, cbk), lambda j, b: (0, nj + j)),
                  w1, pl.BlockSpec((1, cbk), lambda j, b: (0, nj + j)), seq],
        out_specs=[pl.BlockSpec((2, S, cbk), lambda j, b: (0, b, j)), wk, wk, w1, w1],
        out_shape=[jax.ShapeDtypeStruct((2, T, F), BF16),
                   jax.ShapeDtypeStruct((K, F), F32), jax.ShapeDtypeStruct((K, F), F32),
                   jax.ShapeDtypeStruct((1, F), F32), jax.ShapeDtypeStruct((1, F), F32)],
        scratch_shapes=[pltpu.VMEM((S, cbk), F32), pltpu.VMEM((S, cbk), F32)],
        compiler_params=_cp("parallel", "arbitrary"),
    )(up0, up0, cw, cw, cb, cb, dact)
    du, dwg, dwv, dbg, dbv = outs
    return du, jnp.concatenate([dwg, dwv], axis=1), jnp.concatenate([dbg, dbv], axis=1)


def _lru_gate_rows(xr_ref, cw_ref, cb_ref, wa_ref, wx_ref, ba_ref, bx_ref, xc_scr, za_scr, zx_scr, S, R):
    def chunk(ci, carry):
        r0 = pl.multiple_of(ci * R, R)
        xc = _conv_chunk(xr_ref, cw_ref, cb_ref, r0, R, ci)
        xc_scr[pl.ds(r0, R), :] = xc
        xb = xc.astype(BF16)
        za_scr[pl.ds(r0, R), :] = jnp.dot(xb, wa_ref[...], preferred_element_type=F32) + ba_ref[...]
        zx_scr[pl.ds(r0, R), :] = jnp.dot(xb, wx_ref[...], preferred_element_type=F32) + bx_ref[...]
        return carry

    lax.fori_loop(0, S // R, chunk, 0)


def _lru_gates(za, zx, sp):
    ra = _sigmoid(za)
    ig = _sigmoid(zx)
    la = -LRU_C * ra * sp
    a = jnp.exp(la)
    s = jnp.sqrt(_neg_expm1(2.0 * la))
    return ra, ig, a, s


def _lru_fwd(proj, cw, cb, wa, wx, ba, bx, lam, *, batch, off_x, name, carry=None):
    T = proj.shape[0]
    H, Dh, _ = wa.shape
    W = H * Dh
    S = T // batch
    K = cw.shape[0]
    ox, oy = off_x // Dh, (off_x + W) // Dh
    R = min(256, S // 2)
    n16 = S // BF16_ROWS

    def body(xr_ref, yr_ref, cw_ref, cb_ref, wa_ref, wx_ref, ba_ref, bx_ref, lam_ref,
             yb_ref, h_ref, xc_scr, za_scr, zx_scr):
        _lru_gate_rows(xr_ref, cw_ref, cb_ref, wa_ref, wx_ref, ba_ref, bx_ref, xc_scr, za_scr, zx_scr, S, R)
        sp = _softplus(-lam_ref[...])
        row = lax.broadcasted_iota(jnp.int32, (SUBLANES, Dh), 0)

        def tile(r0, carry):
            rows = pl.ds(r0, SUBLANES)
            xc = xc_scr[rows, :]
            _, ig, a, s = _lru_gates(za_scr[rows, :], zx_scr[rows, :], sp)
            A, B = a, s * (ig * xc)
            for d in (1, 2, 4):
                m = row >= d
                Bs = pltpu.roll(B, d, 0)
                As = pltpu.roll(A, d, 0)
                B = jnp.where(m, B + A * Bs, B)
                A = jnp.where(m, A * As, A)
            hh = B + A * carry
            h_ref[rows, :] = hh
            return hh, hh[SUBLANES - 1:SUBLANES, :]

        def step(i, carry):
            r0 = pl.multiple_of(i * BF16_ROWS, BF16_ROWS)
            h0, carry = tile(r0, carry)
            h1, carry = tile(r0 + SUBLANES, carry)
            hh = jnp.concatenate([h0, h1], axis=0)
            yb_ref[pl.ds(r0, BF16_ROWS), :] = (hh * _gelu(yr_ref[pl.ds(r0, BF16_ROWS), :])).astype(BF16)
            return carry

        lax.fori_loop(0, n16, step, jnp.zeros((1, Dh), F32))

    vec = pl.BlockSpec((1, Dh), lambda b, h: (0, h))
    wsp = pl.BlockSpec((None, Dh, Dh), lambda b, h: (h, 0, 0))
    seq = pl.BlockSpec((S, Dh), lambda b, h: (b, h))
    outs, carried = _carried_call(
        body, name=name, grid=(batch, H),
        in_specs=[pl.BlockSpec((S, Dh), lambda b, h: (b, ox + h)), pl.BlockSpec((S, Dh), lambda b, h: (b, oy + h)),
                  pl.BlockSpec((K, Dh), lambda b, h: (0, h)), vec, wsp, wsp, vec, vec, vec],
        out_specs=[seq, seq],
        out_shape=[jax.ShapeDtypeStruct((T, W), BF16), jax.ShapeDtypeStruct((T, W), F32)],
        scratch_shapes=[pltpu.VMEM((S, Dh), F32)] * 3,
        args=[proj, proj, cw, cb, wa, wx, ba, bx, lam], semantics=("parallel", "parallel"), carry=carry)
    return (outs[0], outs[1]) if carry is None else (outs[0], outs[1], carried)


def _lru_bwd(proj, hseq, dyb, dproj, cw, cb, wa, wx, wat, wxt, ba, bx, lam, *, batch, off_x, name):
    T = proj.shape[0]
    H, Dh, _ = wa.shape
    W = H * Dh
    S = T // batch
    K = cw.shape[0]
    ox, oy = off_x // Dh, (off_x + W) // Dh
    R = min(256, S // 2)
    n_chunks = S // R
    n16 = S // BF16_ROWS

    def body(xr_ref, yr_ref, h_ref, dyb_ref, cw_ref, cb_ref, wa_ref, wx_ref, wat_ref, wxt_ref,
             ba_ref, bx_ref, lam_ref, dproj_in_ref,
             dxy_ref, dwa_ref, dwx_ref, dcw_ref, dcb_ref, dba_ref, dbx_ref, dlam_ref,
             xc_scr, za_scr, zx_scr, dza_scr, dzx_scr, dxc_scr):
        b = pl.program_id(1)
        _lru_gate_rows(xr_ref, cw_ref, cb_ref, wa_ref, wx_ref, ba_ref, bx_ref, xc_scr, za_scr, zx_scr, S, R)
        lam_v = lam_ref[...]
        sp = _softplus(-lam_v)
        row = lax.broadcasted_iota(jnp.int32, (SUBLANES, Dh), 0)

        def tile(r0, carry):
            a_next, g_next, s_ba, s_bx, s_lam = carry
            rows = pl.ds(r0, SUBLANES)
            xc = xc_scr[rows, :]
            ra, ig, a, s = _lru_gates(za_scr[rows, :], zx_scr[rows, :], sp)
            hh = h_ref[rows, :]
            gy, dgy = _gelu_and_grad(yr_ref[rows, :])
            dy = dyb_ref[rows, :]
            dyr = dy * hh * dgy
            C = jnp.where(row == SUBLANES - 1, a_next, pltpu.roll(a, SUBLANES - 1, 0))
            B = dy * gy
            for d in (1, 2, 4):
                m = row < SUBLANES - d
                Bs = pltpu.roll(B, SUBLANES - d, 0)
                Cs = pltpu.roll(C, SUBLANES - d, 0)
                B = jnp.where(m, B + C * Bs, B)
                C = jnp.where(m, C * Cs, C)
            G = B + C * g_next
            p0 = pl.multiple_of(jnp.maximum(r0 - SUBLANES, 0), SUBLANES)
            h_before = jnp.where(r0 > 0, h_ref[pl.ds(p0, SUBLANES), :][SUBLANES - 1:SUBLANES, :], 0.0)
            h_prev = jnp.where(row == 0, h_before, pltpu.roll(hh, 1, 0))
            da = G * h_prev
            dig = G * s * xc
            ds = G * ig * xc
            dxc_scr[rows, :] = G * s * ig
            dla = da * a - ds * (a * a) / s
            dza = dla * (-LRU_C * sp) * ra * (1.0 - ra)
            dzx = dig * ig * (1.0 - ig)
            dza_scr[rows, :] = dza
            dzx_scr[rows, :] = dzx
            carry = (a[0:1, :], G[0:1, :], s_ba + dza, s_bx + dzx, s_lam + dla * ra)
            return dyr, carry

        def step(it, carry):
            r0 = pl.multiple_of((n16 - 1 - it) * BF16_ROWS, BF16_ROWS)
            d1, carry = tile(r0 + SUBLANES, carry)
            d0, carry = tile(r0, carry)
            dxy_ref[1, pl.ds(r0, BF16_ROWS), :] = jnp.concatenate([d0, d1], axis=0).astype(BF16)
            return carry

        z1 = jnp.zeros((1, Dh), F32)
        z8 = jnp.zeros((SUBLANES, Dh), F32)
        _, _, s_ba, s_bx, s_lam = lax.fori_loop(0, n16, step, (z1, z1, z8, z8, z8))
        dba = jnp.sum(s_ba, axis=0, keepdims=True)
        dbx = jnp.sum(s_bx, axis=0, keepdims=True)
        dlam = jnp.sum(s_lam, axis=0, keepdims=True) * (LRU_C * _sigmoid(-lam_v))

        @pl.when(b == 0)
        def _():
            dwa_ref[...] = jnp.zeros_like(dwa_ref)
            dwx_ref[...] = jnp.zeros_like(dwx_ref)

        def chunk_c(ci, carry):
            r0 = pl.multiple_of(ci * R, R)
            rows = pl.ds(r0, R)
            xb = xc_scr[rows, :].astype(BF16)
            dzab = dza_scr[rows, :].astype(BF16)
            dzxb = dzx_scr[rows, :].astype(BF16)
            dwa_ref[...] += lax.dot_general(xb, dzab, (((0,), (0,)), ((), ())), preferred_element_type=F32)
            dwx_ref[...] += lax.dot_general(xb, dzxb, (((0,), (0,)), ((), ())), preferred_element_type=F32)
            dxc_scr[rows, :] += (jnp.dot(dzab, wat_ref[...], preferred_element_type=F32)
                                 + jnp.dot(dzxb, wxt_ref[...], preferred_element_type=F32))
            return carry

        lax.fori_loop(0, n_chunks, chunk_c, 0)

        def chunk_d(ci, acc):
            r0 = pl.multiple_of(ci * R, R)
            dd = _rows_with_next(dxc_scr, r0, R, ci, n_chunks, S)
            xx = _rows_with_prev(xr_ref, r0, R, ci)
            dxc = dd[:R, :]
            out = dxc * cw_ref[K - 1:K, :]
            for j in range(1, K):
                out = out + _advanced(dd, j, R) * cw_ref[K - 1 - j:K - j, :]
            dxy_ref[0, pl.ds(r0, R), :] = out.astype(BF16)
            new = [acc[k] + jnp.sum(dxc * _delayed(xx, K - 1 - k, R), axis=0, keepdims=True) for k in range(K)]
            new.append(acc[K] + jnp.sum(dxc, axis=0, keepdims=True))
            return tuple(new)

        acc = lax.fori_loop(0, n_chunks, chunk_d, (z1,) * (K + 1))

        @pl.when(b == 0)
        def _():
            for k in range(K):
                dcw_ref[k:k + 1, :] = acc[k]
            dcb_ref[...] = acc[K]
            dba_ref[...] = dba
            dbx_ref[...] = dbx
            dlam_ref[...] = dlam

        @pl.when(b > 0)
        def _():
            for k in range(K):
                dcw_ref[k:k + 1, :] += acc[k]
            dcb_ref[...] += acc[K]
            dba_ref[...] += dba
            dbx_ref[...] += dbx
            dlam_ref[...] += dlam

    vec = pl.BlockSpec((1, Dh), lambda h, b: (0, h))
    wsp = pl.BlockSpec((None, Dh, Dh), lambda h, b: (h, 0, 0))
    seq = pl.BlockSpec((S, Dh), lambda h, b: (b, h))
    ck = pl.BlockSpec((K, Dh), lambda h, b: (0, h))
    row_out = jax.ShapeDtypeStruct((1, W), F32)
    return pl.pallas_call(
        body, name=name, grid=(H, batch),
        in_specs=[pl.BlockSpec((S, Dh), lambda h, b: (b, ox + h)), pl.BlockSpec((S, Dh), lambda h, b: (b, oy + h)),
                  seq, seq, ck, vec, wsp, wsp, wsp, wsp, vec, vec, vec, ANY],
        out_specs=[pl.BlockSpec((2, S, Dh), lambda h, b: (0, b, h)), wsp, wsp, ck, vec, vec, vec, vec],
        out_shape=[jax.ShapeDtypeStruct(dproj.shape, dproj.dtype),
                   jax.ShapeDtypeStruct((H, Dh, Dh), F32), jax.ShapeDtypeStruct((H, Dh, Dh), F32),
                   jax.ShapeDtypeStruct((K, W), F32), row_out, row_out, row_out, row_out],
        scratch_shapes=[pltpu.VMEM((S, Dh), F32)] * 6, input_output_aliases={13: 0},
        compiler_params=_cp("parallel", "arbitrary"),
    )(proj, proj, hseq, dyb, cw, cb, wa, wx, wat, wxt, ba, bx, lam, dproj)


def _adamw(w, g, m, v, *, name, pass_grad=False):
    R, C = w.shape
    tr, tc = _blk(R, 256), _blk(C, 1024)

    def body(w_ref, g_ref, m_ref, v_ref, d_ref, nm_ref, nv_ref, *g_out):
        gv = g_ref[...]
        if pass_grad:
            g_out[0][...] = gv
        nm = ADAM_B1 * m_ref[...] + (1.0 - ADAM_B1) * gv
        nv = ADAM_B2 * v_ref[...] + (1.0 - ADAM_B2) * (gv * gv)
        m_hat = nm / (1.0 - ADAM_B1 ** ADAM_STEP)
        v_hat = nv / (1.0 - ADAM_B2 ** ADAM_STEP)
        d_ref[...] = -ADAM_LR * (m_hat / (jnp.sqrt(v_hat) + ADAM_EPS) + ADAM_WD * w_ref[...])
        nm_ref[...] = nm
        nv_ref[...] = nv

    blk = pl.BlockSpec((tr, tc), lambda i, j: (i, j))
    out = jax.ShapeDtypeStruct((R, C), F32)
    if not pass_grad:
        return pl.pallas_call(
            body, name=name, grid=(R // tr, C // tc), in_specs=[blk] * 4, out_specs=[blk] * 3,
            out_shape=[out] * 3, compiler_params=_cp("parallel", "parallel"),
        )(w, g, m, v)
    return pl.pallas_call(
        body, name=name, grid=(R // tr, C // tc), in_specs=[blk] * 4, out_specs=[blk] * 4,
        out_shape=[out] * 4, input_output_aliases={1: 3}, compiler_params=_cp("parallel", "parallel"),
    )(w, g, m, v)


def _to_slab(a, pos, dtype, *, name, b=None):
    R, C = a.shape
    tr, tc = _blk(R, 512), _blk(C, 1024)

    def body(p_ref, *refs):
        v = refs[0][...]
        if b is not None:
            v = v + refs[1][...]
        refs[-1][...] = v.astype(dtype)

    blk = pl.BlockSpec((tr, tc), lambda i, j, p_ref: (i, j))
    return pl.pallas_call(
        body, name=name,
        grid_spec=pltpu.PrefetchScalarGridSpec(
            num_scalar_prefetch=1, grid=(R // tr, C // tc), in_specs=[blk] * (1 if b is None else 2),
            out_specs=pl.BlockSpec((None, tr, tc), lambda i, j, p_ref: (p_ref[0], i, j))),
        out_shape=jax.ShapeDtypeStruct((N_CHIPS, R, C), dtype),
        compiler_params=_cp("parallel", "parallel"),
    )(pos, a, *([] if b is None else [b]))


def _sum_chips(q, *, name):
    _, R, C = q.shape
    tr = _blk(R, 1024)

    def body(q_ref, o_ref):
        o_ref[...] = ((q_ref[0] + q_ref[1]) + q_ref[2]) + q_ref[3]

    return pl.pallas_call(body, name=name, grid=(R // tr,),
                          in_specs=[pl.BlockSpec((N_CHIPS, tr, C), lambda i: (0, i, 0))],
                          out_specs=pl.BlockSpec((tr, C), lambda i: (i, 0)),
                          out_shape=jax.ShapeDtypeStruct((R, C), q.dtype), compiler_params=_cp("parallel"))(q)


def _pair_add_halves(g, rb, cpos, *, name):
    n, R, C = g.shape
    hr = R // 2
    tr, tc = _blk(hr, 512), _blk(C, 1024)
    nrb = hr // tr

    def body(c_ref, g_ref, r_ref, o_ref):
        o_ref[...] = (g_ref[...].astype(F32) + r_ref[...].astype(F32)).astype(o_ref.dtype)

    return pl.pallas_call(
        body, name=name,
        grid_spec=pltpu.PrefetchScalarGridSpec(
            num_scalar_prefetch=1, grid=(n, nrb, C // tc),
            in_specs=[pl.BlockSpec((None, tr, tc), lambda s, i, j, c_ref: (s, c_ref[0] * nrb + i, j)),
                      pl.BlockSpec((None, tr, tc), lambda s, i, j, c_ref: (s, i, j))],
            out_specs=pl.BlockSpec((None, tr, tc), lambda s, i, j, c_ref: (s, i, j))),
        out_shape=jax.ShapeDtypeStruct((n, hr, C), g.dtype),
        compiler_params=_cp("parallel", "parallel", "parallel"),
    )(cpos, g, rb)


def _chip_final_add(p, q, pos, *, name):
    _, hr, C = p.shape
    tr, tc = _blk(hr, 512), _blk(C, 1024)
    nrb = hr // tr

    def body(k_ref, p_ref, q_ref, o_ref):
        o_ref[...] = ((p_ref[...].astype(F32) + q_ref[0].astype(F32)) + q_ref[1].astype(F32)) + q_ref[2].astype(F32)

    return pl.pallas_call(
        body, name=name,
        grid_spec=pltpu.PrefetchScalarGridSpec(
            num_scalar_prefetch=1, grid=(nrb, C // tc),
            in_specs=[pl.BlockSpec((None, tr, tc), lambda i, j, k_ref: (k_ref[0], i, j)),
                      pl.BlockSpec((N_CHIPS - 1, tr, tc), lambda i, j, k_ref: (0, i, j))],
            out_specs=pl.BlockSpec((tr, tc), lambda i, j, k_ref: (k_ref[1] * nrb + i, j))),
        out_shape=jax.ShapeDtypeStruct((2 * hr, C), F32),
        compiler_params=_cp("parallel", "parallel"),
    )(pos, p, q)


def _pair_swap(v, *, name):
    def body(v_ref, o_ref, send_sem, recv_sem):
        x, y, c = _mesh_pos()
        cp = pltpu.make_async_remote_copy(src_ref=v_ref, dst_ref=o_ref, send_sem=send_sem, recv_sem=recv_sem,
                                          device_id=(x, y, 1 - c), device_id_type=MESH)
        cp.start()
        cp.wait()

    return pl.pallas_call(
        body, name=name, in_specs=[ANY], out_specs=ANY, out_shape=jax.ShapeDtypeStruct(v.shape, v.dtype),
        scratch_shapes=[pltpu.SemaphoreType.DMA, pltpu.SemaphoreType.DMA],
    )(v)


def _pair_sum(g, cpos, *, tag):
    rb, = _comm_call(_Carry(_PairSwap(g)), name=f"rs_pair_swap_{tag}")
    return _pair_add_halves(g, rb, cpos, name=f"rs_pair_add_{tag}")


def _all_reduce(v, pos, *, tag):
    other = _pair_swap(v, name=f"ar_pair_swap_{tag}")
    slabs = _to_slab(v, pos, F32, b=other, name=f"ar_pair_add_{tag}")
    slabs, = _comm_call(_Carry(_GatherSlabs(slabs)), name=f"ar_allgather_{tag}")
    return _sum_chips(slabs, name=f"ar_sum_{tag}")


def _pack(arrays, unit, total_unit=None):
    parts, n = [], 0
    for a in arrays:
        flat = a.reshape(-1)
        pad = (-flat.shape[0]) % unit
        parts.append(jnp.pad(flat, (0, pad)) if pad else flat)
        n += flat.shape[0] + pad
    if total_unit and n % total_unit:
        parts.append(jnp.zeros((-n) % total_unit, arrays[0].dtype))
    return jnp.concatenate(parts).reshape(-1, LANES)


def _unpack(packed, shapes, unit):
    lead = packed.shape[:-2]
    flat = packed.reshape(lead + (-1,))
    out, pos = [], 0
    for shp in shapes:
        n = math.prod(shp)
        out.append(flat[..., pos:pos + n].reshape(lead + tuple(shp)))
        pos += n + (-n) % unit
    return out


def kernel(x, g_mix, w_in, sg_ln_g, sg_ln_b, sg_w, sg_b, lru_conv_w, lru_conv_b, lru_wa, lru_ba, lru_wx, lru_bx, lru_lam, p_sg, p_lru, w_out, g_ffn, w_up, ffn_conv_w, ffn_conv_b, w_down, g_final, loss_target, m_g_mix, m_w_in, m_sg_ln_g, m_sg_ln_b, m_sg_w, m_sg_b, m_lru_conv_w, m_lru_conv_b, m_lru_wa, m_lru_ba, m_lru_wx, m_lru_bx, m_lru_lam, m_p_sg, m_p_lru, m_w_out, m_g_ffn, m_w_up, m_ffn_conv_w, m_ffn_conv_b, m_w_down, m_g_final, v_g_mix, v_w_in, v_sg_ln_g, v_sg_ln_b, v_sg_w, v_sg_b, v_lru_conv_w, v_lru_conv_b, v_lru_wa, v_lru_ba, v_lru_wx, v_lru_bx, v_lru_lam, v_p_sg, v_p_lru, v_w_out, v_g_ffn, v_w_up, v_ffn_conv_w, v_ffn_conv_b, v_w_down, v_g_final):
    params = dict(g_mix=g_mix, w_in=w_in, sg_ln_g=sg_ln_g, sg_ln_b=sg_ln_b, sg_w=sg_w, sg_b=sg_b,
                  lru_conv_w=lru_conv_w, lru_conv_b=lru_conv_b, lru_wa=lru_wa, lru_ba=lru_ba, lru_wx=lru_wx,
                  lru_bx=lru_bx, lru_lam=lru_lam, p_sg=p_sg, p_lru=p_lru, w_out=w_out, g_ffn=g_ffn, w_up=w_up,
                  ffn_conv_w=ffn_conv_w, ffn_conv_b=ffn_conv_b, w_down=w_down, g_final=g_final)
    mom1 = dict(g_mix=m_g_mix, w_in=m_w_in, sg_ln_g=m_sg_ln_g, sg_ln_b=m_sg_ln_b, sg_w=m_sg_w, sg_b=m_sg_b,
                lru_conv_w=m_lru_conv_w, lru_conv_b=m_lru_conv_b, lru_wa=m_lru_wa, lru_ba=m_lru_ba,
                lru_wx=m_lru_wx, lru_bx=m_lru_bx, lru_lam=m_lru_lam, p_sg=m_p_sg, p_lru=m_p_lru, w_out=m_w_out,
                g_ffn=m_g_ffn, w_up=m_w_up, ffn_conv_w=m_ffn_conv_w, ffn_conv_b=m_ffn_conv_b, w_down=m_w_down,
                g_final=m_g_final)
    mom2 = dict(g_mix=v_g_mix, w_in=v_w_in, sg_ln_g=v_sg_ln_g, sg_ln_b=v_sg_ln_b, sg_w=v_sg_w, sg_b=v_sg_b,
                lru_conv_w=v_lru_conv_w, lru_conv_b=v_lru_conv_b, lru_wa=v_lru_wa, lru_ba=v_lru_ba,
                lru_wx=v_lru_wx, lru_bx=v_lru_bx, lru_lam=v_lru_lam, p_sg=v_p_sg, p_lru=v_p_lru, w_out=v_w_out,
                g_ffn=v_g_ffn, w_up=v_w_up, ffn_conv_w=v_ffn_conv_w, ffn_conv_b=v_ffn_conv_b, w_down=v_w_down,
                g_final=v_g_final)
    names = list(params)
    big = ["w_in", "p_sg", "p_lru", "w_out", "w_up", "w_down"]
    col_sharded = {"w_in", "p_sg", "w_up"}
    small = [n for n in names if n not in big]

    batch, S, D = x.shape
    T = batch * S
    W_sg = sg_ln_g.shape[-1]
    H, _, Dh = lru_wa.shape[1:]
    W_lru = H * Dh
    K_lru = lru_conv_w.shape[1]
    K_ffn = ffn_conv_w.shape[1]
    F2 = ffn_conv_b.shape[-1]
    off_lru = 2 * W_sg
    off_gate = off_lru + 2 * W_lru

    cx, cy, cc = _mesh_pos()
    chip = 2 * cx + cy
    cpos = jnp.reshape(cc, (1,)).astype(jnp.int32)
    pos = jnp.stack([chip, cc]).astype(jnp.int32)

    xf = x.reshape(T, D)
    tgt = loss_target.reshape(T, D)

    wb = {n: _to_slab(params[n][0], pos, BF16, name=f"cast_{n}") for n in big}
    rows = {n: wb[n].shape[1] for n in big}
    sharded_small = ["lru_conv_w", "ffn_conv_w", "lru_wa", "lru_wx"]
    unit_g = 2 * BF16_ROWS * LANES
    pack_g = 256 * LANES
    sm_shapes = [params[n][0].shape for n in sharded_small]
    sm = _to_slab(_pack([params[n][0] for n in sharded_small], unit_g, pack_g), pos, F32, name="slab_small")

    parts = 32

    def ici(n, lo=0, hi=parts):
        return _GatherIci(wb[n], rows[n] * lo // parts, rows[n] * hi // parts)

    def fwd(n, lo=0, hi=parts):
        return _GatherFwd(wb[n], rows[n] * lo // parts, rows[n] * hi // parts)

    w_in_g, sm = _comm_call(_Carry(_GatherRows(wb["w_in"], 0, rows["w_in"]), _GatherRows(sm, 0, sm.shape[1])),
                            name="gather_first")
    cwl_s, cwf_s, wa_s, wx_s = _unpack(sm, sm_shapes, unit_g)
    lru_cw = jnp.transpose(cwl_s, (1, 0, 2)).reshape(K_lru, W_lru)
    ffn_cw = jnp.transpose(cwf_s, (1, 0, 2)).reshape(K_ffn, F2)
    wa_full = jnp.transpose(wa_s, (1, 0, 2, 3)).reshape(H, Dh, Dh)
    wx_full = jnp.transpose(wx_s, (1, 0, 2, 3)).reshape(H, Dh, Dh)
    wa_b, wx_b = wa_full.astype(BF16), wx_full.astype(BF16)
    wat_b, wxt_b = jnp.swapaxes(wa_b, 1, 2), jnp.swapaxes(wx_b, 1, 2)

    wm, wmt = _sg_mask(sg_w[0], name="sg_mask")
    bt = sg_b[0].T

    h1 = _rms_fwd(xf, g_mix, name="rms1_fwd")
    proj, (wb["p_sg"], wb["p_lru"], wb["w_out"], wb["w_up"]) = _mm_nn(
        h1, w_in_g, out_dtype=F32, name="mm_proj",
        carry=_Carry(ici("p_sg"), ici("p_lru"), ici("w_out"), ici("w_up", 0, 8)))
    y_a, (wb["w_up"],) = _sg_fwd(proj, sg_ln_g, sg_ln_b, wm, bt, name="sg_fwd", carry=_Carry(ici("w_up", 8, 9)))
    y_b, hseq, (p_sg_g, p_lru_g, w_out_g, wb["w_up"]) = _lru_fwd(
        proj, lru_cw, lru_conv_b, wa_b, wx_b, lru_ba, lru_bx, lru_lam, batch=batch, off_x=off_lru, name="lru_fwd",
        carry=_Carry(fwd("p_sg"), fwd("p_lru"), fwd("w_out"), fwd("w_up", 0, 8), fwd("w_up", 8, 9),
                     ici("w_up", 9, 18)))
    p_lru_g = p_lru_g.reshape(-1, D)
    w_out_g = w_out_g.reshape(-1, D)
    pa, (wb["w_up"],) = _mm_nn(y_a, p_sg_g, out_dtype=F32, name="mm_pa",
                               carry=_Carry(fwd("w_up", 9, 18), ici("w_up", 18, 20)))
    pb, (wb["w_up"],) = _mm_nn(y_b, p_lru_g, out_dtype=F32, name="mm_pb",
                               carry=_Carry(fwd("w_up", 18, 20), ici("w_up", 20, 25)))
    merged, (wb["w_up"],) = _merge_fwd(proj, pa, pb, off_a=off_gate, name="merge_fwd",
                                       carry=_Carry(fwd("w_up", 20, 25), ici("w_up", 25, 28)))
    x1, (wb["w_up"],) = _mm_nn(merged, w_out_g, out_dtype=F32, res=xf, name="mm_out",
                               carry=_Carry(fwd("w_up", 25, 28), ici("w_up", 28, 32)))
    h2, (w_up_g,) = _rms_fwd(x1, g_ffn, name="rms2_fwd", carry=_Carry(fwd("w_up", 28, 32)))
    up0, (wb["w_down"],) = _mm_nn(h2, w_up_g, out_dtype=F32, name="mm_up", carry=_Carry(ici("w_down")))
    act, (w_down_g,) = _ffn_act_fwd(up0, ffn_cw, ffn_conv_b, batch=batch, name="ffn_act_fwd",
                                    carry=_Carry(fwd("w_down")))
    w_down_g = w_down_g.reshape(-1, D)
    x2 = _mm_nn(act, w_down_g, out_dtype=F32, res=x1, name="mm_down")
    lvec, dx2, dx2_b, dg_final = _loss_head(x2, tgt, g_final.reshape(1, D), name="loss_head")
    loss = lax.psum(jnp.sum(lvec) * (0.5 / D), ("x", "y", "c"))

    assert 2 * W_sg == W_lru == D, "d proj travels in equal column pieces"
    ps, qs = {}, {}

    def pair_add(g, rb, tag):
        return _pair_add_halves(g, rb, cpos, name=f"rs_pair_add_{tag}")

    g = _mm_tn(act, dx2_b, out_dtype=BF16, name="mm_dw_down").reshape(N_CHIPS, -1, D)
    dact, (rb,) = _mm_nt(dx2_b, w_down_g, out_dtype=F32, name="mm_dact", carry=_Carry(_PairSwap(g)))
    ps["w_down"] = pair_add(g, rb, "w_down")
    dup0, d_ffn_cw, d_ffn_cb = _ffn_act_bwd(up0, ffn_cw, ffn_conv_b, dact, batch=batch, name="ffn_act_bwd")
    g, (qs["w_down"],) = _mm_tn(h2, dup0, out_dtype=BF16, col_shards=N_CHIPS, name="mm_dw_up",
                                carry=_Carry(_ChipExchange(ps["w_down"])))
    ps["w_up"] = _pair_sum(g, cpos, tag="w_up")
    dh2, (qs["w_up"],) = _mm_nt(dup0, w_up_g, out_dtype=F32, name="mm_dh2", carry=_Carry(_ChipExchange(ps["w_up"])))
    dx1, dx1_b, dg_ffn = _rms_bwd(x1, g_ffn, dh2, dx2, name="rms2_bwd")
    g = _mm_tn(merged, dx1_b, out_dtype=BF16, name="mm_dw_out").reshape(N_CHIPS, -1, D)
    dmerged, (rb,) = _mm_nt(dx1_b, w_out_g, out_dtype=F32, name="mm_dmerged", carry=_Carry(_PairSwap(g)))
    ps["w_out"] = pair_add(g, rb, "w_out")
    dproj, dpa, dpb = _merge_bwd(proj, pa, pb, dmerged, off_a=off_gate, name="merge_bwd")
    g_sg = _mm_tn(y_a, dpa, out_dtype=BF16, col_shards=N_CHIPS, name="mm_dp_sg")
    g_lru = _mm_tn(y_b, dpb, out_dtype=BF16, name="mm_dp_lru").reshape(N_CHIPS, -1, D)
    dya, (rb,) = _mm_nt(dpa, p_sg_g, out_dtype=F32, name="mm_dya", carry=_Carry(_PairSwap(g_sg)))
    ps["p_sg"] = pair_add(g_sg, rb, "p_sg")
    dyb, (rb,) = _mm_nt(dpb, p_lru_g, out_dtype=F32, name="mm_dyb", carry=_Carry(_PairSwap(g_lru)))
    ps["p_lru"] = pair_add(g_lru, rb, "p_lru")
    dproj, d_wa, d_wx, d_lru_cw, d_lru_cb, d_ba, d_bx, d_lam = _lru_bwd(
        proj, hseq, dyb, dproj, lru_cw, lru_conv_b, wa_b, wx_b, wat_b, wxt_b, lru_ba, lru_bx, lru_lam,
        batch=batch, off_x=off_lru, name="lru_bwd")
    g = jnp.stack([d_wa, d_wx]).reshape(2, H, N_CHIPS, Dh // N_CHIPS, Dh)
    g = jnp.transpose(g, (2, 0, 1, 3, 4)).reshape(N_CHIPS, -1, Dh).astype(BF16)
    ps["gates"] = _pair_sum(g, cpos, tag="gates")
    dproj, d_wm, d_bt, d_lg, d_lb = _sg_bwd(proj, dya, dproj, sg_ln_g, sg_ln_b, wm, wmt, bt, name="sg_bwd")
    late = ["w_out", "p_sg", "p_lru", "gates"]
    g, carried = _mm_tn(h1, dproj, out_dtype=BF16, col_shards=N_CHIPS, name="mm_dw_in", b_shift=DPROJ_SHIFT,
                        carry=_Carry(*[_ChipExchange(ps[n]) for n in late]))
    qs.update(zip(late, carried))
    ps["w_in"] = _pair_sum(g, cpos, tag="w_in")
    dh1, (qs["w_in"],) = _mm_nt(dproj, w_in_g, out_dtype=F32, name="mm_dh1", a_shift=DPROJ_SHIFT,
                                carry=_Carry(_ChipExchange(ps["w_in"])))
    dx, _, dg_mix = _rms_bwd(xf, g_mix, dh1, dx1, name="rms1_bwd")

    scattered = big + ["gates"]
    halves = [_chip_final_add(ps[n], qs[n], pos, name=f"rs_final_add_{n}") for n in scattered]
    grads = dict(zip(scattered, _comm_call(_Carry(*[_ShareHalves(h) for h in halves]), name="rs_share")))
    d_gates = grads.pop("gates").reshape(2, H, Dh // N_CHIPS, Dh)
    grads["lru_wa"], grads["lru_wx"] = d_gates[0].reshape(lru_wa.shape), d_gates[1].reshape(lru_wx.shape)
    small_full = dict(g_mix=dg_mix, sg_ln_g=d_lg, sg_ln_b=d_lb, sg_w=d_wm, sg_b=d_bt.T, lru_conv_w=d_lru_cw,
                      lru_conv_b=d_lru_cb, lru_ba=d_ba, lru_bx=d_bx, lru_lam=d_lam,
                      g_ffn=dg_ffn, ffn_conv_w=d_ffn_cw, ffn_conv_b=d_ffn_cb, g_final=dg_final)
    reduced = list(small_full)
    unit_s = SUBLANES * LANES
    pack_s = 512 * LANES
    red = _all_reduce(_pack([small_full[n] for n in reduced], unit_s, pack_s), pos, tag="small")
    red = dict(zip(reduced, _unpack(red, [small_full[n].shape for n in reduced], unit_s)))
    cs_lru = W_lru // N_CHIPS
    cs_ffn = F2 // N_CHIPS
    red["lru_conv_w"] = lax.dynamic_slice_in_dim(red["lru_conv_w"], chip * cs_lru, cs_lru, axis=1)
    red["ffn_conv_w"] = lax.dynamic_slice_in_dim(red["ffn_conv_w"], chip * cs_ffn, cs_ffn, axis=1)
    for n in reduced:
        grads[n] = red[n].reshape(params[n].shape)

    delta, new_m, new_v = {}, {}, {}
    for n in big:
        shp = params[n].shape
        two_d = (-1, shp[-1])
        d, nm, nv, gr = _adamw(params[n].reshape(two_d), grads[n], mom1[n].reshape(two_d),
                               mom2[n].reshape(two_d), name=f"adamw_{n}", pass_grad=True)
        delta[n], new_m[n], new_v[n], grads[n] = d.reshape(shp), nm.reshape(shp), nv.reshape(shp), gr.reshape(shp)
    packs = [_pack([src[n] for n in small], unit_s, pack_s) for src in (params, grads, mom1, mom2)]
    outs = _adamw(*packs, name="adamw_small")
    shapes = [params[n].shape for n in small]
    for dst, packed in zip((delta, new_m, new_v), outs):
        dst.update(dict(zip(small, _unpack(packed, shapes, unit_s))))

    return (loss, dx.reshape(x.shape), *[grads[n] for n in names], *[delta[n] for n in names],
            *[new_m[n] for n in names], *[new_v[n] for n in names])
```

```python
import math

import jax
import jax.numpy as jnp
from jax import lax
from jax.experimental import pallas as pl
from jax.experimental.pallas import tpu as pltpu

F32 = jnp.float32
BF16 = jnp.bfloat16
MESH = pl.DeviceIdType.MESH
ANY = pl.BlockSpec(memory_space=pl.ANY)

EPS = 1e-6
LRU_C = 8.0
ADAM_LR = 0.001
ADAM_B1 = 0.9
ADAM_B2 = 0.999
ADAM_EPS = 1e-08
ADAM_WD = 0.01
ADAM_STEP = 10

N_CHIPS = 4
SUBLANES = 8
BF16_ROWS = 16
LANES = 128
VMEM_LIMIT = 56 * 1024 * 1024
DPROJ_PIECES = 5
DPROJ_SHIFT = 4
GELU_C = math.sqrt(2.0 / math.pi)
GELU_K = 0.044715


def _cp(*sem):
    return pltpu.CompilerParams(dimension_semantics=sem, vmem_limit_bytes=VMEM_LIMIT)


def _blk(dim, pref):
    if dim <= pref:
        return dim
    b = pref
    while dim % b:
        b //= 2
    return b


def _gelu(x):
    t = jnp.tanh(GELU_C * (x + GELU_K * x * x * x))
    return 0.5 * x * (1.0 + t)


def _gelu_and_grad(x):
    x2 = x * x
    t = jnp.tanh(GELU_C * (x + GELU_K * x * x2))
    g = 0.5 * x * (1.0 + t)
    dg = 0.5 * (1.0 + t) + 0.5 * x * (1.0 - t * t) * (GELU_C * (1.0 + 3.0 * GELU_K * x2))
    return g, dg


def _sigmoid(x):
    return 1.0 / (1.0 + jnp.exp(-x))


def _softplus(x):
    e = jnp.exp(-jnp.abs(x))
    series = e * (1.0 - e * (0.5 - e * (1.0 / 3.0 - e * (0.25 - e * 0.2))))
    return jnp.where(e < 0.01, series, jnp.log(1.0 + e)) + jnp.maximum(x, 0.0)


def _neg_expm1(x):
    series = -(x * (1.0 + x * (0.5 + x * (1.0 / 6.0 + x * (1.0 / 24.0)))))
    return jnp.where(x > -0.01, series, 1.0 - jnp.exp(x))


def _mesh_pos():
    return lax.axis_index("x"), lax.axis_index("y"), lax.axis_index("c")


def _other_chips(x, y):
    return [(1 - x, y), (x, 1 - y), (1 - x, 1 - y)]


def _remote(k, src, dst, to, send_sems, recv_sems):
    return pltpu.make_async_remote_copy(src_ref=src, dst_ref=dst, send_sem=send_sems.at[k],
                                        recv_sem=recv_sems.at[k], device_id=to, device_id_type=MESH)


class _GatherRows:
    n_sems = 6

    def __init__(self, buf, r0, r1):
        self.args = [buf]
        self.out_shape = [jax.ShapeDtypeStruct(buf.shape, buf.dtype)]
        self.aliases = {0: 0}
        self.r0, self.h = r0, (r1 - r0) // 2

    def _rows(self, half):
        return pl.ds(self.r0 + half * self.h, self.h)

    def start(self, ins, outs, ss, rs, base):
        x, y, c = _mesh_pos()
        mine = ins[0].at[2 * x + y, self._rows(c), :]
        for j, (px, py) in enumerate(_other_chips(x, y)):
            _remote(base + j, mine, outs[0].at[2 * x + y, self._rows(c), :], (px, py, c), ss, rs).start()

    def finish(self, ins, outs, ss, rs, base):
        x, y, c = _mesh_pos()
        sibling = (x, y, 1 - c)
        chips = _other_chips(x, y)
        mine = ins[0].at[2 * x + y, self._rows(c), :]
        for j, (px, py) in enumerate(chips):
            got = outs[0].at[2 * px + py, self._rows(c), :]
            _remote(base + j, got, got, (px, py, c), ss, rs).wait_recv()
            _remote(base + 3 + j, got, got, sibling, ss, rs).start()
        for j, (px, py) in enumerate(chips):
            fwd = outs[0].at[2 * px + py, self._rows(1 - c), :]
            _remote(base + 3 + j, fwd, fwd, sibling, ss, rs).wait_recv()
        for j, (px, py) in enumerate(chips):
            got = outs[0].at[2 * px + py, self._rows(c), :]
            _remote(base + j, mine, mine, (px, py, c), ss, rs).wait_send()
            _remote(base + 3 + j, got, got, sibling, ss, rs).wait_send()


class _GatherIci(_GatherRows):
    n_sems = 3

    def finish(self, ins, outs, ss, rs, base):
        x, y, c = _mesh_pos()
        mine = ins[0].at[2 * x + y, self._rows(c), :]
        for j, (px, py) in enumerate(_other_chips(x, y)):
            got = outs[0].at[2 * px + py, self._rows(c), :]
            _remote(base + j, got, got, (px, py, c), ss, rs).wait_recv()
            _remote(base + j, mine, mine, (px, py, c), ss, rs).wait_send()


class _GatherFwd(_GatherRows):
    n_sems = 3

    def start(self, ins, outs, ss, rs, base):
        x, y, c = _mesh_pos()
        for j, (px, py) in enumerate(_other_chips(x, y)):
            _remote(base + j, ins[0].at[2 * px + py, self._rows(c), :], outs[0].at[2 * px + py, self._rows(c), :],
                    (x, y, 1 - c), ss, rs).start()

    def finish(self, ins, outs, ss, rs, base):
        x, y, c = _mesh_pos()
        for j, (px, py) in enumerate(_other_chips(x, y)):
            got = ins[0].at[2 * px + py, self._rows(c), :]
            fwd = outs[0].at[2 * px + py, self._rows(1 - c), :]
            _remote(base + j, got, got, (x, y, 1 - c), ss, rs).wait_send()
            _remote(base + j, fwd, fwd, (x, y, 1 - c), ss, rs).wait_recv()


class _ChipExchange:
    n_sems = 3

    def __init__(self, p, q=None, lo=0, hi=1, of=1):
        hr = p.shape[1]
        self.r0, self.n = hr * lo // of, hr * (hi - lo) // of
        self.args = [p] if q is None else [p, q]
        self.out_shape = [jax.ShapeDtypeStruct((N_CHIPS - 1,) + p.shape[1:], p.dtype)]
        self.aliases = {} if q is None else {1: 0}

    def _copies(self, ins, outs, ss, rs, base):
        x, y, c = _mesh_pos()
        rows = pl.ds(self.r0, self.n)
        return [_remote(base + j, ins[0].at[2 * px + py, rows, :], outs[0].at[j, rows, :], (px, py, c), ss, rs)
                for j, (px, py) in enumerate(_other_chips(x, y))]

    def start(self, ins, outs, ss, rs, base):
        for cp in self._copies(ins, outs, ss, rs, base):
            cp.start()

    def finish(self, ins, outs, ss, rs, base):
        for cp in self._copies(ins, outs, ss, rs, base):
            cp.wait()


class _PairSwap:
    n_sems = 1

    def __init__(self, g):
        n, R, C = g.shape
        self.args = [g]
        self.out_shape = [jax.ShapeDtypeStruct((n, R // 2, C), g.dtype)]
        self.aliases = {}
        self.hr = R // 2

    def _copy(self, ins, outs, ss, rs, base):
        x, y, c = _mesh_pos()
        return _remote(base, ins[0].at[:, pl.ds((1 - c) * self.hr, self.hr), :], outs[0], (x, y, 1 - c), ss, rs)

    def start(self, ins, outs, ss, rs, base):
        self._copy(ins, outs, ss, rs, base).start()

    def finish(self, ins, outs, ss, rs, base):
        self._copy(ins, outs, ss, rs, base).wait()


class _ShareHalves:
    n_sems = 1

    def __init__(self, buf):
        self.args = [buf]
        self.out_shape = [jax.ShapeDtypeStruct(buf.shape, buf.dtype)]
        self.aliases = {0: 0}
        self.hr = buf.shape[0] // 2

    def start(self, ins, outs, ss, rs, base):
        x, y, c = _mesh_pos()
        rows = pl.ds(c * self.hr, self.hr)
        _remote(base, ins[0].at[rows, :], outs[0].at[rows, :], (x, y, 1 - c), ss, rs).start()

    def finish(self, ins, outs, ss, rs, base):
        x, y, c = _mesh_pos()
        mine = ins[0].at[pl.ds(c * self.hr, self.hr), :]
        theirs = outs[0].at[pl.ds((1 - c) * self.hr, self.hr), :]
        _remote(base, mine, mine, (x, y, 1 - c), ss, rs).wait_send()
        _remote(base, theirs, theirs, (x, y, 1 - c), ss, rs).wait_recv()


class _GatherSlabs:
    n_sems = 3

    def __init__(self, buf):
        self.args = [buf]
        self.out_shape = [jax.ShapeDtypeStruct(buf.shape, buf.dtype)]
        self.aliases = {0: 0}

    def start(self, ins, outs, ss, rs, base):
        x, y, c = _mesh_pos()
        for j, (px, py) in enumerate(_other_chips(x, y)):
            _remote(base + j, ins[0].at[2 * x + y], outs[0].at[2 * x + y], (px, py, c), ss, rs).start()

    def finish(self, ins, outs, ss, rs, base):
        x, y, c = _mesh_pos()
        mine = ins[0].at[2 * x + y]
        for j, (px, py) in enumerate(_other_chips(x, y)):
            got = outs[0].at[2 * px + py]
            _remote(base + j, mine, mine, (px, py, c), ss, rs).wait_send()
            _remote(base + j, got, got, (px, py, c), ss, rs).wait_recv()


class _Carry:
    def __init__(self, *items):
        self.items = items
        self.args, self.out_shape, self._alias, self._slots = [], [], {}, []
        seen = {}
        for it in items:
            in_idx, out_idx = [], [None] * len(it.out_shape)
            for ai, a in enumerate(it.args):
                aliased = ai in it.aliases
                if aliased and id(a) in seen:
                    i, o = seen[id(a)]
                else:
                    i, o = len(self.args), None
                    self.args.append(a)
                    if aliased:
                        o = len(self.out_shape)
                        self.out_shape.append(it.out_shape[it.aliases[ai]])
                        self._alias[i] = o
                        seen[id(a)] = (i, o)
                in_idx.append(i)
                if aliased:
                    out_idx[it.aliases[ai]] = o
            for oi, shape in enumerate(it.out_shape):
                if out_idx[oi] is None:
                    out_idx[oi] = len(self.out_shape)
                    self.out_shape.append(shape)
            self._slots.append((in_idx, out_idx))
        self.n_sems = sum(it.n_sems for it in items)

    def aliases(self, in_base, out_base):
        return {in_base + i: out_base + o for i, o in self._alias.items()}

    def _each(self, method, ins, outs, ss, rs):
        base = 0
        for it, (in_idx, out_idx) in zip(self.items, self._slots):
            getattr(it, method)([ins[i] for i in in_idx], [outs[o] for o in out_idx], ss, rs, base)
            base += it.n_sems

    def start(self, ins, outs, ss, rs):
        self._each("start", ins, outs, ss, rs)

    def finish(self, ins, outs, ss, rs):
        self._each("finish", ins, outs, ss, rs)


def _carried_call(body, *, name, grid, in_specs, out_specs, out_shape, scratch_shapes, args, semantics, carry=None):
    if carry is None:
        outs = pl.pallas_call(body, name=name, grid=grid, in_specs=in_specs, out_specs=out_specs,
                              out_shape=out_shape, scratch_shapes=scratch_shapes,
                              compiler_params=_cp(*semantics))(*args)
        return list(outs), []
    n_in, n_out, n_scr = len(in_specs), len(out_specs), len(scratch_shapes)
    n_cin, n_cout = len(carry.args), len(carry.out_shape)

    def full(*refs):
        ins = refs[:n_in]
        cins = refs[n_in:n_in + n_cin]
        outs = refs[n_in + n_cin:n_in + n_cin + n_out]
        couts = refs[n_in + n_cin + n_out:n_in + n_cin + n_out + n_cout]
        scr = refs[n_in + n_cin + n_out + n_cout:n_in + n_cin + n_out + n_cout + n_scr]
        ss, rs = refs[-2], refs[-1]
        first = pl.program_id(0) == 0
        last = pl.program_id(0) == grid[0] - 1
        for d in range(1, len(grid)):
            first = jnp.logical_and(first, pl.program_id(d) == 0)
            last = jnp.logical_and(last, pl.program_id(d) == grid[d] - 1)

        @pl.when(first)
        def _():
            carry.start(cins, couts, ss, rs)

        body(*ins, *outs, *scr)

        @pl.when(last)
        def _():
            carry.finish(cins, couts, ss, rs)

    outs = pl.pallas_call(
        full, name=name, grid=grid, in_specs=list(in_specs) + [ANY] * n_cin,
        out_specs=list(out_specs) + [ANY] * n_cout, out_shape=list(out_shape) + carry.out_shape,
        scratch_shapes=list(scratch_shapes) + [pltpu.SemaphoreType.DMA((carry.n_sems,))] * 2,
        input_output_aliases=carry.aliases(n_in, n_out),
        compiler_params=_cp(*(("arbitrary",) * len(grid))),
    )(*args, *carry.args)
    return list(outs[:n_out]), list(outs[n_out:])


def _comm_call(carry, *, name):
    n_cin = len(carry.args)

    def body(*refs):
        cins, couts = refs[:n_cin], refs[n_cin:-2]
        carry.start(cins, couts, refs[-2], refs[-1])
        carry.finish(cins, couts, refs[-2], refs[-1])

    outs = pl.pallas_call(
        body, name=name, in_specs=[ANY] * n_cin, out_specs=[ANY] * len(carry.out_shape), out_shape=carry.out_shape,
        scratch_shapes=[pltpu.SemaphoreType.DMA((carry.n_sems,))] * 2,
        input_output_aliases=carry.aliases(0, 0),
    )(*carry.args)
    return list(outs)


def _mm_nn(a, b, *, out_dtype, name, res=None, carry=None):
    M, K = a.shape
    cs = b.shape[-1]
    N = cs * (b.shape[0] if b.ndim == 3 else 1)
    tm, tn, tk = _blk(M, 1024 if res is None else 512), _blk(cs, 1024), _blk(K, 4096)
    nbs, nk = cs // tn, K // tk
    if b.ndim == 3:
        b_spec = pl.BlockSpec((None, tk, tn), lambda i, j, k: (j // nbs, k, j % nbs))
    else:
        b_spec = pl.BlockSpec((tk, tn), lambda i, j, k: (k, j))
    in_specs = [pl.BlockSpec((tm, tk), lambda i, j, k: (i, k)), b_spec]
    args = [a, b]
    if res is not None:
        in_specs.append(pl.BlockSpec((tm, tn), lambda i, j, k: (i, j)))
        args.append(res)

    def body(*refs):
        a_ref, b_ref = refs[0], refs[1]
        r_ref = refs[2] if res is not None else None
        p = jnp.dot(a_ref[...], b_ref[...], preferred_element_type=F32)
        if nk == 1:
            o_ref = refs[-1]
            o_ref[...] = (p if r_ref is None else p + r_ref[...]).astype(out_dtype)
            return
        o_ref, acc = refs[-2], refs[-1]
        k = pl.program_id(2)

        @pl.when(k == 0)
        def _():
            acc[...] = p

        @pl.when(k > 0)
        def _():
            acc[...] += p

        @pl.when(k == nk - 1)
        def _():
            r = acc[...]
            if r_ref is not None:
                r = r + r_ref[...]
            o_ref[...] = r.astype(out_dtype)

    outs, carried = _carried_call(
        body, name=name, grid=(M // tm, N // tn, nk), in_specs=in_specs,
        out_specs=[pl.BlockSpec((tm, tn), lambda i, j, k: (i, j))],
        out_shape=[jax.ShapeDtypeStruct((M, N), out_dtype)],
        scratch_shapes=[pltpu.VMEM((tm, tn), F32)] if nk > 1 else [], args=args,
        semantics=("parallel", "parallel", "arbitrary"), carry=carry)
    return outs[0] if carry is None else (outs[0], carried)


def _mm_nt(a, b, *, out_dtype, name, carry=None, a_shift=0):
    M = a.shape[-2]
    wp = a.shape[-1]
    Kc = wp * (a.shape[0] if a.ndim == 3 else 1)
    cs = b.shape[-1]
    N = b.shape[-2]
    tm, tn, tk = _blk(M, 1024), _blk(N, 1024), _blk(math.gcd(cs, wp), 4096)
    nks, nka, nk = cs // tk, wp // tk, Kc // tk
    if b.ndim == 3:
        b_spec = pl.BlockSpec((None, tn, tk), lambda i, j, k: (k // nks, j, k % nks))
    else:
        b_spec = pl.BlockSpec((tn, tk), lambda i, j, k: (j, k))
    if a.ndim == 3:
        n_pieces = a.shape[0]
        a_spec = pl.BlockSpec((None, tm, tk), lambda i, j, k: ((k // nka + a_shift) % n_pieces, i, k % nka))
    else:
        a_spec = pl.BlockSpec((tm, tk), lambda i, j, k: (i, k))

    def body(a_ref, b_ref, o_ref, *scr):
        p = lax.dot_general(a_ref[...], b_ref[...], (((1,), (1,)), ((), ())), preferred_element_type=F32)
        if nk == 1:
            o_ref[...] = p.astype(out_dtype)
            return
        acc = scr[0]
        k = pl.program_id(2)

        @pl.when(k == 0)
        def _():
            acc[...] = p

        @pl.when(k > 0)
        def _():
            acc[...] += p

        @pl.when(k == nk - 1)
        def _():
            o_ref[...] = acc[...].astype(out_dtype)

    outs, carried = _carried_call(
        body, name=name, grid=(M // tm, N // tn, nk),
        in_specs=[a_spec, b_spec],
        out_specs=[pl.BlockSpec((tm, tn), lambda i, j, k: (i, j))],
        out_shape=[jax.ShapeDtypeStruct((M, N), out_dtype)],
        scratch_shapes=[pltpu.VMEM((tm, tn), F32)] if nk > 1 else [], args=[a, b],
        semantics=("parallel", "parallel", "arbitrary"), carry=carry)
    return outs[0] if carry is None else (outs[0], carried)


def _mm_tn(a, b, *, out_dtype, name, col_shards=None, carry=None, b_shift=0):
    T, K1 = a.shape
    wp = b.shape[-1]
    N = wp * (b.shape[0] if b.ndim == 3 else 1)
    cs = N // col_shards if col_shards else N
    tm, tn, tk = _blk(K1, 1024), _blk(math.gcd(cs, wp), 1024), _blk(T, 4096)
    nbs, njb, nk = cs // tn, wp // tn, T // tk
    if b.ndim == 3:
        n_pieces = b.shape[0]
        b_spec = pl.BlockSpec((None, tk, tn), lambda i, j, k: ((j // njb + b_shift) % n_pieces, k, j % njb))
    else:
        b_spec = pl.BlockSpec((tk, tn), lambda i, j, k: (k, j))
    if col_shards:
        o_spec = pl.BlockSpec((None, tm, tn), lambda i, j, k: (j // nbs, i, j % nbs))
        o_shape = jax.ShapeDtypeStruct((col_shards, K1, cs), out_dtype)
    else:
        o_spec = pl.BlockSpec((tm, tn), lambda i, j, k: (i, j))
        o_shape = jax.ShapeDtypeStruct((K1, N), out_dtype)

    def body(a_ref, b_ref, o_ref, *scr):
        p = lax.dot_general(a_ref[...], b_ref[...], (((0,), (0,)), ((), ())), preferred_element_type=F32)
        if nk == 1:
            o_ref[...] = p.astype(out_dtype)
            return
        acc = scr[0]
        k = pl.program_id(2)

        @pl.when(k == 0)
        def _():
            acc[...] = p

        @pl.when(k > 0)
        def _():
            acc[...] += p

        @pl.when(k == nk - 1)
        def _():
            o_ref[...] = acc[...].astype(out_dtype)

    outs, carried = _carried_call(
        body, name=name, grid=(K1 // tm, N // tn, nk),
        in_specs=[pl.BlockSpec((tk, tm), lambda i, j, k: (k, i)), b_spec],
        out_specs=[o_spec], out_shape=[o_shape],
        scratch_shapes=[pltpu.VMEM((tm, tn), F32)] if nk > 1 else [], args=[a, b],
        semantics=("parallel", "parallel", "arbitrary"), carry=carry)
    return outs[0] if carry is None else (outs[0], carried)


def _rms_fwd(x, g, *, name, carry=None):
    T, D = x.shape
    tm = _blk(T, 256)

    def body(x_ref, g_ref, o_ref):
        xv = x_ref[...]
        r = lax.rsqrt(jnp.mean(xv * xv, axis=-1, keepdims=True) + EPS)
        o_ref[...] = (xv * r * g_ref[...]).astype(BF16)

    outs, carried = _carried_call(
        body, name=name, grid=(T // tm,),
        in_specs=[pl.BlockSpec((tm, D), lambda i: (i, 0)), pl.BlockSpec((1, D), lambda i: (0, 0))],
        out_specs=[pl.BlockSpec((tm, D), lambda i: (i, 0))],
        out_shape=[jax.ShapeDtypeStruct((T, D), BF16)], scratch_shapes=[], args=[x, g],
        semantics=("parallel",), carry=carry)
    return outs[0] if carry is None else (outs[0], carried)


def _rms_bwd(x, g, dh, dres, *, name):
    T, D = x.shape
    tm = _blk(T, 256)

    def body(x_ref, g_ref, dh_ref, dres_ref, dx_ref, dxb_ref, dg_ref):
        i = pl.program_id(0)
        xv = x_ref[...]
        r = lax.rsqrt(jnp.mean(xv * xv, axis=-1, keepdims=True) + EPS)
        n = xv * r
        dh_v = dh_ref[...]
        dn = dh_v * g_ref[...]
        dx = dres_ref[...] + r * (dn - n * jnp.mean(dn * n, axis=-1, keepdims=True))
        dx_ref[...] = dx
        dxb_ref[...] = dx.astype(BF16)
        part = jnp.sum(dh_v * n, axis=0, keepdims=True)

        @pl.when(i == 0)
        def _():
            dg_ref[...] = part

        @pl.when(i > 0)
        def _():
            dg_ref[...] += part

    row = pl.BlockSpec((tm, D), lambda i: (i, 0))
    vec = pl.BlockSpec((1, D), lambda i: (0, 0))
    return pl.pallas_call(
        body, name=name, grid=(T // tm,),
        in_specs=[row, vec, row, row], out_specs=[row, row, vec],
        out_shape=[jax.ShapeDtypeStruct((T, D), F32), jax.ShapeDtypeStruct((T, D), BF16),
                   jax.ShapeDtypeStruct((1, D), F32)],
        compiler_params=_cp("arbitrary"),
    )(x, g, dh, dres)


def _loss_head(x2, tgt, g, *, name):
    T, D = x2.shape
    tm = _blk(T, 256)

    def body(x_ref, t_ref, g_ref, l_ref, dx_ref, dxb_ref, dg_ref):
        i = pl.program_id(0)
        xv = x_ref[...]
        gv = g_ref[...]
        r = lax.rsqrt(jnp.mean(xv * xv, axis=-1, keepdims=True) + EPS)
        n = xv * r
        err = n * gv - t_ref[...]
        dy = err * (1.0 / D)
        dn = dy * gv
        dx = r * (dn - n * jnp.mean(dn * n, axis=-1, keepdims=True))
        dx_ref[...] = dx
        dxb_ref[...] = dx.astype(BF16)
        lpart = jnp.sum(err * err, axis=0, keepdims=True)
        gpart = jnp.sum(dy * n, axis=0, keepdims=True)

        @pl.when(i == 0)
        def _():
            l_ref[...] = lpart
            dg_ref[...] = gpart

        @pl.when(i > 0)
        def _():
            l_ref[...] += lpart
            dg_ref[...] += gpart

    row = pl.BlockSpec((tm, D), lambda i: (i, 0))
    vec = pl.BlockSpec((1, D), lambda i: (0, 0))
    return pl.pallas_call(
        body, name=name, grid=(T // tm,),
        in_specs=[row, row, vec], out_specs=[vec, row, row, vec],
        out_shape=[jax.ShapeDtypeStruct((1, D), F32), jax.ShapeDtypeStruct((T, D), F32),
                   jax.ShapeDtypeStruct((T, D), BF16), jax.ShapeDtypeStruct((1, D), F32)],
        compiler_params=_cp("arbitrary"),
    )(x2, tgt, g)


def _merge_fwd(proj, pa, pb, *, off_a, name, carry=None):
    T, D = pa.shape
    tm, tn = _blk(T, 256), _blk(D, 1024)
    oa, ob = off_a // tn, (off_a + D) // tn

    def body(ga_ref, gb_ref, pa_ref, pb_ref, o_ref):
        o_ref[...] = (_sigmoid(ga_ref[...]) * pa_ref[...] + _sigmoid(gb_ref[...]) * pb_ref[...]).astype(BF16)

    blk = pl.BlockSpec((tm, tn), lambda i, j: (i, j))
    outs, carried = _carried_call(
        body, name=name, grid=(T // tm, D // tn),
        in_specs=[pl.BlockSpec((tm, tn), lambda i, j: (i, oa + j)),
                  pl.BlockSpec((tm, tn), lambda i, j: (i, ob + j)), blk, blk],
        out_specs=[blk], out_shape=[jax.ShapeDtypeStruct((T, D), BF16)], scratch_shapes=[],
        args=[proj, proj, pa, pb], semantics=("parallel", "parallel"), carry=carry)
    return outs[0] if carry is None else (outs[0], carried)


def _merge_bwd(proj, pa, pb, dm, *, off_a, name):
    T, D = pa.shape
    tm, tn = _blk(T, 256), _blk(D, 1024)
    oa, ob = off_a // tn, (off_a + D) // tn

    def body(ga_ref, gb_ref, pa_ref, pb_ref, dm_ref, dg_ref, dpa_ref, dpb_ref):
        dmv = dm_ref[...]
        sa = _sigmoid(ga_ref[...])
        sb = _sigmoid(gb_ref[...])
        dg_ref[0] = (dmv * pa_ref[...] * sa * (1.0 - sa)).astype(BF16)
        dg_ref[1] = (dmv * pb_ref[...] * sb * (1.0 - sb)).astype(BF16)
        dpa_ref[...] = (dmv * sa).astype(BF16)
        dpb_ref[...] = (dmv * sb).astype(BF16)

    blk = pl.BlockSpec((tm, tn), lambda i, j: (i, j))
    out = jax.ShapeDtypeStruct((T, D), BF16)
    return pl.pallas_call(
        body, name=name, grid=(T // tm, D // tn),
        in_specs=[pl.BlockSpec((tm, tn), lambda i, j: (i, oa + j)),
                  pl.BlockSpec((tm, tn), lambda i, j: (i, ob + j)), blk, blk, blk],
        out_specs=[pl.BlockSpec((2, tm, tn), lambda i, j: (1, i, j)), blk, blk],
        out_shape=[jax.ShapeDtypeStruct((DPROJ_PIECES, T, D), BF16), out, out],
        compiler_params=_cp("parallel", "parallel"),
    )(proj, proj, pa, pb, dm)


def _sg_mask(sg_w, *, name):
    G, C, _ = sg_w.shape

    def body(w_ref, m_ref, mt_ref):
        row = lax.broadcasted_iota(jnp.int32, (C, C), 0)
        col = lax.broadcasted_iota(jnp.int32, (C, C), 1)
        for g in range(G):
            w = jnp.where(row >= col, w_ref[g], 0.0)
            m_ref[g] = w.astype(BF16)
            mt_ref[g] = w.T.astype(BF16)

    out = jax.ShapeDtypeStruct((G, C, C), BF16)
    return pl.pallas_call(body, name=name, out_shape=[out, out])(sg_w)


def _sg_layernorm(zv, lg, lb):
    v = _gelu(zv)
    mu = jnp.mean(v, axis=-1, keepdims=True)
    xc = v - mu
    rstd = lax.rsqrt(jnp.mean(xc * xc, axis=-1, keepdims=True) + EPS)
    vhat = xc * rstd
    return vhat, rstd, vhat * lg + lb


def _sg_fwd(proj, lg, lb, wm, bt, *, name, carry=None):
    T = proj.shape[0]
    G, C, _ = wm.shape
    W = lg.shape[-1]
    gd = W // G

    def body(zu_ref, zv_ref, lg_ref, lb_ref, wm_ref, bt_ref, ya_ref, vn_scr):
        _, _, vn = _sg_layernorm(zv_ref[...], lg_ref[...], lb_ref[...])
        vn_scr[...] = vn.astype(BF16)
        for g in range(G):
            cols = slice(g * gd, (g + 1) * gd)
            mixed = jnp.dot(wm_ref[g], vn_scr[:, cols], preferred_element_type=F32) + bt_ref[:, g:g + 1]
            ya_ref[:, cols] = (_gelu(zu_ref[:, cols]) * mixed).astype(BF16)

    vec = pl.BlockSpec((1, W), lambda i: (0, 0))
    outs, carried = _carried_call(
        body, name=name, grid=(T // C,),
        in_specs=[pl.BlockSpec((C, W), lambda i: (i, 0)), pl.BlockSpec((C, W), lambda i: (i, 1)), vec, vec,
                  pl.BlockSpec((G, C, C), lambda i: (0, 0, 0)), pl.BlockSpec((C, G), lambda i: (0, 0))],
        out_specs=[pl.BlockSpec((C, W), lambda i: (i, 0))],
        out_shape=[jax.ShapeDtypeStruct((T, W), BF16)],
        scratch_shapes=[pltpu.VMEM((C, W), BF16)], args=[proj, proj, lg, lb, wm, bt],
        semantics=("parallel",), carry=carry)
    return outs[0] if carry is None else (outs[0], carried)


def _sg_bwd(proj, dya, dproj, lg, lb, wm, wmt, bt, *, name):
    T = proj.shape[0]
    G, C, _ = wm.shape
    W = lg.shape[-1]
    gd = W // G
    n_steps = T // C

    def body(zu_ref, zv_ref, dya_ref, lg_ref, lb_ref, wm_ref, wmt_ref, bt_ref, dproj_in_ref,
             dz_ref, dwm_ref, dbt_ref, dlg_ref, dlb_ref, vn_scr, dvn_scr):
        i = pl.program_id(0)

        @pl.when(i == 0)
        def _():
            dwm_ref[...] = jnp.zeros_like(dwm_ref)
            dbt_ref[...] = jnp.zeros_like(dbt_ref)
            dlg_ref[...] = jnp.zeros_like(dlg_ref)
            dlb_ref[...] = jnp.zeros_like(dlb_ref)

        lgv = lg_ref[...]
        vhat, rstd, vn = _sg_layernorm(zv_ref[...], lgv, lb_ref[...])
        vn_scr[...] = vn.astype(BF16)
        for g in range(G):
            cols = slice(g * gd, (g + 1) * gd)
            vnb = vn_scr[:, cols]
            mixed = jnp.dot(wm_ref[g], vnb, preferred_element_type=F32) + bt_ref[:, g:g + 1]
            u, du = _gelu_and_grad(zu_ref[:, cols])
            dy = dya_ref[:, cols]
            dz_ref[:, cols] = (dy * mixed * du).astype(BF16)
            dmix = dy * u
            dmb = dmix.astype(BF16)
            dbt_ref[:, g:g + 1] += jnp.sum(dmix, axis=1, keepdims=True)
            dwm_ref[g] += lax.dot_general(dmb, vnb, (((1,), (1,)), ((), ())), preferred_element_type=F32)
            dvn_scr[:, cols] = jnp.dot(wmt_ref[g], dmb, preferred_element_type=F32)
        dvn = dvn_scr[...]
        dlg_ref[...] += jnp.sum(dvn * vhat, axis=0, keepdims=True)
        dlb_ref[...] += jnp.sum(dvn, axis=0, keepdims=True)
        dvh = dvn * lgv
        dv = rstd * (dvh - jnp.mean(dvh, axis=-1, keepdims=True)
                     - vhat * jnp.mean(dvh * vhat, axis=-1, keepdims=True))
        _, dgv = _gelu_and_grad(zv_ref[...])
        dz_ref[:, W:] = (dv * dgv).astype(BF16)

        @pl.when(i == n_steps - 1)
        def _():
            row = lax.broadcasted_iota(jnp.int32, (C, C), 0)
            col = lax.broadcasted_iota(jnp.int32, (C, C), 1)
            for g in range(G):
                dwm_ref[g] = jnp.where(row >= col, dwm_ref[g], 0.0)

    vec = pl.BlockSpec((1, W), lambda i: (0, 0))
    mat = pl.BlockSpec((G, C, C), lambda i: (0, 0, 0))
    bts = pl.BlockSpec((C, G), lambda i: (0, 0))
    return pl.pallas_call(
        body, name=name, grid=(n_steps,),
        in_specs=[pl.BlockSpec((C, W), lambda i: (i, 0)), pl.BlockSpec((C, W), lambda i: (i, 1)),
                  pl.BlockSpec((C, W), lambda i: (i, 0)), vec, vec, mat, mat, bts, ANY],
        out_specs=[pl.BlockSpec((None, C, 2 * W), lambda i: (DPROJ_PIECES - 1, i, 0)), mat, bts, vec, vec],
        out_shape=[jax.ShapeDtypeStruct(dproj.shape, dproj.dtype), jax.ShapeDtypeStruct((G, C, C), F32),
                   jax.ShapeDtypeStruct((C, G), F32), jax.ShapeDtypeStruct((1, W), F32),
                   jax.ShapeDtypeStruct((1, W), F32)],
        scratch_shapes=[pltpu.VMEM((C, W), BF16), pltpu.VMEM((C, W), F32)], input_output_aliases={8: 0},
        compiler_params=_cp("arbitrary"),
    )(proj, proj, dya, lg, lb, wm, wmt, bt, dproj)


def _rows_with_prev(ref, r0, rows, ci):
    p0 = pl.multiple_of(jnp.maximum(r0 - SUBLANES, 0), SUBLANES)
    prev = jnp.where(ci > 0, ref[pl.ds(p0, SUBLANES), :], 0.0)
    return jnp.concatenate([prev, ref[pl.ds(r0, rows), :]], axis=0)


def _rows_with_next(ref, r0, rows, ci, n_chunks, total):
    n0 = pl.multiple_of(jnp.minimum(r0 + rows, total - SUBLANES), SUBLANES)
    nxt = jnp.where(ci < n_chunks - 1, ref[pl.ds(n0, SUBLANES), :], 0.0)
    return jnp.concatenate([ref[pl.ds(r0, rows), :], nxt], axis=0)


def _delayed(xx, k, rows):
    if k == 0:
        return xx[SUBLANES:, :]
    return pltpu.roll(xx, k, 0)[SUBLANES:, :]


def _advanced(xx, k, rows):
    if k == 0:
        return xx[:rows, :]
    return pltpu.roll(xx, rows + SUBLANES - k, 0)[:rows, :]


def _conv_chunk(x_ref, w_ref, b_ref, r0, rows, ci):
    K = w_ref.shape[0]
    xx = _rows_with_prev(x_ref, r0, rows, ci)
    out = _delayed(xx, K - 1, rows) * w_ref[0:1, :]
    for k in range(1, K):
        out = out + _delayed(xx, K - 1 - k, rows) * w_ref[k:k + 1, :]
    return out + b_ref[...]


def _ffn_act_fwd(up0, cw, cb, *, batch, name, carry=None):
    T, F2 = up0.shape
    F = F2 // 2
    S = T // batch
    K = cw.shape[0]
    cbk = _blk(F, 512)
    nj = F // cbk
    R = min(64, S // 2)
    n_chunks = S // R

    def body(ug_ref, uv_ref, wg_ref, wv_ref, bg_ref, bv_ref, act_ref):
        def chunk(ci, carry):
            r0 = pl.multiple_of(ci * R, R)
            cg = _conv_chunk(ug_ref, wg_ref, bg_ref, r0, R, ci)
            cv = _conv_chunk(uv_ref, wv_ref, bv_ref, r0, R, ci)
            act_ref[pl.ds(r0, R), :] = (_gelu(cg) * cv).astype(BF16)
            return carry

        lax.fori_loop(0, n_chunks, chunk, 0)

    outs, carried = _carried_call(
        body, name=name, grid=(nj, batch),
        in_specs=[pl.BlockSpec((S, cbk), lambda j, b: (b, j)), pl.BlockSpec((S, cbk), lambda j, b: (b, nj + j)),
                  pl.BlockSpec((K, cbk), lambda j, b: (0, j)), pl.BlockSpec((K, cbk), lambda j, b: (0, nj + j)),
                  pl.BlockSpec((1, cbk), lambda j, b: (0, j)), pl.BlockSpec((1, cbk), lambda j, b: (0, nj + j))],
        out_specs=[pl.BlockSpec((S, cbk), lambda j, b: (b, j))],
        out_shape=[jax.ShapeDtypeStruct((T, F), BF16)], scratch_shapes=[],
        args=[up0, up0, cw, cw, cb, cb], semantics=("parallel", "parallel"), carry=carry)
    return outs[0] if carry is None else (outs[0], carried)


def _ffn_act_bwd(up0, cw, cb, dact, *, batch, name):
    T, F2 = up0.shape
    F = F2 // 2
    S = T // batch
    K = cw.shape[0]
    cbk = _blk(F, 512)
    nj = F // cbk
    R = min(64, S // 2)
    n_chunks = S // R

    def body(ug_ref, uv_ref, wg_ref, wv_ref, bg_ref, bv_ref, da_ref,
             du_ref, dw_g_ref, dw_v_ref, db_g_ref, db_v_ref, dcg_scr, dcv_scr):
        b = pl.program_id(1)

        def chunk_a(ci, acc):
            r0 = pl.multiple_of(ci * R, R)
            xg = _rows_with_prev(ug_ref, r0, R, ci)
            xv = _rows_with_prev(uv_ref, r0, R, ci)
            dg_taps = [_delayed(xg, K - 1 - k, R) for k in range(K)]
            dv_taps = [_delayed(xv, K - 1 - k, R) for k in range(K)]
            cg = dg_taps[0] * wg_ref[0:1, :]
            cv = dv_taps[0] * wv_ref[0:1, :]
            for k in range(1, K):
                cg = cg + dg_taps[k] * wg_ref[k:k + 1, :]
                cv = cv + dv_taps[k] * wv_ref[k:k + 1, :]
            cg = cg + bg_ref[...]
            cv = cv + bv_ref[...]
            gl, dgl = _gelu_and_grad(cg)
            da = da_ref[pl.ds(r0, R), :]
            dcg = da * cv * dgl
            dcv = da * gl
            dcg_scr[pl.ds(r0, R), :] = dcg
            dcv_scr[pl.ds(r0, R), :] = dcv
            new = []
            for k in range(K):
                new.append(acc[k] + jnp.sum(dcg * dg_taps[k], axis=0, keepdims=True))
            for k in range(K):
                new.append(acc[K + k] + jnp.sum(dcv * dv_taps[k], axis=0, keepdims=True))
            new.append(acc[2 * K] + jnp.sum(dcg, axis=0, keepdims=True))
            new.append(acc[2 * K + 1] + jnp.sum(dcv, axis=0, keepdims=True))
            return tuple(new)

        zero = jnp.zeros((1, cbk), F32)
        acc = lax.fori_loop(0, n_chunks, chunk_a, (zero,) * (2 * K + 2))

        def chunk_b(ci, carry):
            r0 = pl.multiple_of(ci * R, R)
            dg = _rows_with_next(dcg_scr, r0, R, ci, n_chunks, S)
            dv = _rows_with_next(dcv_scr, r0, R, ci, n_chunks, S)
            og = _advanced(dg, 0, R) * wg_ref[K - 1:K, :]
            ov = _advanced(dv, 0, R) * wv_ref[K - 1:K, :]
            for j in range(1, K):
                og = og + _advanced(dg, j, R) * wg_ref[K - 1 - j:K - j, :]
                ov = ov + _advanced(dv, j, R) * wv_ref[K - 1 - j:K - j, :]
            du_ref[0, pl.ds(r0, R), :] = og.astype(BF16)
            du_ref[1, pl.ds(r0, R), :] = ov.astype(BF16)
            return carry

        lax.fori_loop(0, n_chunks, chunk_b, 0)

        @pl.when(b == 0)
        def _():
            for k in range(K):
                dw_g_ref[k:k + 1, :] = acc[k]
                dw_v_ref[k:k + 1, :] = acc[K + k]
            db_g_ref[...] = acc[2 * K]
            db_v_ref[...] = acc[2 * K + 1]

        @pl.when(b > 0)
        def _():
            for k in range(K):
                dw_g_ref[k:k + 1, :] += acc[k]
                dw_v_ref[k:k + 1, :] += acc[K + k]
            db_g_ref[...] += acc[2 * K]
            db_v_ref[...] += acc[2 * K + 1]

    seq = pl.BlockSpec((S, cbk), lambda j, b: (b, j))
    wk = pl.BlockSpec((K, cbk), lambda j, b: (0, j))
    w1 = pl.BlockSpec((1, cbk), lambda j, b: (0, j))
    outs = pl.pallas_call(
        body, name=name, grid=(nj, batch),
        in_specs=[seq, pl.BlockSpec((S, cbk), lambda j, b: (b, nj + j)),
                  wk, pl.BlockSpec((K, cbk), lambda j, b: (0, nj + j)),
                  w1, pl.BlockSpec((1, cbk), lambda j, b: (0, nj + j)), seq],
        out_specs=[pl.BlockSpec((2, S, cbk), lambda j, b: (0, b, j)), wk, wk, w1, w1],
        out_shape=[jax.ShapeDtypeStruct((2, T, F), BF16),
                   jax.ShapeDtypeStruct((K, F), F32), jax.ShapeDtypeStruct((K, F), F32),
                   jax.ShapeDtypeStruct((1, F), F32), jax.ShapeDtypeStruct((1, F), F32)],
        scratch_shapes=[pltpu.VMEM((S, cbk), F32), pltpu.VMEM((S, cbk), F32)],
        compiler_params=_cp("parallel", "arbitrary"),
    )(up0, up0, cw, cw, cb, cb, dact)
    du, dwg, dwv, dbg, dbv = outs
    return du, jnp.concatenate([dwg, dwv], axis=1), jnp.concatenate([dbg, dbv], axis=1)


def _lru_gate_rows(xr_ref, cw_ref, cb_ref, wa_ref, wx_ref, ba_ref, bx_ref, xc_scr, za_scr, zx_scr, S, R):
    def chunk(ci, carry):
        r0 = pl.multiple_of(ci * R, R)
        xc = _conv_chunk(xr_ref, cw_ref, cb_ref, r0, R, ci)
        xc_scr[pl.ds(r0, R), :] = xc
        xb = xc.astype(BF16)
        za_scr[pl.ds(r0, R), :] = jnp.dot(xb, wa_ref[...], preferred_element_type=F32) + ba_ref[...]
        zx_scr[pl.ds(r0, R), :] = jnp.dot(xb, wx_ref[...], preferred_element_type=F32) + bx_ref[...]
        return carry

    lax.fori_loop(0, S // R, chunk, 0)


def _lru_gates(za, zx, sp):
    ra = _sigmoid(za)
    ig = _sigmoid(zx)
    la = -LRU_C * ra * sp
    a = jnp.exp(la)
    s = jnp.sqrt(_neg_expm1(2.0 * la))
    return ra, ig, a, s


def _lru_fwd(proj, cw, cb, wa, wx, ba, bx, lam, *, batch, off_x, name, carry=None):
    T = proj.shape[0]
    H, Dh, _ = wa.shape
    W = H * Dh
    S = T // batch
    K = cw.shape[0]
    ox, oy = off_x // Dh, (off_x + W) // Dh
    R = min(256, S // 2)
    n16 = S // BF16_ROWS

    def body(xr_ref, yr_ref, cw_ref, cb_ref, wa_ref, wx_ref, ba_ref, bx_ref, lam_ref,
             yb_ref, h_ref, xc_scr, za_scr, zx_scr):
        _lru_gate_rows(xr_ref, cw_ref, cb_ref, wa_ref, wx_ref, ba_ref, bx_ref, xc_scr, za_scr, zx_scr, S, R)
        sp = _softplus(-lam_ref[...])
        row = lax.broadcasted_iota(jnp.int32, (SUBLANES, Dh), 0)

        def tile(r0, carry):
            rows = pl.ds(r0, SUBLANES)
            xc = xc_scr[rows, :]
            _, ig, a, s = _lru_gates(za_scr[rows, :], zx_scr[rows, :], sp)
            A, B = a, s * (ig * xc)
            for d in (1, 2, 4):
                m = row >= d
                Bs = pltpu.roll(B, d, 0)
                As = pltpu.roll(A, d, 0)
                B = jnp.where(m, B + A * Bs, B)
                A = jnp.where(m, A * As, A)
            hh = B + A * carry
            h_ref[rows, :] = hh
            return hh, hh[SUBLANES - 1:SUBLANES, :]

        def step(i, carry):
            r0 = pl.multiple_of(i * BF16_ROWS, BF16_ROWS)
            h0, carry = tile(r0, carry)
            h1, carry = tile(r0 + SUBLANES, carry)
            hh = jnp.concatenate([h0, h1], axis=0)
            yb_ref[pl.ds(r0, BF16_ROWS), :] = (hh * _gelu(yr_ref[pl.ds(r0, BF16_ROWS), :])).astype(BF16)
            return carry

        lax.fori_loop(0, n16, step, jnp.zeros((1, Dh), F32))

    vec = pl.BlockSpec((1, Dh), lambda b, h: (0, h))
    wsp = pl.BlockSpec((None, Dh, Dh), lambda b, h: (h, 0, 0))
    seq = pl.BlockSpec((S, Dh), lambda b, h: (b, h))
    outs, carried = _carried_call(
        body, name=name, grid=(batch, H),
        in_specs=[pl.BlockSpec((S, Dh), lambda b, h: (b, ox + h)), pl.BlockSpec((S, Dh), lambda b, h: (b, oy + h)),
                  pl.BlockSpec((K, Dh), lambda b, h: (0, h)), vec, wsp, wsp, vec, vec, vec],
        out_specs=[seq, seq],
        out_shape=[jax.ShapeDtypeStruct((T, W), BF16), jax.ShapeDtypeStruct((T, W), F32)],
        scratch_shapes=[pltpu.VMEM((S, Dh), F32)] * 3,
        args=[proj, proj, cw, cb, wa, wx, ba, bx, lam], semantics=("parallel", "parallel"), carry=carry)
    return (outs[0], outs[1]) if carry is None else (outs[0], outs[1], carried)


def _lru_bwd(proj, hseq, dyb, dproj, cw, cb, wa, wx, wat, wxt, ba, bx, lam, *, batch, off_x, name):
    T = proj.shape[0]
    H, Dh, _ = wa.shape
    W = H * Dh
    S = T // batch
    K = cw.shape[0]
    ox, oy = off_x // Dh, (off_x + W) // Dh
    R = min(256, S // 2)
    n_chunks = S // R
    n16 = S // BF16_ROWS

    def body(xr_ref, yr_ref, h_ref, dyb_ref, cw_ref, cb_ref, wa_ref, wx_ref, wat_ref, wxt_ref,
             ba_ref, bx_ref, lam_ref, dproj_in_ref,
             dxy_ref, dwa_ref, dwx_ref, dcw_ref, dcb_ref, dba_ref, dbx_ref, dlam_ref,
             xc_scr, za_scr, zx_scr, dza_scr, dzx_scr, dxc_scr):
        b = pl.program_id(1)
        _lru_gate_rows(xr_ref, cw_ref, cb_ref, wa_ref, wx_ref, ba_ref, bx_ref, xc_scr, za_scr, zx_scr, S, R)
        lam_v = lam_ref[...]
        sp = _softplus(-lam_v)
        row = lax.broadcasted_iota(jnp.int32, (SUBLANES, Dh), 0)

        def tile(r0, carry):
            a_next, g_next, s_ba, s_bx, s_lam = carry
            rows = pl.ds(r0, SUBLANES)
            xc = xc_scr[rows, :]
            ra, ig, a, s = _lru_gates(za_scr[rows, :], zx_scr[rows, :], sp)
            hh = h_ref[rows, :]
            gy, dgy = _gelu_and_grad(yr_ref[rows, :])
            dy = dyb_ref[rows, :]
            dyr = dy * hh * dgy
            C = jnp.where(row == SUBLANES - 1, a_next, pltpu.roll(a, SUBLANES - 1, 0))
            B = dy * gy
            for d in (1, 2, 4):
                m = row < SUBLANES - d
                Bs = pltpu.roll(B, SUBLANES - d, 0)
                Cs = pltpu.roll(C, SUBLANES - d, 0)
                B = jnp.where(m, B + C * Bs, B)
                C = jnp.where(m, C * Cs, C)
            G = B + C * g_next
            p0 = pl.multiple_of(jnp.maximum(r0 - SUBLANES, 0), SUBLANES)
            h_before = jnp.where(r0 > 0, h_ref[pl.ds(p0, SUBLANES), :][SUBLANES - 1:SUBLANES, :], 0.0)
            h_prev = jnp.where(row == 0, h_before, pltpu.roll(hh, 1, 0))
            da = G * h_prev
            dig = G * s * xc
            ds = G * ig * xc
            dxc_scr[rows, :] = G * s * ig
            dla = da * a - ds * (a * a) / s
            dza = dla * (-LRU_C * sp) * ra * (1.0 - ra)
            dzx = dig * ig * (1.0 - ig)
            dza_scr[rows, :] = dza
            dzx_scr[rows, :] = dzx
            carry = (a[0:1, :], G[0:1, :], s_ba + dza, s_bx + dzx, s_lam + dla * ra)
            return dyr, carry

        def step(it, carry):
            r0 = pl.multiple_of((n16 - 1 - it) * BF16_ROWS, BF16_ROWS)
            d1, carry = tile(r0 + SUBLANES, carry)
            d0, carry = tile(r0, carry)
            dxy_ref[1, pl.ds(r0, BF16_ROWS), :] = jnp.concatenate([d0, d1], axis=0).astype(BF16)
            return carry

        z1 = jnp.zeros((1, Dh), F32)
        z8 = jnp.zeros((SUBLANES, Dh), F32)
        _, _, s_ba, s_bx, s_lam = lax.fori_loop(0, n16, step, (z1, z1, z8, z8, z8))
        dba = jnp.sum(s_ba, axis=0, keepdims=True)
        dbx = jnp.sum(s_bx, axis=0, keepdims=True)
        dlam = jnp.sum(s_lam, axis=0, keepdims=True) * (LRU_C * _sigmoid(-lam_v))

        @pl.when(b == 0)
        def _():
            dwa_ref[...] = jnp.zeros_like(dwa_ref)
            dwx_ref[...] = jnp.zeros_like(dwx_ref)

        def chunk_c(ci, carry):
            r0 = pl.multiple_of(ci * R, R)
            rows = pl.ds(r0, R)
            xb = xc_scr[rows, :].astype(BF16)
            dzab = dza_scr[rows, :].astype(BF16)
            dzxb = dzx_scr[rows, :].astype(BF16)
            dwa_ref[...] += lax.dot_general(xb, dzab, (((0,), (0,)), ((), ())), preferred_element_type=F32)
            dwx_ref[...] += lax.dot_general(xb, dzxb, (((0,), (0,)), ((), ())), preferred_element_type=F32)
            dxc_scr[rows, :] += (jnp.dot(dzab, wat_ref[...], preferred_element_type=F32)
                                 + jnp.dot(dzxb, wxt_ref[...], preferred_element_type=F32))
            return carry

        lax.fori_loop(0, n_chunks, chunk_c, 0)

        def chunk_d(ci, acc):
            r0 = pl.multiple_of(ci * R, R)
            dd = _rows_with_next(dxc_scr, r0, R, ci, n_chunks, S)
            xx = _rows_with_prev(xr_ref, r0, R, ci)
            dxc = dd[:R, :]
            out = dxc * cw_ref[K - 1:K, :]
            for j in range(1, K):
                out = out + _advanced(dd, j, R) * cw_ref[K - 1 - j:K - j, :]
            dxy_ref[0, pl.ds(r0, R), :] = out.astype(BF16)
            new = [acc[k] + jnp.sum(dxc * _delayed(xx, K - 1 - k, R), axis=0, keepdims=True) for k in range(K)]
            new.append(acc[K] + jnp.sum(dxc, axis=0, keepdims=True))
            return tuple(new)

        acc = lax.fori_loop(0, n_chunks, chunk_d, (z1,) * (K + 1))

        @pl.when(b == 0)
        def _():
            for k in range(K):
                dcw_ref[k:k + 1, :] = acc[k]
            dcb_ref[...] = acc[K]
            dba_ref[...] = dba
            dbx_ref[...] = dbx
            dlam_ref[...] = dlam

        @pl.when(b > 0)
        def _():
            for k in range(K):
                dcw_ref[k:k + 1, :] += acc[k]
            dcb_ref[...] += acc[K]
            dba_ref[...] += dba
            dbx_ref[...] += dbx
            dlam_ref[...] += dlam

    vec = pl.BlockSpec((1, Dh), lambda h, b: (0, h))
    wsp = pl.BlockSpec((None, Dh, Dh), lambda h, b: (h, 0, 0))
    seq = pl.BlockSpec((S, Dh), lambda h, b: (b, h))
    ck = pl.BlockSpec((K, Dh), lambda h, b: (0, h))
    row_out = jax.ShapeDtypeStruct((1, W), F32)
    return pl.pallas_call(
        body, name=name, grid=(H, batch),
        in_specs=[pl.BlockSpec((S, Dh), lambda h, b: (b, ox + h)), pl.BlockSpec((S, Dh), lambda h, b: (b, oy + h)),
                  seq, seq, ck, vec, wsp, wsp, wsp, wsp, vec, vec, vec, ANY],
        out_specs=[pl.BlockSpec((2, S, Dh), lambda h, b: (0, b, h)), wsp, wsp, ck, vec, vec, vec, vec],
        out_shape=[jax.ShapeDtypeStruct(dproj.shape, dproj.dtype),
                   jax.ShapeDtypeStruct((H, Dh, Dh), F32), jax.ShapeDtypeStruct((H, Dh, Dh), F32),
                   jax.ShapeDtypeStruct((K, W), F32), row_out, row_out, row_out, row_out],
        scratch_shapes=[pltpu.VMEM((S, Dh), F32)] * 6, input_output_aliases={13: 0},
        compiler_params=_cp("parallel", "arbitrary"),
    )(proj, proj, hseq, dyb, cw, cb, wa, wx, wat, wxt, ba, bx, lam, dproj)


def _adamw(w, g, m, v, *, name, pass_grad=False, carry=None):
    R, C = w.shape
    tr, tc = _blk(R, 256), _blk(C, 1024)

    def body(w_ref, g_ref, m_ref, v_ref, d_ref, nm_ref, nv_ref, *g_out):
        gv = g_ref[...]
        if pass_grad:
            g_out[0][...] = gv
        nm = ADAM_B1 * m_ref[...] + (1.0 - ADAM_B1) * gv
        nv = ADAM_B2 * v_ref[...] + (1.0 - ADAM_B2) * (gv * gv)
        m_hat = nm / (1.0 - ADAM_B1 ** ADAM_STEP)
        v_hat = nv / (1.0 - ADAM_B2 ** ADAM_STEP)
        d_ref[...] = -ADAM_LR * (m_hat / (jnp.sqrt(v_hat) + ADAM_EPS) + ADAM_WD * w_ref[...])
        nm_ref[...] = nm
        nv_ref[...] = nv

    blk = pl.BlockSpec((tr, tc), lambda i, j: (i, j))
    out = jax.ShapeDtypeStruct((R, C), F32)
    n_out = 4 if pass_grad else 3
    outs, carried = _carried_call(
        body, name=name, grid=(R // tr, C // tc), in_specs=[blk] * 4, out_specs=[blk] * n_out,
        out_shape=[out] * n_out, scratch_shapes=[], args=[w, g, m, v], semantics=("parallel", "parallel"),
        carry=carry)
    return outs if carry is None else (outs, carried)


def _to_slab(a, pos, dtype, *, name, b=None):
    R, C = a.shape
    tr, tc = _blk(R, 512), _blk(C, 1024)

    def body(p_ref, *refs):
        v = refs[0][...]
        if b is not None:
            v = v + refs[1][...]
        refs[-1][...] = v.astype(dtype)

    blk = pl.BlockSpec((tr, tc), lambda i, j, p_ref: (i, j))
    return pl.pallas_call(
        body, name=name,
        grid_spec=pltpu.PrefetchScalarGridSpec(
            num_scalar_prefetch=1, grid=(R // tr, C // tc), in_specs=[blk] * (1 if b is None else 2),
            out_specs=pl.BlockSpec((None, tr, tc), lambda i, j, p_ref: (p_ref[0], i, j))),
        out_shape=jax.ShapeDtypeStruct((N_CHIPS, R, C), dtype),
        compiler_params=_cp("parallel", "parallel"),
    )(pos, a, *([] if b is None else [b]))


def _sum_chips(q, *, name):
    _, R, C = q.shape
    tr = _blk(R, 1024)

    def body(q_ref, o_ref):
        o_ref[...] = ((q_ref[0] + q_ref[1]) + q_ref[2]) + q_ref[3]

    return pl.pallas_call(body, name=name, grid=(R // tr,),
                          in_specs=[pl.BlockSpec((N_CHIPS, tr, C), lambda i: (0, i, 0))],
                          out_specs=pl.BlockSpec((tr, C), lambda i: (i, 0)),
                          out_shape=jax.ShapeDtypeStruct((R, C), q.dtype), compiler_params=_cp("parallel"))(q)


def _pair_add_halves(g, rb, cpos, *, name):
    n, R, C = g.shape
    hr = R // 2
    tr, tc = _blk(hr, 512), _blk(C, 1024)
    nrb = hr // tr

    def body(c_ref, g_ref, r_ref, o_ref):
        o_ref[...] = (g_ref[...].astype(F32) + r_ref[...].astype(F32)).astype(o_ref.dtype)

    return pl.pallas_call(
        body, name=name,
        grid_spec=pltpu.PrefetchScalarGridSpec(
            num_scalar_prefetch=1, grid=(n, nrb, C // tc),
            in_specs=[pl.BlockSpec((None, tr, tc), lambda s, i, j, c_ref: (s, c_ref[0] * nrb + i, j)),
                      pl.BlockSpec((None, tr, tc), lambda s, i, j, c_ref: (s, i, j))],
            out_specs=pl.BlockSpec((None, tr, tc), lambda s, i, j, c_ref: (s, i, j))),
        out_shape=jax.ShapeDtypeStruct((n, hr, C), g.dtype),
        compiler_params=_cp("parallel", "parallel", "parallel"),
    )(cpos, g, rb)


def _chip_final_add(p, q, pos, *, name):
    _, hr, C = p.shape
    tr, tc = _blk(hr, 512), _blk(C, 1024)
    nrb = hr // tr

    def body(k_ref, p_ref, q_ref, o_ref):
        o_ref[...] = ((p_ref[...].astype(F32) + q_ref[0].astype(F32)) + q_ref[1].astype(F32)) + q_ref[2].astype(F32)

    return pl.pallas_call(
        body, name=name,
        grid_spec=pltpu.PrefetchScalarGridSpec(
            num_scalar_prefetch=1, grid=(nrb, C // tc),
            in_specs=[pl.BlockSpec((None, tr, tc), lambda i, j, k_ref: (k_ref[0], i, j)),
                      pl.BlockSpec((N_CHIPS - 1, tr, tc), lambda i, j, k_ref: (0, i, j))],
            out_specs=pl.BlockSpec((tr, tc), lambda i, j, k_ref: (k_ref[1] * nrb + i, j))),
        out_shape=jax.ShapeDtypeStruct((2 * hr, C), F32),
        compiler_params=_cp("parallel", "parallel"),
    )(pos, p, q)


def _pair_swap(v, *, name):
    def body(v_ref, o_ref, send_sem, recv_sem):
        x, y, c = _mesh_pos()
        cp = pltpu.make_async_remote_copy(src_ref=v_ref, dst_ref=o_ref, send_sem=send_sem, recv_sem=recv_sem,
                                          device_id=(x, y, 1 - c), device_id_type=MESH)
        cp.start()
        cp.wait()

    return pl.pallas_call(
        body, name=name, in_specs=[ANY], out_specs=ANY, out_shape=jax.ShapeDtypeStruct(v.shape, v.dtype),
        scratch_shapes=[pltpu.SemaphoreType.DMA, pltpu.SemaphoreType.DMA],
    )(v)


def _pair_sum(g, cpos, *, tag):
    rb, = _comm_call(_Carry(_PairSwap(g)), name=f"rs_pair_swap_{tag}")
    return _pair_add_halves(g, rb, cpos, name=f"rs_pair_add_{tag}")


def _all_reduce(v, pos, *, tag):
    other = _pair_swap(v, name=f"ar_pair_swap_{tag}")
    slabs = _to_slab(v, pos, F32, b=other, name=f"ar_pair_add_{tag}")
    slabs, = _comm_call(_Carry(_GatherSlabs(slabs)), name=f"ar_allgather_{tag}")
    return _sum_chips(slabs, name=f"ar_sum_{tag}")


def _pack(arrays, unit, total_unit=None):
    parts, n = [], 0
    for a in arrays:
        flat = a.reshape(-1)
        pad = (-flat.shape[0]) % unit
        parts.append(jnp.pad(flat, (0, pad)) if pad else flat)
        n += flat.shape[0] + pad
    if total_unit and n % total_unit:
        parts.append(jnp.zeros((-n) % total_unit, arrays[0].dtype))
    return jnp.concatenate(parts).reshape(-1, LANES)


def _unpack(packed, shapes, unit):
    lead = packed.shape[:-2]
    flat = packed.reshape(lead + (-1,))
    out, pos = [], 0
    for shp in shapes:
        n = math.prod(shp)
        out.append(flat[..., pos:pos + n].reshape(lead + tuple(shp)))
        pos += n + (-n) % unit
    return out


def kernel(x, g_mix, w_in, sg_ln_g, sg_ln_b, sg_w, sg_b, lru_conv_w, lru_conv_b, lru_wa, lru_ba, lru_wx, lru_bx, lru_lam, p_sg, p_lru, w_out, g_ffn, w_up, ffn_conv_w, ffn_conv_b, w_down, g_final, loss_target, m_g_mix, m_w_in, m_sg_ln_g, m_sg_ln_b, m_sg_w, m_sg_b, m_lru_conv_w, m_lru_conv_b, m_lru_wa, m_lru_ba, m_lru_wx, m_lru_bx, m_lru_lam, m_p_sg, m_p_lru, m_w_out, m_g_ffn, m_w_up, m_ffn_conv_w, m_ffn_conv_b, m_w_down, m_g_final, v_g_mix, v_w_in, v_sg_ln_g, v_sg_ln_b, v_sg_w, v_sg_b, v_lru_conv_w, v_lru_conv_b, v_lru_wa, v_lru_ba, v_lru_wx, v_lru_bx, v_lru_lam, v_p_sg, v_p_lru, v_w_out, v_g_ffn, v_w_up, v_ffn_conv_w, v_ffn_conv_b, v_w_down, v_g_final):
    params = dict(g_mix=g_mix, w_in=w_in, sg_ln_g=sg_ln_g, sg_ln_b=sg_ln_b, sg_w=sg_w, sg_b=sg_b,
                  lru_conv_w=lru_conv_w, lru_conv_b=lru_conv_b, lru_wa=lru_wa, lru_ba=lru_ba, lru_wx=lru_wx,
                  lru_bx=lru_bx, lru_lam=lru_lam, p_sg=p_sg, p_lru=p_lru, w_out=w_out, g_ffn=g_ffn, w_up=w_up,
                  ffn_conv_w=ffn_conv_w, ffn_conv_b=ffn_conv_b, w_down=w_down, g_final=g_final)
    mom1 = dict(g_mix=m_g_mix, w_in=m_w_in, sg_ln_g=m_sg_ln_g, sg_ln_b=m_sg_ln_b, sg_w=m_sg_w, sg_b=m_sg_b,
                lru_conv_w=m_lru_conv_w, lru_conv_b=m_lru_conv_b, lru_wa=m_lru_wa, lru_ba=m_lru_ba,
                lru_wx=m_lru_wx, lru_bx=m_lru_bx, lru_lam=m_lru_lam, p_sg=m_p_sg, p_lru=m_p_lru, w_out=m_w_out,
                g_ffn=m_g_ffn, w_up=m_w_up, ffn_conv_w=m_ffn_conv_w, ffn_conv_b=m_ffn_conv_b, w_down=m_w_down,
                g_final=m_g_final)
    mom2 = dict(g_mix=v_g_mix, w_in=v_w_in, sg_ln_g=v_sg_ln_g, sg_ln_b=v_sg_ln_b, sg_w=v_sg_w, sg_b=v_sg_b,
                lru_conv_w=v_lru_conv_w, lru_conv_b=v_lru_conv_b, lru_wa=v_lru_wa, lru_ba=v_lru_ba,
                lru_wx=v_lru_wx, lru_bx=v_lru_bx, lru_lam=v_lru_lam, p_sg=v_p_sg, p_lru=v_p_lru, w_out=v_w_out,
                g_ffn=v_g_ffn, w_up=v_w_up, ffn_conv_w=v_ffn_conv_w, ffn_conv_b=v_ffn_conv_b, w_down=v_w_down,
                g_final=v_g_final)
    names = list(params)
    big = ["w_in", "p_sg", "p_lru", "w_out", "w_up", "w_down"]
    col_sharded = {"w_in", "p_sg", "w_up"}
    small = [n for n in names if n not in big]

    batch, S, D = x.shape
    T = batch * S
    W_sg = sg_ln_g.shape[-1]
    H, _, Dh = lru_wa.shape[1:]
    W_lru = H * Dh
    K_lru = lru_conv_w.shape[1]
    K_ffn = ffn_conv_w.shape[1]
    F2 = ffn_conv_b.shape[-1]
    off_lru = 2 * W_sg
    off_gate = off_lru + 2 * W_lru

    cx, cy, cc = _mesh_pos()
    chip = 2 * cx + cy
    cpos = jnp.reshape(cc, (1,)).astype(jnp.int32)
    pos = jnp.stack([chip, cc]).astype(jnp.int32)

    xf = x.reshape(T, D)
    tgt = loss_target.reshape(T, D)

    wb = {n: _to_slab(params[n][0], pos, BF16, name=f"cast_{n}") for n in big}
    rows = {n: wb[n].shape[1] for n in big}
    sharded_small = ["lru_conv_w", "ffn_conv_w", "lru_wa", "lru_wx"]
    unit_g = 2 * BF16_ROWS * LANES
    pack_g = 256 * LANES
    sm_shapes = [params[n][0].shape for n in sharded_small]
    sm = _to_slab(_pack([params[n][0] for n in sharded_small], unit_g, pack_g), pos, F32, name="slab_small")

    parts = 32

    def ici(n, lo=0, hi=parts):
        return _GatherIci(wb[n], rows[n] * lo // parts, rows[n] * hi // parts)

    def fwd(n, lo=0, hi=parts):
        return _GatherFwd(wb[n], rows[n] * lo // parts, rows[n] * hi // parts)

    quarters = [_GatherRows(wb["w_in"], rows["w_in"] * i // 4, rows["w_in"] * (i + 1) // 4) for i in range(4)]
    w_in_g, sm = _comm_call(_Carry(*quarters, _GatherRows(sm, 0, sm.shape[1])), name="gather_first")
    cwl_s, cwf_s, wa_s, wx_s = _unpack(sm, sm_shapes, unit_g)
    lru_cw = jnp.transpose(cwl_s, (1, 0, 2)).reshape(K_lru, W_lru)
    ffn_cw = jnp.transpose(cwf_s, (1, 0, 2)).reshape(K_ffn, F2)
    wa_full = jnp.transpose(wa_s, (1, 0, 2, 3)).reshape(H, Dh, Dh)
    wx_full = jnp.transpose(wx_s, (1, 0, 2, 3)).reshape(H, Dh, Dh)
    wa_b, wx_b = wa_full.astype(BF16), wx_full.astype(BF16)
    wat_b, wxt_b = jnp.swapaxes(wa_b, 1, 2), jnp.swapaxes(wx_b, 1, 2)

    wm, wmt = _sg_mask(sg_w[0], name="sg_mask")
    bt = sg_b[0].T

    h1 = _rms_fwd(xf, g_mix, name="rms1_fwd")
    proj, (wb["p_sg"], wb["p_lru"], wb["w_out"], wb["w_up"]) = _mm_nn(
        h1, w_in_g, out_dtype=F32, name="mm_proj",
        carry=_Carry(ici("p_sg"), ici("p_lru"), ici("w_out"), ici("w_up", 0, 8)))
    y_a, (wb["w_up"],) = _sg_fwd(proj, sg_ln_g, sg_ln_b, wm, bt, name="sg_fwd", carry=_Carry(ici("w_up", 8, 9)))
    y_b, hseq, (p_sg_g, p_lru_g, w_out_g, wb["w_up"]) = _lru_fwd(
        proj, lru_cw, lru_conv_b, wa_b, wx_b, lru_ba, lru_bx, lru_lam, batch=batch, off_x=off_lru, name="lru_fwd",
        carry=_Carry(fwd("p_sg"), fwd("p_lru"), fwd("w_out"), fwd("w_up", 0, 8), fwd("w_up", 8, 9),
                     ici("w_up", 9, 18)))
    p_lru_g = p_lru_g.reshape(-1, D)
    w_out_g = w_out_g.reshape(-1, D)
    pa, (wb["w_up"],) = _mm_nn(y_a, p_sg_g, out_dtype=F32, name="mm_pa",
                               carry=_Carry(fwd("w_up", 9, 18), ici("w_up", 18, 20)))
    pb, (wb["w_up"],) = _mm_nn(y_b, p_lru_g, out_dtype=F32, name="mm_pb",
                               carry=_Carry(fwd("w_up", 18, 20), ici("w_up", 20, 25)))
    merged, (wb["w_up"],) = _merge_fwd(proj, pa, pb, off_a=off_gate, name="merge_fwd",
                                       carry=_Carry(fwd("w_up", 20, 25), ici("w_up", 25, 28)))
    x1, (wb["w_up"],) = _mm_nn(merged, w_out_g, out_dtype=F32, res=xf, name="mm_out",
                               carry=_Carry(fwd("w_up", 25, 28), ici("w_up", 28, 32)))
    h2, (w_up_g,) = _rms_fwd(x1, g_ffn, name="rms2_fwd", carry=_Carry(fwd("w_up", 28, 32)))
    up0, (wb["w_down"],) = _mm_nn(h2, w_up_g, out_dtype=F32, name="mm_up", carry=_Carry(ici("w_down")))
    act, (w_down_g,) = _ffn_act_fwd(up0, ffn_cw, ffn_conv_b, batch=batch, name="ffn_act_fwd",
                                    carry=_Carry(fwd("w_down")))
    w_down_g = w_down_g.reshape(-1, D)
    x2 = _mm_nn(act, w_down_g, out_dtype=F32, res=x1, name="mm_down")
    lvec, dx2, dx2_b, dg_final = _loss_head(x2, tgt, g_final.reshape(1, D), name="loss_head")
    loss = lax.psum(jnp.sum(lvec) * (0.5 / D), ("x", "y", "c"))

    assert 2 * W_sg == W_lru == D, "d proj travels in equal column pieces"
    ps, qs = {}, {}

    def pair_add(g, rb, tag):
        return _pair_add_halves(g, rb, cpos, name=f"rs_pair_add_{tag}")

    g = _mm_tn(act, dx2_b, out_dtype=BF16, name="mm_dw_down").reshape(N_CHIPS, -1, D)
    dact, (rb,) = _mm_nt(dx2_b, w_down_g, out_dtype=F32, name="mm_dact", carry=_Carry(_PairSwap(g)))
    ps["w_down"] = pair_add(g, rb, "w_down")
    dup0, d_ffn_cw, d_ffn_cb = _ffn_act_bwd(up0, ffn_cw, ffn_conv_b, dact, batch=batch, name="ffn_act_bwd")
    g, (qs["w_down"],) = _mm_tn(h2, dup0, out_dtype=BF16, col_shards=N_CHIPS, name="mm_dw_up",
                                carry=_Carry(_ChipExchange(ps["w_down"])))
    ps["w_up"] = _pair_sum(g, cpos, tag="w_up")
    dh2, (q_up,) = _mm_nt(dup0, w_up_g, out_dtype=F32, name="mm_dh2",
                          carry=_Carry(_ChipExchange(ps["w_up"], None, 0, 6, 8)))
    dx1, dx1_b, dg_ffn = _rms_bwd(x1, g_ffn, dh2, dx2, name="rms2_bwd")
    g, (q_up,) = _mm_tn(merged, dx1_b, out_dtype=BF16, name="mm_dw_out",
                        carry=_Carry(_ChipExchange(ps["w_up"], q_up, 6, 7, 8)))
    g = g.reshape(N_CHIPS, -1, D)
    dmerged, (rb, qs["w_up"]) = _mm_nt(dx1_b, w_out_g, out_dtype=F32, name="mm_dmerged",
                                       carry=_Carry(_PairSwap(g), _ChipExchange(ps["w_up"], q_up, 7, 8, 8)))
    ps["w_out"] = pair_add(g, rb, "w_out")
    dproj, dpa, dpb = _merge_bwd(proj, pa, pb, dmerged, off_a=off_gate, name="merge_bwd")
    g_sg = _mm_tn(y_a, dpa, out_dtype=BF16, col_shards=N_CHIPS, name="mm_dp_sg")
    g_lru = _mm_tn(y_b, dpb, out_dtype=BF16, name="mm_dp_lru").reshape(N_CHIPS, -1, D)
    dya, (rb,) = _mm_nt(dpa, p_sg_g, out_dtype=F32, name="mm_dya", carry=_Carry(_PairSwap(g_sg)))
    ps["p_sg"] = pair_add(g_sg, rb, "p_sg")
    dyb, (rb,) = _mm_nt(dpb, p_lru_g, out_dtype=F32, name="mm_dyb", carry=_Carry(_PairSwap(g_lru)))
    ps["p_lru"] = pair_add(g_lru, rb, "p_lru")
    dproj, d_wa, d_wx, d_lru_cw, d_lru_cb, d_ba, d_bx, d_lam = _lru_bwd(
        proj, hseq, dyb, dproj, lru_cw, lru_conv_b, wa_b, wx_b, wat_b, wxt_b, lru_ba, lru_bx, lru_lam,
        batch=batch, off_x=off_lru, name="lru_bwd")
    g = jnp.stack([d_wa, d_wx]).reshape(2, H, N_CHIPS, Dh // N_CHIPS, Dh)
    g = jnp.transpose(g, (2, 0, 1, 3, 4)).reshape(N_CHIPS, -1, Dh).astype(BF16)
    ps["gates"] = _pair_sum(g, cpos, tag="gates")
    dproj, d_wm, d_bt, d_lg, d_lb = _sg_bwd(proj, dya, dproj, sg_ln_g, sg_ln_b, wm, wmt, bt, name="sg_bwd")
    late = ["w_out", "p_sg", "p_lru", "gates"]
    g, carried = _mm_tn(h1, dproj, out_dtype=BF16, col_shards=N_CHIPS, name="mm_dw_in", b_shift=DPROJ_SHIFT,
                        carry=_Carry(*[_ChipExchange(ps[n]) for n in late]))
    qs.update(zip(late, carried))
    ps["w_in"] = _pair_sum(g, cpos, tag="w_in")
    dh1, (q_in,) = _mm_nt(dproj, w_in_g, out_dtype=F32, name="mm_dh1", a_shift=DPROJ_SHIFT,
                          carry=_Carry(_ChipExchange(ps["w_in"], None, 0, 6, 8)))
    dx, _, dg_mix = _rms_bwd(xf, g_mix, dh1, dx1, name="rms1_bwd")

    def reduced_shards(tags, name):
        halves = [_chip_final_add(ps[n], qs[n], pos, name=f"rs_final_add_{n}") for n in tags]
        return dict(zip(tags, _comm_call(_Carry(*[_ShareHalves(h) for h in halves]), name=name)))

    grads = reduced_shards([n for n in big if n != "w_in"] + ["gates"], "rs_share")
    d_gates = grads.pop("gates").reshape(2, H, Dh // N_CHIPS, Dh)
    grads["lru_wa"], grads["lru_wx"] = d_gates[0].reshape(lru_wa.shape), d_gates[1].reshape(lru_wx.shape)
    small_full = dict(g_mix=dg_mix, sg_ln_g=d_lg, sg_ln_b=d_lb, sg_w=d_wm, sg_b=d_bt.T, lru_conv_w=d_lru_cw,
                      lru_conv_b=d_lru_cb, lru_ba=d_ba, lru_bx=d_bx, lru_lam=d_lam,
                      g_ffn=dg_ffn, ffn_conv_w=d_ffn_cw, ffn_conv_b=d_ffn_cb, g_final=dg_final)
    reduced = list(small_full)
    unit_s = SUBLANES * LANES
    pack_s = 512 * LANES
    red = _all_reduce(_pack([small_full[n] for n in reduced], unit_s, pack_s), pos, tag="small")
    red = dict(zip(reduced, _unpack(red, [small_full[n].shape for n in reduced], unit_s)))
    cs_lru = W_lru // N_CHIPS
    cs_ffn = F2 // N_CHIPS
    red["lru_conv_w"] = lax.dynamic_slice_in_dim(red["lru_conv_w"], chip * cs_lru, cs_lru, axis=1)
    red["ffn_conv_w"] = lax.dynamic_slice_in_dim(red["ffn_conv_w"], chip * cs_ffn, cs_ffn, axis=1)
    for n in reduced:
        grads[n] = red[n].reshape(params[n].shape)

    delta, new_m, new_v = {}, {}, {}

    def adam_big(n, carry=None):
        shp = params[n].shape
        two_d = (-1, shp[-1])
        outs = _adamw(params[n].reshape(two_d), grads[n], mom1[n].reshape(two_d), mom2[n].reshape(two_d),
                      name=f"adamw_{n}", pass_grad=True, carry=carry)
        (d, nm, nv, gr), carried = outs if carry is not None else (outs, [])
        delta[n], new_m[n], new_v[n], grads[n] = d.reshape(shp), nm.reshape(shp), nv.reshape(shp), gr.reshape(shp)
        return carried

    qs["w_in"], = adam_big("w_up", _Carry(_ChipExchange(ps["w_in"], q_in, 6, 8, 8)))
    for n in ["w_down", "p_sg", "p_lru", "w_out"]:
        adam_big(n)
    grads.update(reduced_shards(["w_in"], "rs_share_w_in"))
    adam_big("w_in")
    packs = [_pack([src[n] for n in small], unit_s, pack_s) for src in (params, grads, mom1, mom2)]
    outs = _adamw(*packs, name="adamw_small")
    shapes = [params[n].shape for n in small]
    for dst, packed in zip((delta, new_m, new_v), outs):
        dst.update(dict(zip(small, _unpack(packed, shapes, unit_s))))

    return (loss, dx.reshape(x.shape), *[grads[n] for n in names], *[delta[n] for n in names],
            *[new_m[n] for n in names], *[new_v[n] for n in names])
```

```python
import math

import jax
import jax.numpy as jnp
from jax import lax
from jax.experimental import pallas as pl
from jax.experimental.pallas import tpu as pltpu

F32 = jnp.float32
BF16 = jnp.bfloat16
MESH = pl.DeviceIdType.MESH
ANY = pl.BlockSpec(memory_space=pl.ANY)

EPS = 1e-6
LRU_C = 8.0
ADAM_LR = 0.001
ADAM_B1 = 0.9
ADAM_B2 = 0.999
ADAM_EPS = 1e-08
ADAM_WD = 0.01
ADAM_STEP = 10

N_CHIPS = 4
SUBLANES = 8
BF16_ROWS = 16
LANES = 128
VMEM_LIMIT = 56 * 1024 * 1024
DPROJ_PIECES = 5
DPROJ_SHIFT = 4
GELU_C = math.sqrt(2.0 / math.pi)
GELU_K = 0.044715


def _cp(*sem):
    return pltpu.CompilerParams(dimension_semantics=sem, vmem_limit_bytes=VMEM_LIMIT)


def _blk(dim, pref):
    if dim <= pref:
        return dim
    for b in range(pref, 0, -LANES):
        if dim % b == 0:
            return b
    b = pref
    while dim % b:
        b //= 2
    return b


def _gelu(x):
    t = jnp.tanh(GELU_C * (x + GELU_K * x * x * x))
    return 0.5 * x * (1.0 + t)


def _gelu_and_grad(x):
    x2 = x * x
    t = jnp.tanh(GELU_C * (x + GELU_K * x * x2))
    g = 0.5 * x * (1.0 + t)
    dg = 0.5 * (1.0 + t) + 0.5 * x * (1.0 - t * t) * (GELU_C * (1.0 + 3.0 * GELU_K * x2))
    return g, dg


def _sigmoid(x):
    return 1.0 / (1.0 + jnp.exp(-x))


def _softplus(x):
    e = jnp.exp(-jnp.abs(x))
    series = e * (1.0 - e * (0.5 - e * (1.0 / 3.0 - e * (0.25 - e * 0.2))))
    return jnp.where(e < 0.01, series, jnp.log(1.0 + e)) + jnp.maximum(x, 0.0)


def _neg_expm1(x):
    series = -(x * (1.0 + x * (0.5 + x * (1.0 / 6.0 + x * (1.0 / 24.0)))))
    return jnp.where(x > -0.01, series, 1.0 - jnp.exp(x))


def _mesh_pos():
    return lax.axis_index("x"), lax.axis_index("y"), lax.axis_index("c")


def _other_chips(x, y):
    return [(1 - x, y), (x, 1 - y), (1 - x, 1 - y)]


def _remote(k, src, dst, to, send_sems, recv_sems):
    return pltpu.make_async_remote_copy(src_ref=src, dst_ref=dst, send_sem=send_sems.at[k],
                                        recv_sem=recv_sems.at[k], device_id=to, device_id_type=MESH)


class _GatherRows:
    n_sems = 6

    def __init__(self, buf, r0, r1):
        self.args = [buf]
        self.out_shape = [jax.ShapeDtypeStruct(buf.shape, buf.dtype)]
        self.aliases = {0: 0}
        self.r0, self.h = r0, (r1 - r0) // 2

    def _rows(self, half):
        return pl.ds(self.r0 + half * self.h, self.h)

    def start(self, ins, outs, ss, rs, base):
        x, y, c = _mesh_pos()
        mine = ins[0].at[2 * x + y, self._rows(c), :]
        for j, (px, py) in enumerate(_other_chips(x, y)):
            _remote(base + j, mine, outs[0].at[2 * x + y, self._rows(c), :], (px, py, c), ss, rs).start()

    def finish(self, ins, outs, ss, rs, base):
        x, y, c = _mesh_pos()
        sibling = (x, y, 1 - c)
        chips = _other_chips(x, y)
        mine = ins[0].at[2 * x + y, self._rows(c), :]
        for j, (px, py) in enumerate(chips):
            got = outs[0].at[2 * px + py, self._rows(c), :]
            _remote(base + j, got, got, (px, py, c), ss, rs).wait_recv()
            _remote(base + 3 + j, got, got, sibling, ss, rs).start()
        for j, (px, py) in enumerate(chips):
            fwd = outs[0].at[2 * px + py, self._rows(1 - c), :]
            _remote(base + 3 + j, fwd, fwd, sibling, ss, rs).wait_recv()
        for j, (px, py) in enumerate(chips):
            got = outs[0].at[2 * px + py, self._rows(c), :]
            _remote(base + j, mine, mine, (px, py, c), ss, rs).wait_send()
            _remote(base + 3 + j, got, got, sibling, ss, rs).wait_send()


class _GatherIci(_GatherRows):
    n_sems = 3

    def finish(self, ins, outs, ss, rs, base):
        x, y, c = _mesh_pos()
        mine = ins[0].at[2 * x + y, self._rows(c), :]
        for j, (px, py) in enumerate(_other_chips(x, y)):
            got = outs[0].at[2 * px + py, self._rows(c), :]
            _remote(base + j, got, got, (px, py, c), ss, rs).wait_recv()
            _remote(base + j, mine, mine, (px, py, c), ss, rs).wait_send()


class _GatherFwd(_GatherRows):
    n_sems = 3

    def start(self, ins, outs, ss, rs, base):
        x, y, c = _mesh_pos()
        for j, (px, py) in enumerate(_other_chips(x, y)):
            _remote(base + j, ins[0].at[2 * px + py, self._rows(c), :], outs[0].at[2 * px + py, self._rows(c), :],
                    (x, y, 1 - c), ss, rs).start()

    def finish(self, ins, outs, ss, rs, base):
        x, y, c = _mesh_pos()
        for j, (px, py) in enumerate(_other_chips(x, y)):
            got = ins[0].at[2 * px + py, self._rows(c), :]
            fwd = outs[0].at[2 * px + py, self._rows(1 - c), :]
            _remote(base + j, got, got, (x, y, 1 - c), ss, rs).wait_send()
            _remote(base + j, fwd, fwd, (x, y, 1 - c), ss, rs).wait_recv()


class _ChipExchange:
    n_sems = 3

    def __init__(self, p, q=None, lo=0, hi=1, of=1):
        hr = p.shape[1]
        self.r0, self.n = hr * lo // of, hr * (hi - lo) // of
        self.args = [p] if q is None else [p, q]
        self.out_shape = [jax.ShapeDtypeStruct((N_CHIPS - 1,) + p.shape[1:], p.dtype)]
        self.aliases = {} if q is None else {1: 0}

    def _copies(self, ins, outs, ss, rs, base):
        x, y, c = _mesh_pos()
        rows = pl.ds(self.r0, self.n)
        return [_remote(base + j, ins[0].at[2 * px + py, rows, :], outs[0].at[j, rows, :], (px, py, c), ss, rs)
                for j, (px, py) in enumerate(_other_chips(x, y))]

    def start(self, ins, outs, ss, rs, base):
        for cp in self._copies(ins, outs, ss, rs, base):
            cp.start()

    def finish(self, ins, outs, ss, rs, base):
        for cp in self._copies(ins, outs, ss, rs, base):
            cp.wait()


class _PairSwap:
    n_sems = 1

    def __init__(self, g):
        n, R, C = g.shape
        self.args = [g]
        self.out_shape = [jax.ShapeDtypeStruct((n, R // 2, C), g.dtype)]
        self.aliases = {}
        self.hr = R // 2

    def _copy(self, ins, outs, ss, rs, base):
        x, y, c = _mesh_pos()
        return _remote(base, ins[0].at[:, pl.ds((1 - c) * self.hr, self.hr), :], outs[0], (x, y, 1 - c), ss, rs)

    def start(self, ins, outs, ss, rs, base):
        self._copy(ins, outs, ss, rs, base).start()

    def finish(self, ins, outs, ss, rs, base):
        self._copy(ins, outs, ss, rs, base).wait()


class _ShareHalves:
    n_sems = 1

    def __init__(self, buf):
        self.args = [buf]
        self.out_shape = [jax.ShapeDtypeStruct(buf.shape, buf.dtype)]
        self.aliases = {0: 0}
        self.hr = buf.shape[0] // 2

    def start(self, ins, outs, ss, rs, base):
        x, y, c = _mesh_pos()
        rows = pl.ds(c * self.hr, self.hr)
        _remote(base, ins[0].at[rows, :], outs[0].at[rows, :], (x, y, 1 - c), ss, rs).start()

    def finish(self, ins, outs, ss, rs, base):
        x, y, c = _mesh_pos()
        mine = ins[0].at[pl.ds(c * self.hr, self.hr), :]
        theirs = outs[0].at[pl.ds((1 - c) * self.hr, self.hr), :]
        _remote(base, mine, mine, (x, y, 1 - c), ss, rs).wait_send()
        _remote(base, theirs, theirs, (x, y, 1 - c), ss, rs).wait_recv()


class _GatherSlabs:
    n_sems = 3

    def __init__(self, buf):
        self.args = [buf]
        self.out_shape = [jax.ShapeDtypeStruct(buf.shape, buf.dtype)]
        self.aliases = {0: 0}

    def start(self, ins, outs, ss, rs, base):
        x, y, c = _mesh_pos()
        for j, (px, py) in enumerate(_other_chips(x, y)):
            _remote(base + j, ins[0].at[2 * x + y], outs[0].at[2 * x + y], (px, py, c), ss, rs).start()

    def finish(self, ins, outs, ss, rs, base):
        x, y, c = _mesh_pos()
        mine = ins[0].at[2 * x + y]
        for j, (px, py) in enumerate(_other_chips(x, y)):
            got = outs[0].at[2 * px + py]
            _remote(base + j, mine, mine, (px, py, c), ss, rs).wait_send()
            _remote(base + j, got, got, (px, py, c), ss, rs).wait_recv()


class _Carry:
    def __init__(self, *items):
        self.items = items
        self.args, self.out_shape, self._alias, self._slots = [], [], {}, []
        seen = {}
        for it in items:
            in_idx, out_idx = [], [None] * len(it.out_shape)
            for ai, a in enumerate(it.args):
                aliased = ai in it.aliases
                if aliased and id(a) in seen:
                    i, o = seen[id(a)]
                else:
                    i, o = len(self.args), None
                    self.args.append(a)
                    if aliased:
                        o = len(self.out_shape)
                        self.out_shape.append(it.out_shape[it.aliases[ai]])
                        self._alias[i] = o
                        seen[id(a)] = (i, o)
                in_idx.append(i)
                if aliased:
                    out_idx[it.aliases[ai]] = o
            for oi, shape in enumerate(it.out_shape):
                if out_idx[oi] is None:
                    out_idx[oi] = len(self.out_shape)
                    self.out_shape.append(shape)
            self._slots.append((in_idx, out_idx))
        self.n_sems = sum(it.n_sems for it in items)

    def aliases(self, in_base, out_base):
        return {in_base + i: out_base + o for i, o in self._alias.items()}

    def _each(self, method, ins, outs, ss, rs):
        base = 0
        for it, (in_idx, out_idx) in zip(self.items, self._slots):
            getattr(it, method)([ins[i] for i in in_idx], [outs[o] for o in out_idx], ss, rs, base)
            base += it.n_sems

    def start(self, ins, outs, ss, rs):
        self._each("start", ins, outs, ss, rs)

    def finish(self, ins, outs, ss, rs):
        self._each("finish", ins, outs, ss, rs)


def _carried_call(body, *, name, grid, in_specs, out_specs, out_shape, scratch_shapes, args, semantics, carry=None,
                  aliases=None):
    aliases = aliases or {}
    if carry is None:
        outs = pl.pallas_call(body, name=name, grid=grid, in_specs=in_specs, out_specs=out_specs,
                              out_shape=out_shape, scratch_shapes=scratch_shapes, input_output_aliases=aliases,
                              compiler_params=_cp(*semantics))(*args)
        return list(outs), []
    n_in, n_out, n_scr = len(in_specs), len(out_specs), len(scratch_shapes)
    n_cin, n_cout = len(carry.args), len(carry.out_shape)

    def full(*refs):
        ins = refs[:n_in]
        cins = refs[n_in:n_in + n_cin]
        outs = refs[n_in + n_cin:n_in + n_cin + n_out]
        couts = refs[n_in + n_cin + n_out:n_in + n_cin + n_out + n_cout]
        scr = refs[n_in + n_cin + n_out + n_cout:n_in + n_cin + n_out + n_cout + n_scr]
        ss, rs = refs[-2], refs[-1]
        first = pl.program_id(0) == 0
        last = pl.program_id(0) == grid[0] - 1
        for d in range(1, len(grid)):
            first = jnp.logical_and(first, pl.program_id(d) == 0)
            last = jnp.logical_and(last, pl.program_id(d) == grid[d] - 1)

        @pl.when(first)
        def _():
            carry.start(cins, couts, ss, rs)

        body(*ins, *outs, *scr)

        @pl.when(last)
        def _():
            carry.finish(cins, couts, ss, rs)

    outs = pl.pallas_call(
        full, name=name, grid=grid, in_specs=list(in_specs) + [ANY] * n_cin,
        out_specs=list(out_specs) + [ANY] * n_cout, out_shape=list(out_shape) + carry.out_shape,
        scratch_shapes=list(scratch_shapes) + [pltpu.SemaphoreType.DMA((carry.n_sems,))] * 2,
        input_output_aliases={**aliases, **carry.aliases(n_in, n_out)},
        compiler_params=_cp(*(("arbitrary",) * len(grid))),
    )(*args, *carry.args)
    return list(outs[:n_out]), list(outs[n_out:])


def _comm_call(carry, *, name):
    n_cin = len(carry.args)

    def body(*refs):
        cins, couts = refs[:n_cin], refs[n_cin:-2]
        carry.start(cins, couts, refs[-2], refs[-1])
        carry.finish(cins, couts, refs[-2], refs[-1])

    outs = pl.pallas_call(
        body, name=name, in_specs=[ANY] * n_cin, out_specs=[ANY] * len(carry.out_shape), out_shape=carry.out_shape,
        scratch_shapes=[pltpu.SemaphoreType.DMA((carry.n_sems,))] * 2,
        input_output_aliases=carry.aliases(0, 0),
    )(*carry.args)
    return list(outs)


def _mm_nn(a, b, *, out_dtype, name, res=None, carry=None):
    M, K = a.shape
    cs = b.shape[-1]
    N = cs * (b.shape[0] if b.ndim == 3 else 1)
    tm, tn, tk = _blk(M, 1024 if res is None else 512), _blk(cs, 1024), _blk(K, 4096)
    nbs, nk = cs // tn, K // tk
    if b.ndim == 3:
        b_spec = pl.BlockSpec((None, tk, tn), lambda i, j, k: (j // nbs, k, j % nbs))
    else:
        b_spec = pl.BlockSpec((tk, tn), lambda i, j, k: (k, j))
    in_specs = [pl.BlockSpec((tm, tk), lambda i, j, k: (i, k)), b_spec]
    args = [a, b]
    if res is not None:
        in_specs.append(pl.BlockSpec((tm, tn), lambda i, j, k: (i, j)))
        args.append(res)

    def body(*refs):
        a_ref, b_ref = refs[0], refs[1]
        r_ref = refs[2] if res is not None else None
        p = jnp.dot(a_ref[...], b_ref[...], preferred_element_type=F32)
        if nk == 1:
            o_ref = refs[-1]
            o_ref[...] = (p if r_ref is None else p + r_ref[...]).astype(out_dtype)
            return
        o_ref, acc = refs[-2], refs[-1]
        k = pl.program_id(2)

        @pl.when(k == 0)
        def _():
            acc[...] = p

        @pl.when(k > 0)
        def _():
            acc[...] += p

        @pl.when(k == nk - 1)
        def _():
            r = acc[...]
            if r_ref is not None:
                r = r + r_ref[...]
            o_ref[...] = r.astype(out_dtype)

    outs, carried = _carried_call(
        body, name=name, grid=(M // tm, N // tn, nk), in_specs=in_specs,
        out_specs=[pl.BlockSpec((tm, tn), lambda i, j, k: (i, j))],
        out_shape=[jax.ShapeDtypeStruct((M, N), out_dtype)],
        scratch_shapes=[pltpu.VMEM((tm, tn), F32)] if nk > 1 else [], args=args,
        semantics=("parallel", "parallel", "arbitrary"), carry=carry)
    return outs[0] if carry is None else (outs[0], carried)


def _mm_nt(a, b, *, out_dtype, name, carry=None, a_shift=0):
    M = a.shape[-2]
    wp = a.shape[-1]
    Kc = wp * (a.shape[0] if a.ndim == 3 else 1)
    cs = b.shape[-1]
    N = b.shape[-2]
    tm, tn, tk = _blk(M, 1024), _blk(N, 1024), _blk(math.gcd(cs, wp), 4096)
    kb = wp // tk if (a.ndim == 3 and b.ndim == 3 and tk < 2048) else 1
    nks, nka, nk = cs // tk, wp // (kb * tk), Kc // (kb * tk)
    if b.ndim == 3:
        b_specs = [pl.BlockSpec((None, tn, tk), lambda i, j, k, r=r: ((k * kb + r) // nks, j, (k * kb + r) % nks))
                   for r in range(kb)]
    else:
        b_specs = [pl.BlockSpec((tn, tk), lambda i, j, k: (j, k))]
    if a.ndim == 3:
        n_pieces = a.shape[0]
        a_spec = pl.BlockSpec((None, tm, kb * tk), lambda i, j, k: ((k // nka + a_shift) % n_pieces, i, k % nka))
    else:
        a_spec = pl.BlockSpec((tm, tk), lambda i, j, k: (i, k))

    def body(a_ref, *refs):
        b_refs, o_ref, scr = refs[:kb], refs[kb], refs[kb + 1:]
        p = None
        for r in range(kb):
            a_blk = a_ref[...] if kb == 1 else a_ref[:, r * tk:(r + 1) * tk]
            t = lax.dot_general(a_blk, b_refs[r][...], (((1,), (1,)), ((), ())), preferred_element_type=F32)
            p = t if p is None else p + t
        if nk == 1:
            o_ref[...] = p.astype(out_dtype)
            return
        acc = scr[0]
        k = pl.program_id(2)

        @pl.when(k == 0)
        def _():
            acc[...] = p

        @pl.when(k > 0)
        def _():
            acc[...] += p

        @pl.when(k == nk - 1)
        def _():
            o_ref[...] = acc[...].astype(out_dtype)

    outs, carried = _carried_call(
        body, name=name, grid=(M // tm, N // tn, nk),
        in_specs=[a_spec] + b_specs,
        out_specs=[pl.BlockSpec((tm, tn), lambda i, j, k: (i, j))],
        out_shape=[jax.ShapeDtypeStruct((M, N), out_dtype)],
        scratch_shapes=[pltpu.VMEM((tm, tn), F32)] if nk > 1 else [], args=[a] + [b] * kb,
        semantics=("parallel", "parallel", "arbitrary"), carry=carry)
    return outs[0] if carry is None else (outs[0], carried)


def _mm_tn(a, b, *, out_dtype, name, col_shards=None, carry=None, b_shift=0):
    T, K1 = a.shape
    wp = b.shape[-1]
    N = wp * (b.shape[0] if b.ndim == 3 else 1)
    cs = N // col_shards if col_shards else N
    tm, tn, tk = _blk(K1, 1024), _blk(math.gcd(cs, wp), 1024), _blk(T, 4096)
    nbs, njb, nk = cs // tn, wp // tn, T // tk
    if b.ndim == 3:
        n_pieces = b.shape[0]
        b_spec = pl.BlockSpec((None, tk, tn), lambda i, j, k: ((j // njb + b_shift) % n_pieces, k, j % njb))
    else:
        b_spec = pl.BlockSpec((tk, tn), lambda i, j, k: (k, j))
    if col_shards:
        o_spec = pl.BlockSpec((None, tm, tn), lambda i, j, k: (j // nbs, i, j % nbs))
        o_shape = jax.ShapeDtypeStruct((col_shards, K1, cs), out_dtype)
    else:
        o_spec = pl.BlockSpec((tm, tn), lambda i, j, k: (i, j))
        o_shape = jax.ShapeDtypeStruct((K1, N), out_dtype)

    def body(a_ref, b_ref, o_ref, *scr):
        p = lax.dot_general(a_ref[...], b_ref[...], (((0,), (0,)), ((), ())), preferred_element_type=F32)
        if nk == 1:
            o_ref[...] = p.astype(out_dtype)
            return
        acc = scr[0]
        k = pl.program_id(2)

        @pl.when(k == 0)
        def _():
            acc[...] = p

        @pl.when(k > 0)
        def _():
            acc[...] += p

        @pl.when(k == nk - 1)
        def _():
            o_ref[...] = acc[...].astype(out_dtype)

    outs, carried = _carried_call(
        body, name=name, grid=(K1 // tm, N // tn, nk),
        in_specs=[pl.BlockSpec((tk, tm), lambda i, j, k: (k, i)), b_spec],
        out_specs=[o_spec], out_shape=[o_shape],
        scratch_shapes=[pltpu.VMEM((tm, tn), F32)] if nk > 1 else [], args=[a, b],
        semantics=("parallel", "parallel", "arbitrary"), carry=carry)
    return outs[0] if carry is None else (outs[0], carried)


def _rms_fwd(x, g, *, name, carry=None):
    T, D = x.shape
    tm = _blk(T, 256)

    def body(x_ref, g_ref, o_ref):
        xv = x_ref[...]
        r = lax.rsqrt(jnp.mean(xv * xv, axis=-1, keepdims=True) + EPS)
        o_ref[...] = (xv * r * g_ref[...]).astype(BF16)

    outs, carried = _carried_call(
        body, name=name, grid=(T // tm,),
        in_specs=[pl.BlockSpec((tm, D), lambda i: (i, 0)), pl.BlockSpec((1, D), lambda i: (0, 0))],
        out_specs=[pl.BlockSpec((tm, D), lambda i: (i, 0))],
        out_shape=[jax.ShapeDtypeStruct((T, D), BF16)], scratch_shapes=[], args=[x, g],
        semantics=("parallel",), carry=carry)
    return outs[0] if carry is None else (outs[0], carried)


def _rms_bwd(x, g, dh, dres, *, name):
    T, D = x.shape
    tm = _blk(T, 256)

    def body(x_ref, g_ref, dh_ref, dres_ref, dx_ref, dxb_ref, dg_ref):
        i = pl.program_id(0)
        xv = x_ref[...]
        r = lax.rsqrt(jnp.mean(xv * xv, axis=-1, keepdims=True) + EPS)
        n = xv * r
        dh_v = dh_ref[...]
        dn = dh_v * g_ref[...]
        dx = dres_ref[...] + r * (dn - n * jnp.mean(dn * n, axis=-1, keepdims=True))
        dx_ref[...] = dx
        dxb_ref[...] = dx.astype(BF16)
        part = jnp.sum(dh_v * n, axis=0, keepdims=True)

        @pl.when(i == 0)
        def _():
            dg_ref[...] = part

        @pl.when(i > 0)
        def _():
            dg_ref[...] += part

    row = pl.BlockSpec((tm, D), lambda i: (i, 0))
    vec = pl.BlockSpec((1, D), lambda i: (0, 0))
    return pl.pallas_call(
        body, name=name, grid=(T // tm,),
        in_specs=[row, vec, row, row], out_specs=[row, row, vec],
        out_shape=[jax.ShapeDtypeStruct((T, D), F32), jax.ShapeDtypeStruct((T, D), BF16),
                   jax.ShapeDtypeStruct((1, D), F32)],
        compiler_params=_cp("arbitrary"),
    )(x, g, dh, dres)


def _loss_head(x2, tgt, g, *, name):
    T, D = x2.shape
    tm = _blk(T, 256)

    def body(x_ref, t_ref, g_ref, l_ref, dx_ref, dxb_ref, dg_ref):
        i = pl.program_id(0)
        xv = x_ref[...]
        gv = g_ref[...]
        r = lax.rsqrt(jnp.mean(xv * xv, axis=-1, keepdims=True) + EPS)
        n = xv * r
        err = n * gv - t_ref[...]
        dy = err * (1.0 / D)
        dn = dy * gv
        dx = r * (dn - n * jnp.mean(dn * n, axis=-1, keepdims=True))
        dx_ref[...] = dx
        dxb_ref[...] = dx.astype(BF16)
        lpart = jnp.sum(err * err, axis=0, keepdims=True)
        gpart = jnp.sum(dy * n, axis=0, keepdims=True)

        @pl.when(i == 0)
        def _():
            l_ref[...] = lpart
            dg_ref[...] = gpart

        @pl.when(i > 0)
        def _():
            l_ref[...] += lpart
            dg_ref[...] += gpart

    row = pl.BlockSpec((tm, D), lambda i: (i, 0))
    vec = pl.BlockSpec((1, D), lambda i: (0, 0))
    return pl.pallas_call(
        body, name=name, grid=(T // tm,),
        in_specs=[row, row, vec], out_specs=[vec, row, row, vec],
        out_shape=[jax.ShapeDtypeStruct((1, D), F32), jax.ShapeDtypeStruct((T, D), F32),
                   jax.ShapeDtypeStruct((T, D), BF16), jax.ShapeDtypeStruct((1, D), F32)],
        compiler_params=_cp("arbitrary"),
    )(x2, tgt, g)


def _merge_fwd(proj, pa, pb, *, off_a, name, carry=None):
    T, D = pa.shape
    tm, tn = _blk(T, 256), _blk(D, 1024)
    oa, ob = off_a // tn, (off_a + D) // tn

    def body(ga_ref, gb_ref, pa_ref, pb_ref, o_ref):
        o_ref[...] = (_sigmoid(ga_ref[...]) * pa_ref[...] + _sigmoid(gb_ref[...]) * pb_ref[...]).astype(BF16)

    blk = pl.BlockSpec((tm, tn), lambda i, j: (i, j))
    outs, carried = _carried_call(
        body, name=name, grid=(T // tm, D // tn),
        in_specs=[pl.BlockSpec((tm, tn), lambda i, j: (i, oa + j)),
                  pl.BlockSpec((tm, tn), lambda i, j: (i, ob + j)), blk, blk],
        out_specs=[blk], out_shape=[jax.ShapeDtypeStruct((T, D), BF16)], scratch_shapes=[],
        args=[proj, proj, pa, pb], semantics=("parallel", "parallel"), carry=carry)
    return outs[0] if carry is None else (outs[0], carried)


def _merge_bwd(proj, pa, pb, dm, *, off_a, name):
    T, D = pa.shape
    tm, tn = _blk(T, 256), _blk(D, 1024)
    oa, ob = off_a // tn, (off_a + D) // tn

    def body(ga_ref, gb_ref, pa_ref, pb_ref, dm_ref, dg_ref, dpa_ref, dpb_ref):
        dmv = dm_ref[...]
        sa = _sigmoid(ga_ref[...])
        sb = _sigmoid(gb_ref[...])
        dg_ref[0] = (dmv * pa_ref[...] * sa * (1.0 - sa)).astype(BF16)
        dg_ref[1] = (dmv * pb_ref[...] * sb * (1.0 - sb)).astype(BF16)
        dpa_ref[...] = (dmv * sa).astype(BF16)
        dpb_ref[...] = (dmv * sb).astype(BF16)

    blk = pl.BlockSpec((tm, tn), lambda i, j: (i, j))
    out = jax.ShapeDtypeStruct((T, D), BF16)
    return pl.pallas_call(
        body, name=name, grid=(T // tm, D // tn),
        in_specs=[pl.BlockSpec((tm, tn), lambda i, j: (i, oa + j)),
                  pl.BlockSpec((tm, tn), lambda i, j: (i, ob + j)), blk, blk, blk],
        out_specs=[pl.BlockSpec((2, tm, tn), lambda i, j: (1, i, j)), blk, blk],
        out_shape=[jax.ShapeDtypeStruct((DPROJ_PIECES, T, D), BF16), out, out],
        compiler_params=_cp("parallel", "parallel"),
    )(proj, proj, pa, pb, dm)


def _sg_mask(sg_w, *, name):
    G, C, _ = sg_w.shape

    def body(w_ref, m_ref, mt_ref):
        row = lax.broadcasted_iota(jnp.int32, (C, C), 0)
        col = lax.broadcasted_iota(jnp.int32, (C, C), 1)
        for g in range(G):
            w = jnp.where(row >= col, w_ref[g], 0.0)
            m_ref[g] = w.astype(BF16)
            mt_ref[g] = w.T.astype(BF16)

    out = jax.ShapeDtypeStruct((G, C, C), BF16)
    return pl.pallas_call(body, name=name, out_shape=[out, out])(sg_w)


def _sg_layernorm(zv, lg, lb):
    v = _gelu(zv)
    mu = jnp.mean(v, axis=-1, keepdims=True)
    xc = v - mu
    rstd = lax.rsqrt(jnp.mean(xc * xc, axis=-1, keepdims=True) + EPS)
    vhat = xc * rstd
    return vhat, rstd, vhat * lg + lb


def _sg_fwd(proj, lg, lb, wm, bt, *, name, carry=None):
    T = proj.shape[0]
    G, C, _ = wm.shape
    W = lg.shape[-1]
    gd = W // G

    def body(zu_ref, zv_ref, lg_ref, lb_ref, wm_ref, bt_ref, ya_ref, vn_scr):
        _, _, vn = _sg_layernorm(zv_ref[...], lg_ref[...], lb_ref[...])
        vn_scr[...] = vn.astype(BF16)
        for g in range(G):
            cols = slice(g * gd, (g + 1) * gd)
            mixed = jnp.dot(wm_ref[g], vn_scr[:, cols], preferred_element_type=F32) + bt_ref[:, g:g + 1]
            ya_ref[:, cols] = (_gelu(zu_ref[:, cols]) * mixed).astype(BF16)

    vec = pl.BlockSpec((1, W), lambda i: (0, 0))
    outs, carried = _carried_call(
        body, name=name, grid=(T // C,),
        in_specs=[pl.BlockSpec((C, W), lambda i: (i, 0)), pl.BlockSpec((C, W), lambda i: (i, 1)), vec, vec,
                  pl.BlockSpec((G, C, C), lambda i: (0, 0, 0)), pl.BlockSpec((C, G), lambda i: (0, 0))],
        out_specs=[pl.BlockSpec((C, W), lambda i: (i, 0))],
        out_shape=[jax.ShapeDtypeStruct((T, W), BF16)],
        scratch_shapes=[pltpu.VMEM((C, W), BF16)], args=[proj, proj, lg, lb, wm, bt],
        semantics=("parallel",), carry=carry)
    return outs[0] if carry is None else (outs[0], carried)


def _sg_bwd(proj, dya, dproj, lg, lb, wm, wmt, bt, *, name):
    T = proj.shape[0]
    G, C, _ = wm.shape
    W = lg.shape[-1]
    gd = W // G
    n_steps = T // C

    def body(zu_ref, zv_ref, dya_ref, lg_ref, lb_ref, wm_ref, wmt_ref, bt_ref, dproj_in_ref,
             dz_ref, dwm_ref, dbt_ref, dlg_ref, dlb_ref, vn_scr, dvn_scr):
        i = pl.program_id(0)

        @pl.when(i == 0)
        def _():
            dwm_ref[...] = jnp.zeros_like(dwm_ref)
            dbt_ref[...] = jnp.zeros_like(dbt_ref)
            dlg_ref[...] = jnp.zeros_like(dlg_ref)
            dlb_ref[...] = jnp.zeros_like(dlb_ref)

        lgv = lg_ref[...]
        vhat, rstd, vn = _sg_layernorm(zv_ref[...], lgv, lb_ref[...])
        vn_scr[...] = vn.astype(BF16)
        for g in range(G):
            cols = slice(g * gd, (g + 1) * gd)
            vnb = vn_scr[:, cols]
            mixed = jnp.dot(wm_ref[g], vnb, preferred_element_type=F32) + bt_ref[:, g:g + 1]
            u, du = _gelu_and_grad(zu_ref[:, cols])
            dy = dya_ref[:, cols]
            dz_ref[:, cols] = (dy * mixed * du).astype(BF16)
            dmix = dy * u
            dmb = dmix.astype(BF16)
            dbt_ref[:, g:g + 1] += jnp.sum(dmix, axis=1, keepdims=True)
            dwm_ref[g] += lax.dot_general(dmb, vnb, (((1,), (1,)), ((), ())), preferred_element_type=F32)
            dvn_scr[:, cols] = jnp.dot(wmt_ref[g], dmb, preferred_element_type=F32)
        dvn = dvn_scr[...]
        dlg_ref[...] += jnp.sum(dvn * vhat, axis=0, keepdims=True)
        dlb_ref[...] += jnp.sum(dvn, axis=0, keepdims=True)
        dvh = dvn * lgv
        dv = rstd * (dvh - jnp.mean(dvh, axis=-1, keepdims=True)
                     - vhat * jnp.mean(dvh * vhat, axis=-1, keepdims=True))
        _, dgv = _gelu_and_grad(zv_ref[...])
        dz_ref[:, W:] = (dv * dgv).astype(BF16)

        @pl.when(i == n_steps - 1)
        def _():
            row = lax.broadcasted_iota(jnp.int32, (C, C), 0)
            col = lax.broadcasted_iota(jnp.int32, (C, C), 1)
            for g in range(G):
                dwm_ref[g] = jnp.where(row >= col, dwm_ref[g], 0.0)

    vec = pl.BlockSpec((1, W), lambda i: (0, 0))
    mat = pl.BlockSpec((G, C, C), lambda i: (0, 0, 0))
    bts = pl.BlockSpec((C, G), lambda i: (0, 0))
    return pl.pallas_call(
        body, name=name, grid=(n_steps,),
        in_specs=[pl.BlockSpec((C, W), lambda i: (i, 0)), pl.BlockSpec((C, W), lambda i: (i, 1)),
                  pl.BlockSpec((C, W), lambda i: (i, 0)), vec, vec, mat, mat, bts, ANY],
        out_specs=[pl.BlockSpec((None, C, 2 * W), lambda i: (DPROJ_PIECES - 1, i, 0)), mat, bts, vec, vec],
        out_shape=[jax.ShapeDtypeStruct(dproj.shape, dproj.dtype), jax.ShapeDtypeStruct((G, C, C), F32),
                   jax.ShapeDtypeStruct((C, G), F32), jax.ShapeDtypeStruct((1, W), F32),
                   jax.ShapeDtypeStruct((1, W), F32)],
        scratch_shapes=[pltpu.VMEM((C, W), BF16), pltpu.VMEM((C, W), F32)], input_output_aliases={8: 0},
        compiler_params=_cp("arbitrary"),
    )(proj, proj, dya, lg, lb, wm, wmt, bt, dproj)


def _rows_with_prev(ref, r0, rows, ci):
    p0 = pl.multiple_of(jnp.maximum(r0 - SUBLANES, 0), SUBLANES)
    prev = jnp.where(ci > 0, ref[pl.ds(p0, SUBLANES), :], 0.0)
    return jnp.concatenate([prev, ref[pl.ds(r0, rows), :]], axis=0)


def _rows_with_next(ref, r0, rows, ci, n_chunks, total):
    n0 = pl.multiple_of(jnp.minimum(r0 + rows, total - SUBLANES), SUBLANES)
    nxt = jnp.where(ci < n_chunks - 1, ref[pl.ds(n0, SUBLANES), :], 0.0)
    return jnp.concatenate([ref[pl.ds(r0, rows), :], nxt], axis=0)


def _delayed(xx, k, rows):
    if k == 0:
        return xx[SUBLANES:, :]
    return pltpu.roll(xx, k, 0)[SUBLANES:, :]


def _advanced(xx, k, rows):
    if k == 0:
        return xx[:rows, :]
    return pltpu.roll(xx, rows + SUBLANES - k, 0)[:rows, :]


def _conv_chunk(x_ref, w_ref, b_ref, r0, rows, ci):
    K = w_ref.shape[0]
    xx = _rows_with_prev(x_ref, r0, rows, ci)
    out = _delayed(xx, K - 1, rows) * w_ref[0:1, :]
    for k in range(1, K):
        out = out + _delayed(xx, K - 1 - k, rows) * w_ref[k:k + 1, :]
    return out + b_ref[...]


def _ffn_act_fwd(up0, cw, cb, *, batch, name, carry=None):
    T, F2 = up0.shape
    F = F2 // 2
    S = T // batch
    K = cw.shape[0]
    cbk = _blk(F, 512)
    nj = F // cbk
    R = min(64, S // 2)
    n_chunks = S // R

    def body(ug_ref, uv_ref, wg_ref, wv_ref, bg_ref, bv_ref, act_ref):
        def chunk(ci, carry):
            r0 = pl.multiple_of(ci * R, R)
            cg = _conv_chunk(ug_ref, wg_ref, bg_ref, r0, R, ci)
            cv = _conv_chunk(uv_ref, wv_ref, bv_ref, r0, R, ci)
            act_ref[pl.ds(r0, R), :] = (_gelu(cg) * cv).astype(BF16)
            return carry

        lax.fori_loop(0, n_chunks, chunk, 0)

    outs, carried = _carried_call(
        body, name=name, grid=(nj, batch),
        in_specs=[pl.BlockSpec((S, cbk), lambda j, b: (b, j)), pl.BlockSpec((S, cbk), lambda j, b: (b, nj + j)),
                  pl.BlockSpec((K, cbk), lambda j, b: (0, j)), pl.BlockSpec((K, cbk), lambda j, b: (0, nj + j)),
                  pl.BlockSpec((1, cbk), lambda j, b: (0, j)), pl.BlockSpec((1, cbk), lambda j, b: (0, nj + j))],
        out_specs=[pl.BlockSpec((S, cbk), lambda j, b: (b, j))],
        out_shape=[jax.ShapeDtypeStruct((T, F), BF16)], scratch_shapes=[],
        args=[up0, up0, cw, cw, cb, cb], semantics=("parallel", "parallel"), carry=carry)
    return outs[0] if carry is None else (outs[0], carried)


def _ffn_act_bwd(up0, cw, cb, dact, *, batch, name):
    T, F2 = up0.shape
    F = F2 // 2
    S = T // batch
    K = cw.shape[0]
    cbk = _blk(F, 512)
    nj = F // cbk
    R = min(64, S // 2)
    n_chunks = S // R

    def body(ug_ref, uv_ref, wg_ref, wv_ref, bg_ref, bv_ref, da_ref,
             du_ref, dw_g_ref, dw_v_ref, db_g_ref, db_v_ref, dcg_scr, dcv_scr):
        b = pl.program_id(1)

        def chunk_a(ci, acc):
            r0 = pl.multiple_of(ci * R, R)
            xg = _rows_with_prev(ug_ref, r0, R, ci)
            xv = _rows_with_prev(uv_ref, r0, R, ci)
            dg_taps = [_delayed(xg, K - 1 - k, R) for k in range(K)]
            dv_taps = [_delayed(xv, K - 1 - k, R) for k in range(K)]
            cg = dg_taps[0] * wg_ref[0:1, :]
            cv = dv_taps[0] * wv_ref[0:1, :]
            for k in range(1, K):
                cg = cg + dg_taps[k] * wg_ref[k:k + 1, :]
                cv = cv + dv_taps[k] * wv_ref[k:k + 1, :]
            cg = cg + bg_ref[...]
            cv = cv + bv_ref[...]
            gl, dgl = _gelu_and_grad(cg)
            da = da_ref[pl.ds(r0, R), :]
            dcg = da * cv * dgl
            dcv = da * gl
            dcg_scr[pl.ds(r0, R), :] = dcg
            dcv_scr[pl.ds(r0, R), :] = dcv
            new = []
            for k in range(K):
                new.append(acc[k] + jnp.sum(dcg * dg_taps[k], axis=0, keepdims=True))
            for k in range(K):
                new.append(acc[K + k] + jnp.sum(dcv * dv_taps[k], axis=0, keepdims=True))
            new.append(acc[2 * K] + jnp.sum(dcg, axis=0, keepdims=True))
            new.append(acc[2 * K + 1] + jnp.sum(dcv, axis=0, keepdims=True))
            return tuple(new)

        zero = jnp.zeros((1, cbk), F32)
        acc = lax.fori_loop(0, n_chunks, chunk_a, (zero,) * (2 * K + 2))

        def chunk_b(ci, carry):
            r0 = pl.multiple_of(ci * R, R)
            dg = _rows_with_next(dcg_scr, r0, R, ci, n_chunks, S)
            dv = _rows_with_next(dcv_scr, r0, R, ci, n_chunks, S)
            og = _advanced(dg, 0, R) * wg_ref[K - 1:K, :]
            ov = _advanced(dv, 0, R) * wv_ref[K - 1:K, :]
            for j in range(1, K):
                og = og + _advanced(dg, j, R) * wg_ref[K - 1 - j:K - j, :]
                ov = ov + _advanced(dv, j, R) * wv_ref[K - 1 - j:K - j, :]
            du_ref[0, pl.ds(r0, R), :] = og.astype(BF16)
            du_ref[1, pl.ds(r0, R), :] = ov.astype(BF16)
            return carry

        lax.fori_loop(0, n_chunks, chunk_b, 0)

        @pl.when(b == 0)
        def _():
            for k in range(K):
                dw_g_ref[k:k + 1, :] = acc[k]
                dw_v_ref[k:k + 1, :] = acc[K + k]
            db_g_ref[...] = acc[2 * K]
            db_v_ref[...] = acc[2 * K + 1]

        @pl.when(b > 0)
        def _():
            for k in range(K):
                dw_g_ref[k:k + 1, :] += acc[k]
                dw_v_ref[k:k + 1, :] += acc[K + k]
            db_g_ref[...] += acc[2 * K]
            db_v_ref[...] += acc[2 * K + 1]

    seq = pl.BlockSpec((S, cbk), lambda j, b: (b, j))
    wk = pl.BlockSpec((K, cbk), lambda j, b: (0, j))
    w1 = pl.BlockSpec((1, cbk), lambda j, b: (0, j))
    outs = pl.pallas_call(
        body, name=name, grid=(nj, batch),
        in_specs=[seq, pl.BlockSpec((S, cbk), lambda j, b: (b, nj + j)),
                  wk, pl.BlockSpec((K, cbk), lambda j, b: (0, nj + j)),
                  w1, pl.BlockSpec((1, cbk), lambda j, b: (0, nj + j)), seq],
        out_specs=[pl.BlockSpec((2, S, cbk), lambda j, b: (0, b, j)), wk, wk, w1, w1],
        out_shape=[jax.ShapeDtypeStruct((2, T, F), BF16),
                   jax.ShapeDtypeStruct((K, F), F32), jax.ShapeDtypeStruct((K, F), F32),
                   jax.ShapeDtypeStruct((1, F), F32), jax.ShapeDtypeStruct((1, F), F32)],
        scratch_shapes=[pltpu.VMEM((S, cbk), F32), pltpu.VMEM((S, cbk), F32)],
        compiler_params=_cp("parallel", "arbitrary"),
    )(up0, up0, cw, cw, cb, cb, dact)
    du, dwg, dwv, dbg, dbv = outs
    return du, jnp.concatenate([dwg, dwv], axis=1), jnp.concatenate([dbg, dbv], axis=1)


def _lru_gate_rows(xr_ref, cw_ref, cb_ref, wa_ref, wx_ref, ba_ref, bx_ref, xc_scr, za_scr, zx_scr, S, R):
    def chunk(ci, carry):
        r0 = pl.multiple_of(ci * R, R)
        xc = _conv_chunk(xr_ref, cw_ref, cb_ref, r0, R, ci)
        xc_scr[pl.ds(r0, R), :] = xc
        xb = xc.astype(BF16)
        za_scr[pl.ds(r0, R), :] = jnp.dot(xb, wa_ref[...], preferred_element_type=F32) + ba_ref[...]
        zx_scr[pl.ds(r0, R), :] = jnp.dot(xb, wx_ref[...], preferred_element_type=F32) + bx_ref[...]
        return carry

    lax.fori_loop(0, S // R, chunk, 0)


def _lru_gates(za, zx, sp):
    ra = _sigmoid(za)
    ig = _sigmoid(zx)
    la = -LRU_C * ra * sp
    a = jnp.exp(la)
    s = jnp.sqrt(_neg_expm1(2.0 * la))
    return ra, ig, a, s


def _lru_fwd(proj, cw, cb, wa, wx, ba, bx, lam, *, batch, off_x, name, carry=None):
    T = proj.shape[0]
    H, Dh, _ = wa.shape
    W = H * Dh
    S = T // batch
    K = cw.shape[0]
    ox, oy = off_x // Dh, (off_x + W) // Dh
    R = min(256, S // 2)
    n16 = S // BF16_ROWS

    def body(xr_ref, yr_ref, cw_ref, cb_ref, wa_ref, wx_ref, ba_ref, bx_ref, lam_ref,
             yb_ref, h_ref, xc_scr, za_scr, zx_scr):
        _lru_gate_rows(xr_ref, cw_ref, cb_ref, wa_ref, wx_ref, ba_ref, bx_ref, xc_scr, za_scr, zx_scr, S, R)
        sp = _softplus(-lam_ref[...])
        row = lax.broadcasted_iota(jnp.int32, (SUBLANES, Dh), 0)

        def tile(r0, carry):
            rows = pl.ds(r0, SUBLANES)
            xc = xc_scr[rows, :]
            _, ig, a, s = _lru_gates(za_scr[rows, :], zx_scr[rows, :], sp)
            A, B = a, s * (ig * xc)
            for d in (1, 2, 4):
                m = row >= d
                Bs = pltpu.roll(B, d, 0)
                As = pltpu.roll(A, d, 0)
                B = jnp.where(m, B + A * Bs, B)
                A = jnp.where(m, A * As, A)
            hh = B + A * carry
            h_ref[rows, :] = hh
            return hh, hh[SUBLANES - 1:SUBLANES, :]

        def step(i, carry):
            r0 = pl.multiple_of(i * BF16_ROWS, BF16_ROWS)
            h0, carry = tile(r0, carry)
            h1, carry = tile(r0 + SUBLANES, carry)
            hh = jnp.concatenate([h0, h1], axis=0)
            yb_ref[pl.ds(r0, BF16_ROWS), :] = (hh * _gelu(yr_ref[pl.ds(r0, BF16_ROWS), :])).astype(BF16)
            return carry

        lax.fori_loop(0, n16, step, jnp.zeros((1, Dh), F32))

    vec = pl.BlockSpec((1, Dh), lambda b, h: (0, h))
    wsp = pl.BlockSpec((None, Dh, Dh), lambda b, h: (h, 0, 0))
    seq = pl.BlockSpec((S, Dh), lambda b, h: (b, h))
    outs, carried = _carried_call(
        body, name=name, grid=(batch, H),
        in_specs=[pl.BlockSpec((S, Dh), lambda b, h: (b, ox + h)), pl.BlockSpec((S, Dh), lambda b, h: (b, oy + h)),
                  pl.BlockSpec((K, Dh), lambda b, h: (0, h)), vec, wsp, wsp, vec, vec, vec],
        out_specs=[seq, seq],
        out_shape=[jax.ShapeDtypeStruct((T, W), BF16), jax.ShapeDtypeStruct((T, W), F32)],
        scratch_shapes=[pltpu.VMEM((S, Dh), F32)] * 3,
        args=[proj, proj, cw, cb, wa, wx, ba, bx, lam], semantics=("parallel", "parallel"), carry=carry)
    return (outs[0], outs[1]) if carry is None else (outs[0], outs[1], carried)


def _lru_bwd(proj, hseq, dyb, dproj, cw, cb, wa, wx, wat, wxt, ba, bx, lam, *, batch, off_x, name, carry=None):
    T = proj.shape[0]
    H, Dh, _ = wa.shape
    W = H * Dh
    S = T // batch
    K = cw.shape[0]
    ox, oy = off_x // Dh, (off_x + W) // Dh
    R = min(256, S // 2)
    n_chunks = S // R
    n16 = S // BF16_ROWS

    def body(xr_ref, yr_ref, h_ref, dyb_ref, cw_ref, cb_ref, wa_ref, wx_ref, wat_ref, wxt_ref,
             ba_ref, bx_ref, lam_ref, dproj_in_ref,
             dxy_ref, dwa_ref, dwx_ref, dcw_ref, dcb_ref, dba_ref, dbx_ref, dlam_ref,
             xc_scr, za_scr, zx_scr, dza_scr, dzx_scr, dxc_scr):
        b = pl.program_id(1)
        _lru_gate_rows(xr_ref, cw_ref, cb_ref, wa_ref, wx_ref, ba_ref, bx_ref, xc_scr, za_scr, zx_scr, S, R)
        lam_v = lam_ref[...]
        sp = _softplus(-lam_v)
        row = lax.broadcasted_iota(jnp.int32, (SUBLANES, Dh), 0)

        def tile(r0, carry):
            a_next, g_next, s_ba, s_bx, s_lam = carry
            rows = pl.ds(r0, SUBLANES)
            xc = xc_scr[rows, :]
            ra, ig, a, s = _lru_gates(za_scr[rows, :], zx_scr[rows, :], sp)
            hh = h_ref[rows, :]
            gy, dgy = _gelu_and_grad(yr_ref[rows, :])
            dy = dyb_ref[rows, :]
            dyr = dy * hh * dgy
            C = jnp.where(row == SUBLANES - 1, a_next, pltpu.roll(a, SUBLANES - 1, 0))
            B = dy * gy
            for d in (1, 2, 4):
                m = row < SUBLANES - d
                Bs = pltpu.roll(B, SUBLANES - d, 0)
                Cs = pltpu.roll(C, SUBLANES - d, 0)
                B = jnp.where(m, B + C * Bs, B)
                C = jnp.where(m, C * Cs, C)
            G = B + C * g_next
            p0 = pl.multiple_of(jnp.maximum(r0 - SUBLANES, 0), SUBLANES)
            h_before = jnp.where(r0 > 0, h_ref[pl.ds(p0, SUBLANES), :][SUBLANES - 1:SUBLANES, :], 0.0)
            h_prev = jnp.where(row == 0, h_before, pltpu.roll(hh, 1, 0))
            da = G * h_prev
            dig = G * s * xc
            ds = G * ig * xc
            dxc_scr[rows, :] = G * s * ig
            dla = da * a - ds * (a * a) / s
            dza = dla * (-LRU_C * sp) * ra * (1.0 - ra)
            dzx = dig * ig * (1.0 - ig)
            dza_scr[rows, :] = dza
            dzx_scr[rows, :] = dzx
            carry = (a[0:1, :], G[0:1, :], s_ba + dza, s_bx + dzx, s_lam + dla * ra)
            return dyr, carry

        def step(it, carry):
            r0 = pl.multiple_of((n16 - 1 - it) * BF16_ROWS, BF16_ROWS)
            d1, carry = tile(r0 + SUBLANES, carry)
            d0, carry = tile(r0, carry)
            dxy_ref[1, pl.ds(r0, BF16_ROWS), :] = jnp.concatenate([d0, d1], axis=0).astype(BF16)
            return carry

        z1 = jnp.zeros((1, Dh), F32)
        z8 = jnp.zeros((SUBLANES, Dh), F32)
        _, _, s_ba, s_bx, s_lam = lax.fori_loop(0, n16, step, (z1, z1, z8, z8, z8))
        dba = jnp.sum(s_ba, axis=0, keepdims=True)
        dbx = jnp.sum(s_bx, axis=0, keepdims=True)
        dlam = jnp.sum(s_lam, axis=0, keepdims=True) * (LRU_C * _sigmoid(-lam_v))

        @pl.when(b == 0)
        def _():
            dwa_ref[...] = jnp.zeros_like(dwa_ref)
            dwx_ref[...] = jnp.zeros_like(dwx_ref)

        def chunk_c(ci, carry):
            r0 = pl.multiple_of(ci * R, R)
            rows = pl.ds(r0, R)
            xb = xc_scr[rows, :].astype(BF16)
            dzab = dza_scr[rows, :].astype(BF16)
            dzxb = dzx_scr[rows, :].astype(BF16)
            dwa_ref[...] += lax.dot_general(xb, dzab, (((0,), (0,)), ((), ())), preferred_element_type=F32)
            dwx_ref[...] += lax.dot_general(xb, dzxb, (((0,), (0,)), ((), ())), preferred_element_type=F32)
            dxc_scr[rows, :] += (jnp.dot(dzab, wat_ref[...], preferred_element_type=F32)
                                 + jnp.dot(dzxb, wxt_ref[...], preferred_element_type=F32))
            return carry

        lax.fori_loop(0, n_chunks, chunk_c, 0)

        def chunk_d(ci, acc):
            r0 = pl.multiple_of(ci * R, R)
            dd = _rows_with_next(dxc_scr, r0, R, ci, n_chunks, S)
            xx = _rows_with_prev(xr_ref, r0, R, ci)
            dxc = dd[:R, :]
            out = dxc * cw_ref[K - 1:K, :]
            for j in range(1, K):
                out = out + _advanced(dd, j, R) * cw_ref[K - 1 - j:K - j, :]
            dxy_ref[0, pl.ds(r0, R), :] = out.astype(BF16)
            new = [acc[k] + jnp.sum(dxc * _delayed(xx, K - 1 - k, R), axis=0, keepdims=True) for k in range(K)]
            new.append(acc[K] + jnp.sum(dxc, axis=0, keepdims=True))
            return tuple(new)

        acc = lax.fori_loop(0, n_chunks, chunk_d, (z1,) * (K + 1))

        @pl.when(b == 0)
        def _():
            for k in range(K):
                dcw_ref[k:k + 1, :] = acc[k]
            dcb_ref[...] = acc[K]
            dba_ref[...] = dba
            dbx_ref[...] = dbx
            dlam_ref[...] = dlam

        @pl.when(b > 0)
        def _():
            for k in range(K):
                dcw_ref[k:k + 1, :] += acc[k]
            dcb_ref[...] += acc[K]
            dba_ref[...] += dba
            dbx_ref[...] += dbx
            dlam_ref[...] += dlam

    vec = pl.BlockSpec((1, Dh), lambda h, b: (0, h))
    wsp = pl.BlockSpec((None, Dh, Dh), lambda h, b: (h, 0, 0))
    seq = pl.BlockSpec((S, Dh), lambda h, b: (b, h))
    ck = pl.BlockSpec((K, Dh), lambda h, b: (0, h))
    row_out = jax.ShapeDtypeStruct((1, W), F32)
    outs, carried = _carried_call(
        body, name=name, grid=(H, batch),
        in_specs=[pl.BlockSpec((S, Dh), lambda h, b: (b, ox + h)), pl.BlockSpec((S, Dh), lambda h, b: (b, oy + h)),
                  seq, seq, ck, vec, wsp, wsp, wsp, wsp, vec, vec, vec, ANY],
        out_specs=[pl.BlockSpec((2, S, Dh), lambda h, b: (0, b, h)), wsp, wsp, ck, vec, vec, vec, vec],
        out_shape=[jax.ShapeDtypeStruct(dproj.shape, dproj.dtype),
                   jax.ShapeDtypeStruct((H, Dh, Dh), F32), jax.ShapeDtypeStruct((H, Dh, Dh), F32),
                   jax.ShapeDtypeStruct((K, W), F32), row_out, row_out, row_out, row_out],
        scratch_shapes=[pltpu.VMEM((S, Dh), F32)] * 6, aliases={13: 0},
        args=[proj, proj, hseq, dyb, cw, cb, wa, wx, wat, wxt, ba, bx, lam, dproj],
        semantics=("parallel", "arbitrary"), carry=carry)
    return outs if carry is None else (outs, carried)


def _adamw(w, g, m, v, *, name, pass_grad=False, carry=None):
    R, C = w.shape
    tr, tc = _blk(R, 256), _blk(C, 1024)

    def body(w_ref, g_ref, m_ref, v_ref, d_ref, nm_ref, nv_ref, *g_out):
        gv = g_ref[...]
        if pass_grad:
            g_out[0][...] = gv
        nm = ADAM_B1 * m_ref[...] + (1.0 - ADAM_B1) * gv
        nv = ADAM_B2 * v_ref[...] + (1.0 - ADAM_B2) * (gv * gv)
        m_hat = nm / (1.0 - ADAM_B1 ** ADAM_STEP)
        v_hat = nv / (1.0 - ADAM_B2 ** ADAM_STEP)
        d_ref[...] = -ADAM_LR * (m_hat / (jnp.sqrt(v_hat) + ADAM_EPS) + ADAM_WD * w_ref[...])
        nm_ref[...] = nm
        nv_ref[...] = nv

    blk = pl.BlockSpec((tr, tc), lambda i, j: (i, j))
    out = jax.ShapeDtypeStruct((R, C), F32)
    n_out = 4 if pass_grad else 3
    outs, carried = _carried_call(
        body, name=name, grid=(R // tr, C // tc), in_specs=[blk] * 4, out_specs=[blk] * n_out,
        out_shape=[out] * n_out, scratch_shapes=[], args=[w, g, m, v], semantics=("parallel", "parallel"),
        carry=carry)
    return outs if carry is None else (outs, carried)


def _to_slab(a, pos, dtype, *, name, b=None):
    R, C = a.shape
    tr, tc = _blk(R, 512), _blk(C, 1024)

    def body(p_ref, *refs):
        v = refs[0][...]
        if b is not None:
            v = v + refs[1][...]
        refs[-1][...] = v.astype(dtype)

    blk = pl.BlockSpec((tr, tc), lambda i, j, p_ref: (i, j))
    return pl.pallas_call(
        body, name=name,
        grid_spec=pltpu.PrefetchScalarGridSpec(
            num_scalar_prefetch=1, grid=(R // tr, C // tc), in_specs=[blk] * (1 if b is None else 2),
            out_specs=pl.BlockSpec((None, tr, tc), lambda i, j, p_ref: (p_ref[0], i, j))),
        out_shape=jax.ShapeDtypeStruct((N_CHIPS, R, C), dtype),
        compiler_params=_cp("parallel", "parallel"),
    )(pos, a, *([] if b is None else [b]))


def _sum_chips(q, *, name):
    _, R, C = q.shape
    tr = _blk(R, 1024)

    def body(q_ref, o_ref):
        o_ref[...] = ((q_ref[0] + q_ref[1]) + q_ref[2]) + q_ref[3]

    return pl.pallas_call(body, name=name, grid=(R // tr,),
                          in_specs=[pl.BlockSpec((N_CHIPS, tr, C), lambda i: (0, i, 0))],
                          out_specs=pl.BlockSpec((tr, C), lambda i: (i, 0)),
                          out_shape=jax.ShapeDtypeStruct((R, C), q.dtype), compiler_params=_cp("parallel"))(q)


def _pair_add_halves(g, rb, cpos, *, name):
    n, R, C = g.shape
    hr = R // 2
    tr, tc = _blk(hr, 512), _blk(C, 1024)
    nrb = hr // tr

    def body(c_ref, g_ref, r_ref, o_ref):
        o_ref[...] = (g_ref[...].astype(F32) + r_ref[...].astype(F32)).astype(o_ref.dtype)

    return pl.pallas_call(
        body, name=name,
        grid_spec=pltpu.PrefetchScalarGridSpec(
            num_scalar_prefetch=1, grid=(n, nrb, C // tc),
            in_specs=[pl.BlockSpec((None, tr, tc), lambda s, i, j, c_ref: (s, c_ref[0] * nrb + i, j)),
                      pl.BlockSpec((None, tr, tc), lambda s, i, j, c_ref: (s, i, j))],
            out_specs=pl.BlockSpec((None, tr, tc), lambda s, i, j, c_ref: (s, i, j))),
        out_shape=jax.ShapeDtypeStruct((n, hr, C), g.dtype),
        compiler_params=_cp("parallel", "parallel", "parallel"),
    )(cpos, g, rb)


def _chip_final_add(p, q, pos, *, name):
    _, hr, C = p.shape
    tr, tc = _blk(hr, 512), _blk(C, 1024)
    nrb = hr // tr

    def body(k_ref, p_ref, q_ref, o_ref):
        o_ref[...] = ((p_ref[...].astype(F32) + q_ref[0].astype(F32)) + q_ref[1].astype(F32)) + q_ref[2].astype(F32)

    return pl.pallas_call(
        body, name=name,
        grid_spec=pltpu.PrefetchScalarGridSpec(
            num_scalar_prefetch=1, grid=(nrb, C // tc),
            in_specs=[pl.BlockSpec((None, tr, tc), lambda i, j, k_ref: (k_ref[0], i, j)),
                      pl.BlockSpec((N_CHIPS - 1, tr, tc), lambda i, j, k_ref: (0, i, j))],
            out_specs=pl.BlockSpec((tr, tc), lambda i, j, k_ref: (k_ref[1] * nrb + i, j))),
        out_shape=jax.ShapeDtypeStruct((2 * hr, C), F32),
        compiler_params=_cp("parallel", "parallel"),
    )(pos, p, q)


def _pair_swap(v, *, name):
    def body(v_ref, o_ref, send_sem, recv_sem):
        x, y, c = _mesh_pos()
        cp = pltpu.make_async_remote_copy(src_ref=v_ref, dst_ref=o_ref, send_sem=send_sem, recv_sem=recv_sem,
                                          device_id=(x, y, 1 - c), device_id_type=MESH)
        cp.start()
        cp.wait()

    return pl.pallas_call(
        body, name=name, in_specs=[ANY], out_specs=ANY, out_shape=jax.ShapeDtypeStruct(v.shape, v.dtype),
        scratch_shapes=[pltpu.SemaphoreType.DMA, pltpu.SemaphoreType.DMA],
    )(v)


def _pair_sum(g, cpos, *, tag):
    rb, = _comm_call(_Carry(_PairSwap(g)), name=f"rs_pair_swap_{tag}")
    return _pair_add_halves(g, rb, cpos, name=f"rs_pair_add_{tag}")


def _all_reduce(v, pos, *, tag):
    other = _pair_swap(v, name=f"ar_pair_swap_{tag}")
    slabs = _to_slab(v, pos, F32, b=other, name=f"ar_pair_add_{tag}")
    slabs, = _comm_call(_Carry(_GatherSlabs(slabs)), name=f"ar_allgather_{tag}")
    return _sum_chips(slabs, name=f"ar_sum_{tag}")


def _pack(arrays, unit, total_unit=None):
    parts, n = [], 0
    for a in arrays:
        flat = a.reshape(-1)
        pad = (-flat.shape[0]) % unit
        parts.append(jnp.pad(flat, (0, pad)) if pad else flat)
        n += flat.shape[0] + pad
    if total_unit and n % total_unit:
        parts.append(jnp.zeros((-n) % total_unit, arrays[0].dtype))
    return jnp.concatenate(parts).reshape(-1, LANES)


def _unpack(packed, shapes, unit):
    lead = packed.shape[:-2]
    flat = packed.reshape(lead + (-1,))
    out, pos = [], 0
    for shp in shapes:
        n = math.prod(shp)
        out.append(flat[..., pos:pos + n].reshape(lead + tuple(shp)))
        pos += n + (-n) % unit
    return out


def kernel(x, g_mix, w_in, sg_ln_g, sg_ln_b, sg_w, sg_b, lru_conv_w, lru_conv_b, lru_wa, lru_ba, lru_wx, lru_bx, lru_lam, p_sg, p_lru, w_out, g_ffn, w_up, ffn_conv_w, ffn_conv_b, w_down, g_final, loss_target, m_g_mix, m_w_in, m_sg_ln_g, m_sg_ln_b, m_sg_w, m_sg_b, m_lru_conv_w, m_lru_conv_b, m_lru_wa, m_lru_ba, m_lru_wx, m_lru_bx, m_lru_lam, m_p_sg, m_p_lru, m_w_out, m_g_ffn, m_w_up, m_ffn_conv_w, m_ffn_conv_b, m_w_down, m_g_final, v_g_mix, v_w_in, v_sg_ln_g, v_sg_ln_b, v_sg_w, v_sg_b, v_lru_conv_w, v_lru_conv_b, v_lru_wa, v_lru_ba, v_lru_wx, v_lru_bx, v_lru_lam, v_p_sg, v_p_lru, v_w_out, v_g_ffn, v_w_up, v_ffn_conv_w, v_ffn_conv_b, v_w_down, v_g_final):
    params = dict(g_mix=g_mix, w_in=w_in, sg_ln_g=sg_ln_g, sg_ln_b=sg_ln_b, sg_w=sg_w, sg_b=sg_b,
                  lru_conv_w=lru_conv_w, lru_conv_b=lru_conv_b, lru_wa=lru_wa, lru_ba=lru_ba, lru_wx=lru_wx,
                  lru_bx=lru_bx, lru_lam=lru_lam, p_sg=p_sg, p_lru=p_lru, w_out=w_out, g_ffn=g_ffn, w_up=w_up,
                  ffn_conv_w=ffn_conv_w, ffn_conv_b=ffn_conv_b, w_down=w_down, g_final=g_final)
    mom1 = dict(g_mix=m_g_mix, w_in=m_w_in, sg_ln_g=m_sg_ln_g, sg_ln_b=m_sg_ln_b, sg_w=m_sg_w, sg_b=m_sg_b,
                lru_conv_w=m_lru_conv_w, lru_conv_b=m_lru_conv_b, lru_wa=m_lru_wa, lru_ba=m_lru_ba,
                lru_wx=m_lru_wx, lru_bx=m_lru_bx, lru_lam=m_lru_lam, p_sg=m_p_sg, p_lru=m_p_lru, w_out=m_w_out,
                g_ffn=m_g_ffn, w_up=m_w_up, ffn_conv_w=m_ffn_conv_w, ffn_conv_b=m_ffn_conv_b, w_down=m_w_down,
                g_final=m_g_final)
    mom2 = dict(g_mix=v_g_mix, w_in=v_w_in, sg_ln_g=v_sg_ln_g, sg_ln_b=v_sg_ln_b, sg_w=v_sg_w, sg_b=v_sg_b,
                lru_conv_w=v_lru_conv_w, lru_conv_b=v_lru_conv_b, lru_wa=v_lru_wa, lru_ba=v_lru_ba,
                lru_wx=v_lru_wx, lru_bx=v_lru_bx, lru_lam=v_lru_lam, p_sg=v_p_sg, p_lru=v_p_lru, w_out=v_w_out,
                g_ffn=v_g_ffn, w_up=v_w_up, ffn_conv_w=v_ffn_conv_w, ffn_conv_b=v_ffn_conv_b, w_down=v_w_down,
                g_final=v_g_final)
    names = list(params)
    big = ["w_in", "p_sg", "p_lru", "w_out", "w_up", "w_down"]
    col_sharded = {"w_in", "p_sg", "w_up"}
    small = [n for n in names if n not in big]

    batch, S, D = x.shape
    T = batch * S
    W_sg = sg_ln_g.shape[-1]
    H, _, Dh = lru_wa.shape[1:]
    W_lru = H * Dh
    K_lru = lru_conv_w.shape[1]
    K_ffn = ffn_conv_w.shape[1]
    F2 = ffn_conv_b.shape[-1]
    off_lru = 2 * W_sg
    off_gate = off_lru + 2 * W_lru

    cx, cy, cc = _mesh_pos()
    chip = 2 * cx + cy
    cpos = jnp.reshape(cc, (1,)).astype(jnp.int32)
    pos = jnp.stack([chip, cc]).astype(jnp.int32)

    xf = x.reshape(T, D)
    tgt = loss_target.reshape(T, D)

    wb = {n: _to_slab(params[n][0], pos, BF16, name=f"cast_{n}") for n in big}
    rows = {n: wb[n].shape[1] for n in big}
    sharded_small = ["lru_conv_w", "ffn_conv_w", "lru_wa", "lru_wx"]
    unit_g = 2 * BF16_ROWS * LANES
    pack_g = 256 * LANES
    sm_shapes = [params[n][0].shape for n in sharded_small]
    sm = _to_slab(_pack([params[n][0] for n in sharded_small], unit_g, pack_g), pos, F32, name="slab_small")

    parts = 32

    def ici(n, lo=0, hi=parts):
        return _GatherIci(wb[n], rows[n] * lo // parts, rows[n] * hi // parts)

    def fwd(n, lo=0, hi=parts):
        return _GatherFwd(wb[n], rows[n] * lo // parts, rows[n] * hi // parts)

    quarters = [_GatherRows(wb["w_in"], rows["w_in"] * i // 4, rows["w_in"] * (i + 1) // 4) for i in range(4)]
    w_in_g, sm = _comm_call(_Carry(*quarters, _GatherRows(sm, 0, sm.shape[1])), name="gather_first")
    cwl_s, cwf_s, wa_s, wx_s = _unpack(sm, sm_shapes, unit_g)
    lru_cw = jnp.transpose(cwl_s, (1, 0, 2)).reshape(K_lru, W_lru)
    ffn_cw = jnp.transpose(cwf_s, (1, 0, 2)).reshape(K_ffn, F2)
    wa_full = jnp.transpose(wa_s, (1, 0, 2, 3)).reshape(H, Dh, Dh)
    wx_full = jnp.transpose(wx_s, (1, 0, 2, 3)).reshape(H, Dh, Dh)
    wa_b, wx_b = wa_full.astype(BF16), wx_full.astype(BF16)
    wat_b, wxt_b = jnp.swapaxes(wa_b, 1, 2), jnp.swapaxes(wx_b, 1, 2)

    wm, wmt = _sg_mask(sg_w[0], name="sg_mask")
    bt = sg_b[0].T

    h1 = _rms_fwd(xf, g_mix, name="rms1_fwd")
    proj, (wb["p_sg"], wb["p_lru"], wb["w_out"], wb["w_up"]) = _mm_nn(
        h1, w_in_g, out_dtype=F32, name="mm_proj",
        carry=_Carry(ici("p_sg"), ici("p_lru"), ici("w_out"), ici("w_up", 0, 8)))
    y_a, (wb["w_up"],) = _sg_fwd(proj, sg_ln_g, sg_ln_b, wm, bt, name="sg_fwd", carry=_Carry(ici("w_up", 8, 9)))
    y_b, hseq, (p_sg_g, p_lru_g, w_out_g, wb["w_up"]) = _lru_fwd(
        proj, lru_cw, lru_conv_b, wa_b, wx_b, lru_ba, lru_bx, lru_lam, batch=batch, off_x=off_lru, name="lru_fwd",
        carry=_Carry(fwd("p_sg"), fwd("p_lru"), fwd("w_out"), fwd("w_up", 0, 8), fwd("w_up", 8, 9),
                     ici("w_up", 9, 18)))
    p_lru_g = p_lru_g.reshape(-1, D)
    w_out_g = w_out_g.reshape(-1, D)
    pa, (wb["w_up"],) = _mm_nn(y_a, p_sg_g, out_dtype=F32, name="mm_pa",
                               carry=_Carry(fwd("w_up", 9, 18), ici("w_up", 18, 20)))
    pb, (wb["w_up"],) = _mm_nn(y_b, p_lru_g, out_dtype=F32, name="mm_pb",
                               carry=_Carry(fwd("w_up", 18, 20), ici("w_up", 20, 25)))
    merged, (wb["w_up"],) = _merge_fwd(proj, pa, pb, off_a=off_gate, name="merge_fwd",
                                       carry=_Carry(fwd("w_up", 20, 25), ici("w_up", 25, 28)))
    x1, (wb["w_up"],) = _mm_nn(merged, w_out_g, out_dtype=F32, res=xf, name="mm_out",
                               carry=_Carry(fwd("w_up", 25, 28), ici("w_up", 28, 32)))
    h2, (w_up_g,) = _rms_fwd(x1, g_ffn, name="rms2_fwd", carry=_Carry(fwd("w_up", 28, 32)))
    up0, (wb["w_down"],) = _mm_nn(h2, w_up_g, out_dtype=F32, name="mm_up", carry=_Carry(ici("w_down")))
    act, (w_down_g,) = _ffn_act_fwd(up0, ffn_cw, ffn_conv_b, batch=batch, name="ffn_act_fwd",
                                    carry=_Carry(fwd("w_down")))
    w_down_g = w_down_g.reshape(-1, D)
    x2 = _mm_nn(act, w_down_g, out_dtype=F32, res=x1, name="mm_down")
    lvec, dx2, dx2_b, dg_final = _loss_head(x2, tgt, g_final.reshape(1, D), name="loss_head")
    loss = lax.psum(jnp.sum(lvec) * (0.5 / D), ("x", "y", "c"))

    assert 2 * W_sg == W_lru == D, "d proj travels in equal column pieces"
    ps, qs = {}, {}

    def pair_add(g, rb, tag):
        return _pair_add_halves(g, rb, cpos, name=f"rs_pair_add_{tag}")

    g = _mm_tn(act, dx2_b, out_dtype=BF16, name="mm_dw_down").reshape(N_CHIPS, -1, D)
    dact, (rb,) = _mm_nt(dx2_b, w_down_g, out_dtype=F32, name="mm_dact", carry=_Carry(_PairSwap(g)))
    ps["w_down"] = pair_add(g, rb, "w_down")
    dup0, d_ffn_cw, d_ffn_cb = _ffn_act_bwd(up0, ffn_cw, ffn_conv_b, dact, batch=batch, name="ffn_act_bwd")
    g, (qs["w_down"],) = _mm_tn(h2, dup0, out_dtype=BF16, col_shards=N_CHIPS, name="mm_dw_up",
                                carry=_Carry(_ChipExchange(ps["w_down"])))
    dh2, (rb,) = _mm_nt(dup0, w_up_g, out_dtype=F32, name="mm_dh2", carry=_Carry(_PairSwap(g)))
    ps["w_up"] = pair_add(g, rb, "w_up")

    def up_piece(q, lo, hi):
        return _ChipExchange(ps["w_up"], q, lo, hi, 8)

    dx1, dx1_b, dg_ffn = _rms_bwd(x1, g_ffn, dh2, dx2, name="rms2_bwd")
    g, (q_up,) = _mm_tn(merged, dx1_b, out_dtype=BF16, name="mm_dw_out", carry=_Carry(up_piece(None, 0, 1)))
    g = g.reshape(N_CHIPS, -1, D)
    dmerged, (rb, q_up) = _mm_nt(dx1_b, w_out_g, out_dtype=F32, name="mm_dmerged",
                                 carry=_Carry(_PairSwap(g), up_piece(q_up, 1, 2)))
    ps["w_out"] = pair_add(g, rb, "w_out")
    dproj, dpa, dpb = _merge_bwd(proj, pa, pb, dmerged, off_a=off_gate, name="merge_bwd")
    g_sg = _mm_tn(y_a, dpa, out_dtype=BF16, col_shards=N_CHIPS, name="mm_dp_sg")
    g_lru, (q_up,) = _mm_tn(y_b, dpb, out_dtype=BF16, name="mm_dp_lru", carry=_Carry(up_piece(q_up, 2, 3)))
    g_lru = g_lru.reshape(N_CHIPS, -1, D)
    dya, (rb,) = _mm_nt(dpa, p_sg_g, out_dtype=F32, name="mm_dya", carry=_Carry(_PairSwap(g_sg)))
    ps["p_sg"] = pair_add(g_sg, rb, "p_sg")
    dyb, (rb, q_up) = _mm_nt(dpb, p_lru_g, out_dtype=F32, name="mm_dyb",
                             carry=_Carry(_PairSwap(g_lru), up_piece(q_up, 3, 4)))
    ps["p_lru"] = pair_add(g_lru, rb, "p_lru")
    (dproj, d_wa, d_wx, d_lru_cw, d_lru_cb, d_ba, d_bx, d_lam), (qs["w_up"],) = _lru_bwd(
        proj, hseq, dyb, dproj, lru_cw, lru_conv_b, wa_b, wx_b, wat_b, wxt_b, lru_ba, lru_bx, lru_lam,
        batch=batch, off_x=off_lru, name="lru_bwd", carry=_Carry(up_piece(q_up, 4, 8)))
    g = jnp.stack([d_wa, d_wx]).reshape(2, H, N_CHIPS, Dh // N_CHIPS, Dh)
    g = jnp.transpose(g, (2, 0, 1, 3, 4)).reshape(N_CHIPS, -1, Dh).astype(BF16)
    ps["gates"] = _pair_sum(g, cpos, tag="gates")
    dproj, d_wm, d_bt, d_lg, d_lb = _sg_bwd(proj, dya, dproj, sg_ln_g, sg_ln_b, wm, wmt, bt, name="sg_bwd")
    late = ["w_out", "p_sg", "p_lru", "gates"]
    g, carried = _mm_tn(h1, dproj, out_dtype=BF16, col_shards=N_CHIPS, name="mm_dw_in", b_shift=DPROJ_SHIFT,
                        carry=_Carry(*[_ChipExchange(ps[n]) for n in late]))
    qs.update(zip(late, carried))
    ps["w_in"] = _pair_sum(g, cpos, tag="w_in")
    dh1, (qs["w_in"],) = _mm_nt(dproj, w_in_g, out_dtype=F32, name="mm_dh1", a_shift=DPROJ_SHIFT,
                                carry=_Carry(_ChipExchange(ps["w_in"])))
    dx, _, dg_mix = _rms_bwd(xf, g_mix, dh1, dx1, name="rms1_bwd")

    scattered = big + ["gates"]
    halves = [_chip_final_add(ps[n], qs[n], pos, name=f"rs_final_add_{n}") for n in scattered]
    grads = dict(zip(scattered, _comm_call(_Carry(*[_ShareHalves(h) for h in halves]), name="rs_share")))
    d_gates = grads.pop("gates").reshape(2, H, Dh // N_CHIPS, Dh)
    grads["lru_wa"], grads["lru_wx"] = d_gates[0].reshape(lru_wa.shape), d_gates[1].reshape(lru_wx.shape)
    small_full = dict(g_mix=dg_mix, sg_ln_g=d_lg, sg_ln_b=d_lb, sg_w=d_wm, sg_b=d_bt.T, lru_conv_w=d_lru_cw,
                      lru_conv_b=d_lru_cb, lru_ba=d_ba, lru_bx=d_bx, lru_lam=d_lam,
                      g_ffn=dg_ffn, ffn_conv_w=d_ffn_cw, ffn_conv_b=d_ffn_cb, g_final=dg_final)
    reduced = list(small_full)
    unit_s = SUBLANES * LANES
    pack_s = 512 * LANES
    red = _all_reduce(_pack([small_full[n] for n in reduced], unit_s, pack_s), pos, tag="small")
    red = dict(zip(reduced, _unpack(red, [small_full[n].shape for n in reduced], unit_s)))
    cs_lru = W_lru // N_CHIPS
    cs_ffn = F2 // N_CHIPS
    red["lru_conv_w"] = lax.dynamic_slice_in_dim(red["lru_conv_w"], chip * cs_lru, cs_lru, axis=1)
    red["ffn_conv_w"] = lax.dynamic_slice_in_dim(red["ffn_conv_w"], chip * cs_ffn, cs_ffn, axis=1)
    for n in reduced:
        grads[n] = red[n].reshape(params[n].shape)

    delta, new_m, new_v = {}, {}, {}

    for n in big:
        shp = params[n].shape
        two_d = (-1, shp[-1])
        d, nm, nv, gr = _adamw(params[n].reshape(two_d), grads[n], mom1[n].reshape(two_d), mom2[n].reshape(two_d),
                               name=f"adamw_{n}", pass_grad=True)
        delta[n], new_m[n], new_v[n], grads[n] = d.reshape(shp), nm.reshape(shp), nv.reshape(shp), gr.reshape(shp)
    packs = [_pack([src[n] for n in small], unit_s, pack_s) for src in (params, grads, mom1, mom2)]
    outs = _adamw(*packs, name="adamw_small")
    shapes = [params[n].shape for n in small]
    for dst, packed in zip((delta, new_m, new_v), outs):
        dst.update(dict(zip(small, _unpack(packed, shapes, unit_s))))

    return (loss, dx.reshape(x.shape), *[grads[n] for n in names], *[delta[n] for n in names],
            *[new_m[n] for n in names], *[new_v[n] for n in names])
```

```python
import math

import jax
import jax.numpy as jnp
from jax import lax
from jax.experimental import pallas as pl
from jax.experimental.pallas import tpu as pltpu

F32 = jnp.float32
BF16 = jnp.bfloat16
MESH = pl.DeviceIdType.MESH
ANY = pl.BlockSpec(memory_space=pl.ANY)

EPS = 1e-6
LRU_C = 8.0
ADAM_LR = 0.001
ADAM_B1 = 0.9
ADAM_B2 = 0.999
ADAM_EPS = 1e-08
ADAM_WD = 0.01
ADAM_STEP = 10

N_CHIPS = 4
SUBLANES = 8
BF16_ROWS = 16
LANES = 128
VMEM_LIMIT = 56 * 1024 * 1024
DPROJ_PIECES = 5
DPROJ_SHIFT = 4
GELU_C = math.sqrt(2.0 / math.pi)
GELU_K = 0.044715


def _cp(*sem):
    return pltpu.CompilerParams(dimension_semantics=sem, vmem_limit_bytes=VMEM_LIMIT)


def _blk(dim, pref):
    if dim <= pref:
        return dim
    for b in range(pref, 0, -LANES):
        if dim % b == 0:
            return b
    b = pref
    while dim % b:
        b //= 2
    return b


def _gelu(x):
    t = jnp.tanh(GELU_C * (x + GELU_K * x * x * x))
    return 0.5 * x * (1.0 + t)


def _gelu_and_grad(x):
    x2 = x * x
    t = jnp.tanh(GELU_C * (x + GELU_K * x * x2))
    g = 0.5 * x * (1.0 + t)
    dg = 0.5 * (1.0 + t) + 0.5 * x * (1.0 - t * t) * (GELU_C * (1.0 + 3.0 * GELU_K * x2))
    return g, dg


def _sigmoid(x):
    return 1.0 / (1.0 + jnp.exp(-x))


def _softplus(x):
    e = jnp.exp(-jnp.abs(x))
    series = e * (1.0 - e * (0.5 - e * (1.0 / 3.0 - e * (0.25 - e * 0.2))))
    return jnp.where(e < 0.01, series, jnp.log(1.0 + e)) + jnp.maximum(x, 0.0)


def _neg_expm1(x):
    series = -(x * (1.0 + x * (0.5 + x * (1.0 / 6.0 + x * (1.0 / 24.0)))))
    return jnp.where(x > -0.01, series, 1.0 - jnp.exp(x))


def _mesh_pos():
    return lax.axis_index("x"), lax.axis_index("y"), lax.axis_index("c")


def _other_chips(x, y):
    return [(1 - x, y), (x, 1 - y), (1 - x, 1 - y)]


def _remote(k, src, dst, to, send_sems, recv_sems):
    return pltpu.make_async_remote_copy(src_ref=src, dst_ref=dst, send_sem=send_sems.at[k],
                                        recv_sem=recv_sems.at[k], device_id=to, device_id_type=MESH)


class _GatherRows:
    n_sems = 6

    def __init__(self, buf, r0, r1):
        self.args = [buf]
        self.out_shape = [jax.ShapeDtypeStruct(buf.shape, buf.dtype)]
        self.aliases = {0: 0}
        self.r0, self.h = r0, (r1 - r0) // 2

    def _rows(self, half):
        return pl.ds(self.r0 + half * self.h, self.h)

    def start(self, ins, outs, ss, rs, base):
        x, y, c = _mesh_pos()
        mine = ins[0].at[2 * x + y, self._rows(c), :]
        for j, (px, py) in enumerate(_other_chips(x, y)):
            _remote(base + j, mine, outs[0].at[2 * x + y, self._rows(c), :], (px, py, c), ss, rs).start()

    def finish(self, ins, outs, ss, rs, base):
        x, y, c = _mesh_pos()
        sibling = (x, y, 1 - c)
        chips = _other_chips(x, y)
        mine = ins[0].at[2 * x + y, self._rows(c), :]
        for j, (px, py) in enumerate(chips):
            got = outs[0].at[2 * px + py, self._rows(c), :]
            _remote(base + j, got, got, (px, py, c), ss, rs).wait_recv()
            _remote(base + 3 + j, got, got, sibling, ss, rs).start()
        for j, (px, py) in enumerate(chips):
            fwd = outs[0].at[2 * px + py, self._rows(1 - c), :]
            _remote(base + 3 + j, fwd, fwd, sibling, ss, rs).wait_recv()
        for j, (px, py) in enumerate(chips):
            got = outs[0].at[2 * px + py, self._rows(c), :]
            _remote(base + j, mine, mine, (px, py, c), ss, rs).wait_send()
            _remote(base + 3 + j, got, got, sibling, ss, rs).wait_send()


class _GatherIci(_GatherRows):
    n_sems = 3

    def finish(self, ins, outs, ss, rs, base):
        x, y, c = _mesh_pos()
        mine = ins[0].at[2 * x + y, self._rows(c), :]
        for j, (px, py) in enumerate(_other_chips(x, y)):
            got = outs[0].at[2 * px + py, self._rows(c), :]
            _remote(base + j, got, got, (px, py, c), ss, rs).wait_recv()
            _remote(base + j, mine, mine, (px, py, c), ss, rs).wait_send()


class _GatherFwd(_GatherRows):
    n_sems = 3

    def start(self, ins, outs, ss, rs, base):
        x, y, c = _mesh_pos()
        for j, (px, py) in enumerate(_other_chips(x, y)):
            _remote(base + j, ins[0].at[2 * px + py, self._rows(c), :], outs[0].at[2 * px + py, self._rows(c), :],
                    (x, y, 1 - c), ss, rs).start()

    def finish(self, ins, outs, ss, rs, base):
        x, y, c = _mesh_pos()
        for j, (px, py) in enumerate(_other_chips(x, y)):
            got = ins[0].at[2 * px + py, self._rows(c), :]
            fwd = outs[0].at[2 * px + py, self._rows(1 - c), :]
            _remote(base + j, got, got, (x, y, 1 - c), ss, rs).wait_send()
            _remote(base + j, fwd, fwd, (x, y, 1 - c), ss, rs).wait_recv()


class _ChipExchange:
    n_sems = 3

    def __init__(self, p, q=None, lo=0, hi=1, of=1):
        hr = p.shape[1]
        self.r0, self.n = hr * lo // of, hr * (hi - lo) // of
        self.args = [p] if q is None else [p, q]
        self.out_shape = [jax.ShapeDtypeStruct((N_CHIPS - 1,) + p.shape[1:], p.dtype)]
        self.aliases = {} if q is None else {1: 0}

    def _copies(self, ins, outs, ss, rs, base):
        x, y, c = _mesh_pos()
        rows = pl.ds(self.r0, self.n)
        return [_remote(base + j, ins[0].at[2 * px + py, rows, :], outs[0].at[j, rows, :], (px, py, c), ss, rs)
                for j, (px, py) in enumerate(_other_chips(x, y))]

    def start(self, ins, outs, ss, rs, base):
        for cp in self._copies(ins, outs, ss, rs, base):
            cp.start()

    def finish(self, ins, outs, ss, rs, base):
        for cp in self._copies(ins, outs, ss, rs, base):
            cp.wait()


class _PairSwap:
    n_sems = 1

    def __init__(self, g):
        n, R, C = g.shape
        self.args = [g]
        self.out_shape = [jax.ShapeDtypeStruct((n, R // 2, C), g.dtype)]
        self.aliases = {}
        self.hr = R // 2

    def _copy(self, ins, outs, ss, rs, base):
        x, y, c = _mesh_pos()
        return _remote(base, ins[0].at[:, pl.ds((1 - c) * self.hr, self.hr), :], outs[0], (x, y, 1 - c), ss, rs)

    def start(self, ins, outs, ss, rs, base):
        self._copy(ins, outs, ss, rs, base).start()

    def finish(self, ins, outs, ss, rs, base):
        self._copy(ins, outs, ss, rs, base).wait()


class _ShareHalves:
    n_sems = 1

    def __init__(self, buf):
        self.args = [buf]
        self.out_shape = [jax.ShapeDtypeStruct(buf.shape, buf.dtype)]
        self.aliases = {0: 0}
        self.hr = buf.shape[0] // 2

    def start(self, ins, outs, ss, rs, base):
        x, y, c = _mesh_pos()
        rows = pl.ds(c * self.hr, self.hr)
        _remote(base, ins[0].at[rows, :], outs[0].at[rows, :], (x, y, 1 - c), ss, rs).start()

    def finish(self, ins, outs, ss, rs, base):
        x, y, c = _mesh_pos()
        mine = ins[0].at[pl.ds(c * self.hr, self.hr), :]
        theirs = outs[0].at[pl.ds((1 - c) * self.hr, self.hr), :]
        _remote(base, mine, mine, (x, y, 1 - c), ss, rs).wait_send()
        _remote(base, theirs, theirs, (x, y, 1 - c), ss, rs).wait_recv()


class _GatherSlabs:
    n_sems = 3

    def __init__(self, buf):
        self.args = [buf]
        self.out_shape = [jax.ShapeDtypeStruct(buf.shape, buf.dtype)]
        self.aliases = {0: 0}

    def start(self, ins, outs, ss, rs, base):
        x, y, c = _mesh_pos()
        for j, (px, py) in enumerate(_other_chips(x, y)):
            _remote(base + j, ins[0].at[2 * x + y], outs[0].at[2 * x + y], (px, py, c), ss, rs).start()

    def finish(self, ins, outs, ss, rs, base):
        x, y, c = _mesh_pos()
        mine = ins[0].at[2 * x + y]
        for j, (px, py) in enumerate(_other_chips(x, y)):
            got = outs[0].at[2 * px + py]
            _remote(base + j, mine, mine, (px, py, c), ss, rs).wait_send()
            _remote(base + j, got, got, (px, py, c), ss, rs).wait_recv()


class _Carry:
    def __init__(self, *items):
        self.items = items
        self.args, self.out_shape, self._alias, self._slots = [], [], {}, []
        seen = {}
        for it in items:
            in_idx, out_idx = [], [None] * len(it.out_shape)
            for ai, a in enumerate(it.args):
                aliased = ai in it.aliases
                if aliased and id(a) in seen:
                    i, o = seen[id(a)]
                else:
                    i, o = len(self.args), None
                    self.args.append(a)
                    if aliased:
                        o = len(self.out_shape)
                        self.out_shape.append(it.out_shape[it.aliases[ai]])
                        self._alias[i] = o
                        seen[id(a)] = (i, o)
                in_idx.append(i)
                if aliased:
                    out_idx[it.aliases[ai]] = o
            for oi, shape in enumerate(it.out_shape):
                if out_idx[oi] is None:
                    out_idx[oi] = len(self.out_shape)
                    self.out_shape.append(shape)
            self._slots.append((in_idx, out_idx))
        self.n_sems = sum(it.n_sems for it in items)

    def aliases(self, in_base, out_base):
        return {in_base + i: out_base + o for i, o in self._alias.items()}

    def _each(self, method, ins, outs, ss, rs):
        base = 0
        for it, (in_idx, out_idx) in zip(self.items, self._slots):
            getattr(it, method)([ins[i] for i in in_idx], [outs[o] for o in out_idx], ss, rs, base)
            base += it.n_sems

    def start(self, ins, outs, ss, rs):
        self._each("start", ins, outs, ss, rs)

    def finish(self, ins, outs, ss, rs):
        self._each("finish", ins, outs, ss, rs)


def _carried_call(body, *, name, grid, in_specs, out_specs, out_shape, scratch_shapes, args, semantics, carry=None,
                  aliases=None):
    aliases = aliases or {}
    if carry is None:
        outs = pl.pallas_call(body, name=name, grid=grid, in_specs=in_specs, out_specs=out_specs,
                              out_shape=out_shape, scratch_shapes=scratch_shapes, input_output_aliases=aliases,
                              compiler_params=_cp(*semantics))(*args)
        return list(outs), []
    n_in, n_out, n_scr = len(in_specs), len(out_specs), len(scratch_shapes)
    n_cin, n_cout = len(carry.args), len(carry.out_shape)

    def full(*refs):
        ins = refs[:n_in]
        cins = refs[n_in:n_in + n_cin]
        outs = refs[n_in + n_cin:n_in + n_cin + n_out]
        couts = refs[n_in + n_cin + n_out:n_in + n_cin + n_out + n_cout]
        scr = refs[n_in + n_cin + n_out + n_cout:n_in + n_cin + n_out + n_cout + n_scr]
        ss, rs = refs[-2], refs[-1]
        first = pl.program_id(0) == 0
        last = pl.program_id(0) == grid[0] - 1
        for d in range(1, len(grid)):
            first = jnp.logical_and(first, pl.program_id(d) == 0)
            last = jnp.logical_and(last, pl.program_id(d) == grid[d] - 1)

        @pl.when(first)
        def _():
            carry.start(cins, couts, ss, rs)

        body(*ins, *outs, *scr)

        @pl.when(last)
        def _():
            carry.finish(cins, couts, ss, rs)

    outs = pl.pallas_call(
        full, name=name, grid=grid, in_specs=list(in_specs) + [ANY] * n_cin,
        out_specs=list(out_specs) + [ANY] * n_cout, out_shape=list(out_shape) + carry.out_shape,
        scratch_shapes=list(scratch_shapes) + [pltpu.SemaphoreType.DMA((carry.n_sems,))] * 2,
        input_output_aliases={**aliases, **carry.aliases(n_in, n_out)},
        compiler_params=_cp(*(("arbitrary",) * len(grid))),
    )(*args, *carry.args)
    return list(outs[:n_out]), list(outs[n_out:])


def _comm_call(carry, *, name):
    n_cin = len(carry.args)

    def body(*refs):
        cins, couts = refs[:n_cin], refs[n_cin:-2]
        carry.start(cins, couts, refs[-2], refs[-1])
        carry.finish(cins, couts, refs[-2], refs[-1])

    outs = pl.pallas_call(
        body, name=name, in_specs=[ANY] * n_cin, out_specs=[ANY] * len(carry.out_shape), out_shape=carry.out_shape,
        scratch_shapes=[pltpu.SemaphoreType.DMA((carry.n_sems,))] * 2,
        input_output_aliases=carry.aliases(0, 0),
    )(*carry.args)
    return list(outs)


def _mm_nn(a, b, *, out_dtype, name, res=None, carry=None):
    M, K = a.shape
    cs = b.shape[-1]
    N = cs * (b.shape[0] if b.ndim == 3 else 1)
    tm, tn, tk = _blk(M, 1024 if res is None else 512), _blk(cs, 1024), _blk(K, 4096)
    nbs, nk = cs // tn, K // tk
    if b.ndim == 3:
        b_spec = pl.BlockSpec((None, tk, tn), lambda i, j, k: (j // nbs, k, j % nbs))
    else:
        b_spec = pl.BlockSpec((tk, tn), lambda i, j, k: (k, j))
    in_specs = [pl.BlockSpec((tm, tk), lambda i, j, k: (i, k)), b_spec]
    args = [a, b]
    if res is not None:
        in_specs.append(pl.BlockSpec((tm, tn), lambda i, j, k: (i, j)))
        args.append(res)

    def body(*refs):
        a_ref, b_ref = refs[0], refs[1]
        r_ref = refs[2] if res is not None else None
        p = jnp.dot(a_ref[...], b_ref[...], preferred_element_type=F32)
        if nk == 1:
            o_ref = refs[-1]
            o_ref[...] = (p if r_ref is None else p + r_ref[...]).astype(out_dtype)
            return
        o_ref, acc = refs[-2], refs[-1]
        k = pl.program_id(2)

        @pl.when(k == 0)
        def _():
            acc[...] = p

        @pl.when(k > 0)
        def _():
            acc[...] += p

        @pl.when(k == nk - 1)
        def _():
            r = acc[...]
            if r_ref is not None:
                r = r + r_ref[...]
            o_ref[...] = r.astype(out_dtype)

    outs, carried = _carried_call(
        body, name=name, grid=(M // tm, N // tn, nk), in_specs=in_specs,
        out_specs=[pl.BlockSpec((tm, tn), lambda i, j, k: (i, j))],
        out_shape=[jax.ShapeDtypeStruct((M, N), out_dtype)],
        scratch_shapes=[pltpu.VMEM((tm, tn), F32)] if nk > 1 else [], args=args,
        semantics=("parallel", "parallel", "arbitrary"), carry=carry)
    return outs[0] if carry is None else (outs[0], carried)


def _mm_nt(a, b, *, out_dtype, name, carry=None, a_shift=0):
    M = a.shape[-2]
    wp = a.shape[-1]
    Kc = wp * (a.shape[0] if a.ndim == 3 else 1)
    cs = b.shape[-1]
    N = b.shape[-2]
    tm, tn, tk = _blk(M, 1024), _blk(N, 1024), _blk(math.gcd(cs, wp), 4096)
    kb = wp // tk if (a.ndim == 3 and b.ndim == 3 and tk < 2048) else 1
    nks, nka, nk = cs // tk, wp // (kb * tk), Kc // (kb * tk)
    if b.ndim == 3:
        b_specs = [pl.BlockSpec((None, tn, tk), lambda i, j, k, r=r: ((k * kb + r) // nks, j, (k * kb + r) % nks))
                   for r in range(kb)]
    else:
        b_specs = [pl.BlockSpec((tn, tk), lambda i, j, k: (j, k))]
    if a.ndim == 3:
        n_pieces = a.shape[0]
        a_spec = pl.BlockSpec((None, tm, kb * tk), lambda i, j, k: ((k // nka + a_shift) % n_pieces, i, k % nka))
    else:
        a_spec = pl.BlockSpec((tm, tk), lambda i, j, k: (i, k))

    def body(a_ref, *refs):
        b_refs, o_ref, scr = refs[:kb], refs[kb], refs[kb + 1:]
        p = None
        for r in range(kb):
            a_blk = a_ref[...] if kb == 1 else a_ref[:, r * tk:(r + 1) * tk]
            t = lax.dot_general(a_blk, b_refs[r][...], (((1,), (1,)), ((), ())), preferred_element_type=F32)
            p = t if p is None else p + t
        if nk == 1:
            o_ref[...] = p.astype(out_dtype)
            return
        acc = scr[0]
        k = pl.program_id(2)

        @pl.when(k == 0)
        def _():
            acc[...] = p

        @pl.when(k > 0)
        def _():
            acc[...] += p

        @pl.when(k == nk - 1)
        def _():
            o_ref[...] = acc[...].astype(out_dtype)

    outs, carried = _carried_call(
        body, name=name, grid=(M // tm, N // tn, nk),
        in_specs=[a_spec] + b_specs,
        out_specs=[pl.BlockSpec((tm, tn), lambda i, j, k: (i, j))],
        out_shape=[jax.ShapeDtypeStruct((M, N), out_dtype)],
        scratch_shapes=[pltpu.VMEM((tm, tn), F32)] if nk > 1 else [], args=[a] + [b] * kb,
        semantics=("parallel", "parallel", "arbitrary"), carry=carry)
    return outs[0] if carry is None else (outs[0], carried)


def _mm_tn(a, b, *, out_dtype, name, col_shards=None, carry=None, b_shift=0):
    T, K1 = a.shape
    wp = b.shape[-1]
    N = wp * (b.shape[0] if b.ndim == 3 else 1)
    cs = N // col_shards if col_shards else N
    tm, tn, tk = _blk(K1, 1024), _blk(math.gcd(cs, wp), 1024), _blk(T, 4096)
    nbs, njb, nk = cs // tn, wp // tn, T // tk
    if b.ndim == 3:
        n_pieces = b.shape[0]
        b_spec = pl.BlockSpec((None, tk, tn), lambda i, j, k: ((j // njb + b_shift) % n_pieces, k, j % njb))
    else:
        b_spec = pl.BlockSpec((tk, tn), lambda i, j, k: (k, j))
    if col_shards:
        o_spec = pl.BlockSpec((None, tm, tn), lambda i, j, k: (j // nbs, i, j % nbs))
        o_shape = jax.ShapeDtypeStruct((col_shards, K1, cs), out_dtype)
    else:
        o_spec = pl.BlockSpec((tm, tn), lambda i, j, k: (i, j))
        o_shape = jax.ShapeDtypeStruct((K1, N), out_dtype)

    def body(a_ref, b_ref, o_ref, *scr):
        p = lax.dot_general(a_ref[...], b_ref[...], (((0,), (0,)), ((), ())), preferred_element_type=F32)
        if nk == 1:
            o_ref[...] = p.astype(out_dtype)
            return
        acc = scr[0]
        k = pl.program_id(2)

        @pl.when(k == 0)
        def _():
            acc[...] = p

        @pl.when(k > 0)
        def _():
            acc[...] += p

        @pl.when(k == nk - 1)
        def _():
            o_ref[...] = acc[...].astype(out_dtype)

    outs, carried = _carried_call(
        body, name=name, grid=(K1 // tm, N // tn, nk),
        in_specs=[pl.BlockSpec((tk, tm), lambda i, j, k: (k, i)), b_spec],
        out_specs=[o_spec], out_shape=[o_shape],
        scratch_shapes=[pltpu.VMEM((tm, tn), F32)] if nk > 1 else [], args=[a, b],
        semantics=("parallel", "parallel", "arbitrary"), carry=carry)
    return outs[0] if carry is None else (outs[0], carried)


def _rms_fwd(x, g, *, name, carry=None):
    T, D = x.shape
    tm = _blk(T, 256)

    def body(x_ref, g_ref, o_ref):
        xv = x_ref[...]
        r = lax.rsqrt(jnp.mean(xv * xv, axis=-1, keepdims=True) + EPS)
        o_ref[...] = (xv * r * g_ref[...]).astype(BF16)

    outs, carried = _carried_call(
        body, name=name, grid=(T // tm,),
        in_specs=[pl.BlockSpec((tm, D), lambda i: (i, 0)), pl.BlockSpec((1, D), lambda i: (0, 0))],
        out_specs=[pl.BlockSpec((tm, D), lambda i: (i, 0))],
        out_shape=[jax.ShapeDtypeStruct((T, D), BF16)], scratch_shapes=[], args=[x, g],
        semantics=("parallel",), carry=carry)
    return outs[0] if carry is None else (outs[0], carried)


def _rms_bwd(x, g, dh, dres, *, name):
    T, D = x.shape
    tm = _blk(T, 256)

    def body(x_ref, g_ref, dh_ref, dres_ref, dx_ref, dxb_ref, dg_ref):
        i = pl.program_id(0)
        xv = x_ref[...]
        r = lax.rsqrt(jnp.mean(xv * xv, axis=-1, keepdims=True) + EPS)
        n = xv * r
        dh_v = dh_ref[...]
        dn = dh_v * g_ref[...]
        dx = dres_ref[...] + r * (dn - n * jnp.mean(dn * n, axis=-1, keepdims=True))
        dx_ref[...] = dx
        dxb_ref[...] = dx.astype(BF16)
        part = jnp.sum(dh_v * n, axis=0, keepdims=True)

        @pl.when(i == 0)
        def _():
            dg_ref[...] = part

        @pl.when(i > 0)
        def _():
            dg_ref[...] += part

    row = pl.BlockSpec((tm, D), lambda i: (i, 0))
    vec = pl.BlockSpec((1, D), lambda i: (0, 0))
    return pl.pallas_call(
        body, name=name, grid=(T // tm,),
        in_specs=[row, vec, row, row], out_specs=[row, row, vec],
        out_shape=[jax.ShapeDtypeStruct((T, D), F32), jax.ShapeDtypeStruct((T, D), BF16),
                   jax.ShapeDtypeStruct((1, D), F32)],
        compiler_params=_cp("arbitrary"),
    )(x, g, dh, dres)


def _loss_head(x2, tgt, g, *, name):
    T, D = x2.shape
    tm = _blk(T, 256)

    def body(x_ref, t_ref, g_ref, l_ref, dx_ref, dxb_ref, dg_ref):
        i = pl.program_id(0)
        xv = x_ref[...]
        gv = g_ref[...]
        r = lax.rsqrt(jnp.mean(xv * xv, axis=-1, keepdims=True) + EPS)
        n = xv * r
        err = n * gv - t_ref[...]
        dy = err * (1.0 / D)
        dn = dy * gv
        dx = r * (dn - n * jnp.mean(dn * n, axis=-1, keepdims=True))
        dx_ref[...] = dx
        dxb_ref[...] = dx.astype(BF16)
        lpart = jnp.sum(err * err, axis=0, keepdims=True)
        gpart = jnp.sum(dy * n, axis=0, keepdims=True)

        @pl.when(i == 0)
        def _():
            l_ref[...] = lpart
            dg_ref[...] = gpart

        @pl.when(i > 0)
        def _():
            l_ref[...] += lpart
            dg_ref[...] += gpart

    row = pl.BlockSpec((tm, D), lambda i: (i, 0))
    vec = pl.BlockSpec((1, D), lambda i: (0, 0))
    return pl.pallas_call(
        body, name=name, grid=(T // tm,),
        in_specs=[row, row, vec], out_specs=[vec, row, row, vec],
        out_shape=[jax.ShapeDtypeStruct((1, D), F32), jax.ShapeDtypeStruct((T, D), F32),
                   jax.ShapeDtypeStruct((T, D), BF16), jax.ShapeDtypeStruct((1, D), F32)],
        compiler_params=_cp("arbitrary"),
    )(x2, tgt, g)


def _merge_fwd(proj, pa, pb, *, off_a, name, carry=None):
    T, D = pa.shape
    tm, tn = _blk(T, 256), _blk(D, 1024)
    oa, ob = off_a // tn, (off_a + D) // tn

    def body(ga_ref, gb_ref, pa_ref, pb_ref, o_ref):
        o_ref[...] = (_sigmoid(ga_ref[...]) * pa_ref[...] + _sigmoid(gb_ref[...]) * pb_ref[...]).astype(BF16)

    blk = pl.BlockSpec((tm, tn), lambda i, j: (i, j))
    outs, carried = _carried_call(
        body, name=name, grid=(T // tm, D // tn),
        in_specs=[pl.BlockSpec((tm, tn), lambda i, j: (i, oa + j)),
                  pl.BlockSpec((tm, tn), lambda i, j: (i, ob + j)), blk, blk],
        out_specs=[blk], out_shape=[jax.ShapeDtypeStruct((T, D), BF16)], scratch_shapes=[],
        args=[proj, proj, pa, pb], semantics=("parallel", "parallel"), carry=carry)
    return outs[0] if carry is None else (outs[0], carried)


def _merge_bwd(proj, pa, pb, dm, *, off_a, name):
    T, D = pa.shape
    tm, tn = _blk(T, 256), _blk(D, 1024)
    oa, ob = off_a // tn, (off_a + D) // tn

    def body(ga_ref, gb_ref, pa_ref, pb_ref, dm_ref, dg_ref, dpa_ref, dpb_ref):
        dmv = dm_ref[...]
        sa = _sigmoid(ga_ref[...])
        sb = _sigmoid(gb_ref[...])
        dg_ref[0] = (dmv * pa_ref[...] * sa * (1.0 - sa)).astype(BF16)
        dg_ref[1] = (dmv * pb_ref[...] * sb * (1.0 - sb)).astype(BF16)
        dpa_ref[...] = (dmv * sa).astype(BF16)
        dpb_ref[...] = (dmv * sb).astype(BF16)

    blk = pl.BlockSpec((tm, tn), lambda i, j: (i, j))
    out = jax.ShapeDtypeStruct((T, D), BF16)
    return pl.pallas_call(
        body, name=name, grid=(T // tm, D // tn),
        in_specs=[pl.BlockSpec((tm, tn), lambda i, j: (i, oa + j)),
                  pl.BlockSpec((tm, tn), lambda i, j: (i, ob + j)), blk, blk, blk],
        out_specs=[pl.BlockSpec((2, tm, tn), lambda i, j: (1, i, j)), blk, blk],
        out_shape=[jax.ShapeDtypeStruct((DPROJ_PIECES, T, D), BF16), out, out],
        compiler_params=_cp("parallel", "parallel"),
    )(proj, proj, pa, pb, dm)


def _sg_mask(sg_w, *, name):
    G, C, _ = sg_w.shape

    def body(w_ref, m_ref, mt_ref):
        row = lax.broadcasted_iota(jnp.int32, (C, C), 0)
        col = lax.broadcasted_iota(jnp.int32, (C, C), 1)
        for g in range(G):
            w = jnp.where(row >= col, w_ref[g], 0.0)
            m_ref[g] = w.astype(BF16)
            mt_ref[g] = w.T.astype(BF16)

    out = jax.ShapeDtypeStruct((G, C, C), BF16)
    return pl.pallas_call(body, name=name, out_shape=[out, out])(sg_w)


def _sg_layernorm(zv, lg, lb):
    v = _gelu(zv)
    mu = jnp.mean(v, axis=-1, keepdims=True)
    xc = v - mu
    rstd = lax.rsqrt(jnp.mean(xc * xc, axis=-1, keepdims=True) + EPS)
    vhat = xc * rstd
    return vhat, rstd, vhat * lg + lb


def _sg_fwd(proj, lg, lb, wm, bt, *, name, carry=None):
    T = proj.shape[0]
    G, C, _ = wm.shape
    W = lg.shape[-1]
    gd = W // G

    def body(zu_ref, zv_ref, lg_ref, lb_ref, wm_ref, bt_ref, ya_ref, vn_scr):
        _, _, vn = _sg_layernorm(zv_ref[...], lg_ref[...], lb_ref[...])
        vn_scr[...] = vn.astype(BF16)
        for g in range(G):
            cols = slice(g * gd, (g + 1) * gd)
            mixed = jnp.dot(wm_ref[g], vn_scr[:, cols], preferred_element_type=F32) + bt_ref[:, g:g + 1]
            ya_ref[:, cols] = (_gelu(zu_ref[:, cols]) * mixed).astype(BF16)

    vec = pl.BlockSpec((1, W), lambda i: (0, 0))
    outs, carried = _carried_call(
        body, name=name, grid=(T // C,),
        in_specs=[pl.BlockSpec((C, W), lambda i: (i, 0)), pl.BlockSpec((C, W), lambda i: (i, 1)), vec, vec,
                  pl.BlockSpec((G, C, C), lambda i: (0, 0, 0)), pl.BlockSpec((C, G), lambda i: (0, 0))],
        out_specs=[pl.BlockSpec((C, W), lambda i: (i, 0))],
        out_shape=[jax.ShapeDtypeStruct((T, W), BF16)],
        scratch_shapes=[pltpu.VMEM((C, W), BF16)], args=[proj, proj, lg, lb, wm, bt],
        semantics=("parallel",), carry=carry)
    return outs[0] if carry is None else (outs[0], carried)


def _sg_bwd(proj, dya, dproj, lg, lb, wm, wmt, bt, *, name):
    T = proj.shape[0]
    G, C, _ = wm.shape
    W = lg.shape[-1]
    gd = W // G
    n_steps = T // C

    def body(zu_ref, zv_ref, dya_ref, lg_ref, lb_ref, wm_ref, wmt_ref, bt_ref, dproj_in_ref,
             dz_ref, dwm_ref, dbt_ref, dlg_ref, dlb_ref, vn_scr, dvn_scr):
        i = pl.program_id(0)

        @pl.when(i == 0)
        def _():
            dwm_ref[...] = jnp.zeros_like(dwm_ref)
            dbt_ref[...] = jnp.zeros_like(dbt_ref)
            dlg_ref[...] = jnp.zeros_like(dlg_ref)
            dlb_ref[...] = jnp.zeros_like(dlb_ref)

        lgv = lg_ref[...]
        vhat, rstd, vn = _sg_layernorm(zv_ref[...], lgv, lb_ref[...])
        vn_scr[...] = vn.astype(BF16)
        for g in range(G):
            cols = slice(g * gd, (g + 1) * gd)
            vnb = vn_scr[:, cols]
            mixed = jnp.dot(wm_ref[g], vnb, preferred_element_type=F32) + bt_ref[:, g:g + 1]
            u, du = _gelu_and_grad(zu_ref[:, cols])
            dy = dya_ref[:, cols]
            dz_ref[:, cols] = (dy * mixed * du).astype(BF16)
            dmix = dy * u
            dmb = dmix.astype(BF16)
            dbt_ref[:, g:g + 1] += jnp.sum(dmix, axis=1, keepdims=True)
            dwm_ref[g] += lax.dot_general(dmb, vnb, (((1,), (1,)), ((), ())), preferred_element_type=F32)
            dvn_scr[:, cols] = jnp.dot(wmt_ref[g], dmb, preferred_element_type=F32)
        dvn = dvn_scr[...]
        dlg_ref[...] += jnp.sum(dvn * vhat, axis=0, keepdims=True)
        dlb_ref[...] += jnp.sum(dvn, axis=0, keepdims=True)
        dvh = dvn * lgv
        dv = rstd * (dvh - jnp.mean(dvh, axis=-1, keepdims=True)
                     - vhat * jnp.mean(dvh * vhat, axis=-1, keepdims=True))
        _, dgv = _gelu_and_grad(zv_ref[...])
        dz_ref[:, W:] = (dv * dgv).astype(BF16)

        @pl.when(i == n_steps - 1)
        def _():
            row = lax.broadcasted_iota(jnp.int32, (C, C), 0)
            col = lax.broadcasted_iota(jnp.int32, (C, C), 1)
            for g in range(G):
                dwm_ref[g] = jnp.where(row >= col, dwm_ref[g], 0.0)

    vec = pl.BlockSpec((1, W), lambda i: (0, 0))
    mat = pl.BlockSpec((G, C, C), lambda i: (0, 0, 0))
    bts = pl.BlockSpec((C, G), lambda i: (0, 0))
    return pl.pallas_call(
        body, name=name, grid=(n_steps,),
        in_specs=[pl.BlockSpec((C, W), lambda i: (i, 0)), pl.BlockSpec((C, W), lambda i: (i, 1)),
                  pl.BlockSpec((C, W), lambda i: (i, 0)), vec, vec, mat, mat, bts, ANY],
        out_specs=[pl.BlockSpec((None, C, 2 * W), lambda i: (DPROJ_PIECES - 1, i, 0)), mat, bts, vec, vec],
        out_shape=[jax.ShapeDtypeStruct(dproj.shape, dproj.dtype), jax.ShapeDtypeStruct((G, C, C), F32),
                   jax.ShapeDtypeStruct((C, G), F32), jax.ShapeDtypeStruct((1, W), F32),
                   jax.ShapeDtypeStruct((1, W), F32)],
        scratch_shapes=[pltpu.VMEM((C, W), BF16), pltpu.VMEM((C, W), F32)], input_output_aliases={8: 0},
        compiler_params=_cp("arbitrary"),
    )(proj, proj, dya, lg, lb, wm, wmt, bt, dproj)


def _rows_with_prev(ref, r0, rows, ci):
    p0 = pl.multiple_of(jnp.maximum(r0 - SUBLANES, 0), SUBLANES)
    prev = jnp.where(ci > 0, ref[pl.ds(p0, SUBLANES), :], 0.0)
    return jnp.concatenate([prev, ref[pl.ds(r0, rows), :]], axis=0)


def _rows_with_next(ref, r0, rows, ci, n_chunks, total):
    n0 = pl.multiple_of(jnp.minimum(r0 + rows, total - SUBLANES), SUBLANES)
    nxt = jnp.where(ci < n_chunks - 1, ref[pl.ds(n0, SUBLANES), :], 0.0)
    return jnp.concatenate([ref[pl.ds(r0, rows), :], nxt], axis=0)


def _delayed(xx, k, rows):
    if k == 0:
        return xx[SUBLANES:, :]
    return pltpu.roll(xx, k, 0)[SUBLANES:, :]


def _advanced(xx, k, rows):
    if k == 0:
        return xx[:rows, :]
    return pltpu.roll(xx, rows + SUBLANES - k, 0)[:rows, :]


def _conv_chunk(x_ref, w_ref, b_ref, r0, rows, ci):
    K = w_ref.shape[0]
    xx = _rows_with_prev(x_ref, r0, rows, ci)
    out = _delayed(xx, K - 1, rows) * w_ref[0:1, :]
    for k in range(1, K):
        out = out + _delayed(xx, K - 1 - k, rows) * w_ref[k:k + 1, :]
    return out + b_ref[...]


def _ffn_act_fwd(up0, cw, cb, *, batch, name, carry=None):
    T, F2 = up0.shape
    F = F2 // 2
    S = T // batch
    K = cw.shape[0]
    cbk = _blk(F, 512)
    nj = F // cbk
    R = min(64, S // 2)
    n_chunks = S // R

    def body(ug_ref, uv_ref, wg_ref, wv_ref, bg_ref, bv_ref, act_ref):
        def chunk(ci, carry):
            r0 = pl.multiple_of(ci * R, R)
            cg = _conv_chunk(ug_ref, wg_ref, bg_ref, r0, R, ci)
            cv = _conv_chunk(uv_ref, wv_ref, bv_ref, r0, R, ci)
            act_ref[pl.ds(r0, R), :] = (_gelu(cg) * cv).astype(BF16)
            return carry

        lax.fori_loop(0, n_chunks, chunk, 0)

    outs, carried = _carried_call(
        body, name=name, grid=(nj, batch),
        in_specs=[pl.BlockSpec((S, cbk), lambda j, b: (b, j)), pl.BlockSpec((S, cbk), lambda j, b: (b, nj + j)),
                  pl.BlockSpec((K, cbk), lambda j, b: (0, j)), pl.BlockSpec((K, cbk), lambda j, b: (0, nj + j)),
                  pl.BlockSpec((1, cbk), lambda j, b: (0, j)), pl.BlockSpec((1, cbk), lambda j, b: (0, nj + j))],
        out_specs=[pl.BlockSpec((S, cbk), lambda j, b: (b, j))],
        out_shape=[jax.ShapeDtypeStruct((T, F), BF16)], scratch_shapes=[],
        args=[up0, up0, cw, cw, cb, cb], semantics=("parallel", "parallel"), carry=carry)
    return outs[0] if carry is None else (outs[0], carried)


def _ffn_act_bwd(up0, cw, cb, dact, *, batch, name):
    T, F2 = up0.shape
    F = F2 // 2
    S = T // batch
    K = cw.shape[0]
    cbk = _blk(F, 512)
    nj = F // cbk
    R = min(64, S // 2)
    n_chunks = S // R

    def body(ug_ref, uv_ref, wg_ref, wv_ref, bg_ref, bv_ref, da_ref,
             du_ref, dw_g_ref, dw_v_ref, db_g_ref, db_v_ref, dcg_scr, dcv_scr):
        b = pl.program_id(1)

        def chunk_a(ci, acc):
            r0 = pl.multiple_of(ci * R, R)
            xg = _rows_with_prev(ug_ref, r0, R, ci)
            xv = _rows_with_prev(uv_ref, r0, R, ci)
            dg_taps = [_delayed(xg, K - 1 - k, R) for k in range(K)]
            dv_taps = [_delayed(xv, K - 1 - k, R) for k in range(K)]
            cg = dg_taps[0] * wg_ref[0:1, :]
            cv = dv_taps[0] * wv_ref[0:1, :]
            for k in range(1, K):
                cg = cg + dg_taps[k] * wg_ref[k:k + 1, :]
                cv = cv + dv_taps[k] * wv_ref[k:k + 1, :]
            cg = cg + bg_ref[...]
            cv = cv + bv_ref[...]
            gl, dgl = _gelu_and_grad(cg)
            da = da_ref[pl.ds(r0, R), :]
            dcg = da * cv * dgl
            dcv = da * gl
            dcg_scr[pl.ds(r0, R), :] = dcg
            dcv_scr[pl.ds(r0, R), :] = dcv
            new = []
            for k in range(K):
                new.append(acc[k] + jnp.sum(dcg * dg_taps[k], axis=0, keepdims=True))
            for k in range(K):
                new.append(acc[K + k] + jnp.sum(dcv * dv_taps[k], axis=0, keepdims=True))
            new.append(acc[2 * K] + jnp.sum(dcg, axis=0, keepdims=True))
            new.append(acc[2 * K + 1] + jnp.sum(dcv, axis=0, keepdims=True))
            return tuple(new)

        zero = jnp.zeros((1, cbk), F32)
        acc = lax.fori_loop(0, n_chunks, chunk_a, (zero,) * (2 * K + 2))

        def chunk_b(ci, carry):
            r0 = pl.multiple_of(ci * R, R)
            dg = _rows_with_next(dcg_scr, r0, R, ci, n_chunks, S)
            dv = _rows_with_next(dcv_scr, r0, R, ci, n_chunks, S)
            og = _advanced(dg, 0, R) * wg_ref[K - 1:K, :]
            ov = _advanced(dv, 0, R) * wv_ref[K - 1:K, :]
            for j in range(1, K):
                og = og + _advanced(dg, j, R) * wg_ref[K - 1 - j:K - j, :]
                ov = ov + _advanced(dv, j, R) * wv_ref[K - 1 - j:K - j, :]
            du_ref[0, pl.ds(r0, R), :] = og.astype(BF16)
            du_ref[1, pl.ds(r0, R), :] = ov.astype(BF16)
            return carry

        lax.fori_loop(0, n_chunks, chunk_b, 0)

        @pl.when(b == 0)
        def _():
            for k in range(K):
                dw_g_ref[k:k + 1, :] = acc[k]
                dw_v_ref[k:k + 1, :] = acc[K + k]
            db_g_ref[...] = acc[2 * K]
            db_v_ref[...] = acc[2 * K + 1]

        @pl.when(b > 0)
        def _():
            for k in range(K):
                dw_g_ref[k:k + 1, :] += acc[k]
                dw_v_ref[k:k + 1, :] += acc[K + k]
            db_g_ref[...] += acc[2 * K]
            db_v_ref[...] += acc[2 * K + 1]

    seq = pl.BlockSpec((S, cbk), lambda j, b: (b, j))
    wk = pl.BlockSpec((K, cbk), lambda j, b: (0, j))
    w1 = pl.BlockSpec((1, cbk), lambda j, b: (0, j))
    outs = pl.pallas_call(
        body, name=name, grid=(nj, batch),
        in_specs=[seq, pl.BlockSpec((S, cbk), lambda j, b: (b, nj + j)),
                  wk, pl.BlockSpec((K, cbk), lambda j, b: (0, nj + j)),
                  w1, pl.BlockSpec((1, cbk), lambda j, b: (0, nj + j)), seq],
        out_specs=[pl.BlockSpec((2, S, cbk), lambda j, b: (0, b, j)), wk, wk, w1, w1],
        out_shape=[jax.ShapeDtypeStruct((2, T, F), BF16),
                   jax.ShapeDtypeStruct((K, F), F32), jax.ShapeDtypeStruct((K, F), F32),
                   jax.ShapeDtypeStruct((1, F), F32), jax.ShapeDtypeStruct((1, F), F32)],
        scratch_shapes=[pltpu.VMEM((S, cbk), F32), pltpu.VMEM((S, cbk), F32)],
        compiler_params=_cp("parallel", "arbitrary"),
    )(up0, up0, cw, cw, cb, cb, dact)
    du, dwg, dwv, dbg, dbv = outs
    return du, jnp.concatenate([dwg, dwv], axis=1), jnp.concatenate([dbg, dbv], axis=1)


def _lru_gate_rows(xr_ref, cw_ref, cb_ref, wa_ref, wx_ref, ba_ref, bx_ref, xc_scr, za_scr, zx_scr, S, R):
    def chunk(ci, carry):
        r0 = pl.multiple_of(ci * R, R)
        xc = _conv_chunk(xr_ref, cw_ref, cb_ref, r0, R, ci)
        xc_scr[pl.ds(r0, R), :] = xc
        xb = xc.astype(BF16)
        za_scr[pl.ds(r0, R), :] = jnp.dot(xb, wa_ref[...], preferred_element_type=F32) + ba_ref[...]
        zx_scr[pl.ds(r0, R), :] = jnp.dot(xb, wx_ref[...], preferred_element_type=F32) + bx_ref[...]
        return carry

    lax.fori_loop(0, S // R, chunk, 0)


def _lru_gates(za, zx, sp):
    ra = _sigmoid(za)
    ig = _sigmoid(zx)
    la = -LRU_C * ra * sp
    a = jnp.exp(la)
    s = jnp.sqrt(_neg_expm1(2.0 * la))
    return ra, ig, a, s


def _lru_fwd(proj, cw, cb, wa, wx, ba, bx, lam, *, batch, off_x, name, carry=None):
    T = proj.shape[0]
    H, Dh, _ = wa.shape
    W = H * Dh
    S = T // batch
    K = cw.shape[0]
    ox, oy = off_x // Dh, (off_x + W) // Dh
    R = min(256, S // 2)
    n16 = S // BF16_ROWS

    def body(xr_ref, yr_ref, cw_ref, cb_ref, wa_ref, wx_ref, ba_ref, bx_ref, lam_ref,
             yb_ref, h_ref, xc_scr, za_scr, zx_scr):
        _lru_gate_rows(xr_ref, cw_ref, cb_ref, wa_ref, wx_ref, ba_ref, bx_ref, xc_scr, za_scr, zx_scr, S, R)
        sp = _softplus(-lam_ref[...])
        row = lax.broadcasted_iota(jnp.int32, (SUBLANES, Dh), 0)

        def tile(r0, carry):
            rows = pl.ds(r0, SUBLANES)
            xc = xc_scr[rows, :]
            _, ig, a, s = _lru_gates(za_scr[rows, :], zx_scr[rows, :], sp)
            A, B = a, s * (ig * xc)
            for d in (1, 2, 4):
                m = row >= d
                Bs = pltpu.roll(B, d, 0)
                As = pltpu.roll(A, d, 0)
                B = jnp.where(m, B + A * Bs, B)
                A = jnp.where(m, A * As, A)
            hh = B + A * carry
            h_ref[rows, :] = hh
            return hh, hh[SUBLANES - 1:SUBLANES, :]

        def step(i, carry):
            r0 = pl.multiple_of(i * BF16_ROWS, BF16_ROWS)
            h0, carry = tile(r0, carry)
            h1, carry = tile(r0 + SUBLANES, carry)
            hh = jnp.concatenate([h0, h1], axis=0)
            yb_ref[pl.ds(r0, BF16_ROWS), :] = (hh * _gelu(yr_ref[pl.ds(r0, BF16_ROWS), :])).astype(BF16)
            return carry

        lax.fori_loop(0, n16, step, jnp.zeros((1, Dh), F32))

    vec = pl.BlockSpec((1, Dh), lambda b, h: (0, h))
    wsp = pl.BlockSpec((None, Dh, Dh), lambda b, h: (h, 0, 0))
    seq = pl.BlockSpec((S, Dh), lambda b, h: (b, h))
    outs, carried = _carried_call(
        body, name=name, grid=(batch, H),
        in_specs=[pl.BlockSpec((S, Dh), lambda b, h: (b, ox + h)), pl.BlockSpec((S, Dh), lambda b, h: (b, oy + h)),
                  pl.BlockSpec((K, Dh), lambda b, h: (0, h)), vec, wsp, wsp, vec, vec, vec],
        out_specs=[seq, seq],
        out_shape=[jax.ShapeDtypeStruct((T, W), BF16), jax.ShapeDtypeStruct((T, W), F32)],
        scratch_shapes=[pltpu.VMEM((S, Dh), F32)] * 3,
        args=[proj, proj, cw, cb, wa, wx, ba, bx, lam], semantics=("parallel", "parallel"), carry=carry)
    return (outs[0], outs[1]) if carry is None else (outs[0], outs[1], carried)


def _lru_bwd(proj, hseq, dyb, dproj, cw, cb, wa, wx, wat, wxt, ba, bx, lam, *, batch, off_x, name, carry=None):
    T = proj.shape[0]
    H, Dh, _ = wa.shape
    W = H * Dh
    S = T // batch
    K = cw.shape[0]
    ox, oy = off_x // Dh, (off_x + W) // Dh
    R = min(256, S // 2)
    n_chunks = S // R
    n16 = S // BF16_ROWS

    def body(xr_ref, yr_ref, h_ref, dyb_ref, cw_ref, cb_ref, wa_ref, wx_ref, wat_ref, wxt_ref,
             ba_ref, bx_ref, lam_ref, dproj_in_ref,
             dxy_ref, dwa_ref, dwx_ref, dcw_ref, dcb_ref, dba_ref, dbx_ref, dlam_ref,
             xc_scr, za_scr, zx_scr, dza_scr, dzx_scr, dxc_scr):
        b = pl.program_id(1)
        _lru_gate_rows(xr_ref, cw_ref, cb_ref, wa_ref, wx_ref, ba_ref, bx_ref, xc_scr, za_scr, zx_scr, S, R)
        lam_v = lam_ref[...]
        sp = _softplus(-lam_v)
        row = lax.broadcasted_iota(jnp.int32, (SUBLANES, Dh), 0)

        def tile(r0, carry):
            a_next, g_next, s_ba, s_bx, s_lam = carry
            rows = pl.ds(r0, SUBLANES)
            xc = xc_scr[rows, :]
            ra, ig, a, s = _lru_gates(za_scr[rows, :], zx_scr[rows, :], sp)
            hh = h_ref[rows, :]
            gy, dgy = _gelu_and_grad(yr_ref[rows, :])
            dy = dyb_ref[rows, :]
            dyr = dy * hh * dgy
            C = jnp.where(row == SUBLANES - 1, a_next, pltpu.roll(a, SUBLANES - 1, 0))
            B = dy * gy
            for d in (1, 2, 4):
                m = row < SUBLANES - d
                Bs = pltpu.roll(B, SUBLANES - d, 0)
                Cs = pltpu.roll(C, SUBLANES - d, 0)
                B = jnp.where(m, B + C * Bs, B)
                C = jnp.where(m, C * Cs, C)
            G = B + C * g_next
            p0 = pl.multiple_of(jnp.maximum(r0 - SUBLANES, 0), SUBLANES)
            h_before = jnp.where(r0 > 0, h_ref[pl.ds(p0, SUBLANES), :][SUBLANES - 1:SUBLANES, :], 0.0)
            h_prev = jnp.where(row == 0, h_before, pltpu.roll(hh, 1, 0))
            da = G * h_prev
            dig = G * s * xc
            ds = G * ig * xc
            dxc_scr[rows, :] = G * s * ig
            dla = da * a - ds * (a * a) / s
            dza = dla * (-LRU_C * sp) * ra * (1.0 - ra)
            dzx = dig * ig * (1.0 - ig)
            dza_scr[rows, :] = dza
            dzx_scr[rows, :] = dzx
            carry = (a[0:1, :], G[0:1, :], s_ba + dza, s_bx + dzx, s_lam + dla * ra)
            return dyr, carry

        def step(it, carry):
            r0 = pl.multiple_of((n16 - 1 - it) * BF16_ROWS, BF16_ROWS)
            d1, carry = tile(r0 + SUBLANES, carry)
            d0, carry = tile(r0, carry)
            dxy_ref[1, pl.ds(r0, BF16_ROWS), :] = jnp.concatenate([d0, d1], axis=0).astype(BF16)
            return carry

        z1 = jnp.zeros((1, Dh), F32)
        z8 = jnp.zeros((SUBLANES, Dh), F32)
        _, _, s_ba, s_bx, s_lam = lax.fori_loop(0, n16, step, (z1, z1, z8, z8, z8))
        dba = jnp.sum(s_ba, axis=0, keepdims=True)
        dbx = jnp.sum(s_bx, axis=0, keepdims=True)
        dlam = jnp.sum(s_lam, axis=0, keepdims=True) * (LRU_C * _sigmoid(-lam_v))

        @pl.when(b == 0)
        def _():
            dwa_ref[...] = jnp.zeros_like(dwa_ref)
            dwx_ref[...] = jnp.zeros_like(dwx_ref)

        def chunk_c(ci, carry):
            r0 = pl.multiple_of(ci * R, R)
            rows = pl.ds(r0, R)
            xb = xc_scr[rows, :].astype(BF16)
            dzab = dza_scr[rows, :].astype(BF16)
            dzxb = dzx_scr[rows, :].astype(BF16)
            dwa_ref[...] += lax.dot_general(xb, dzab, (((0,), (0,)), ((), ())), preferred_element_type=F32)
            dwx_ref[...] += lax.dot_general(xb, dzxb, (((0,), (0,)), ((), ())), preferred_element_type=F32)
            dxc_scr[rows, :] += (jnp.dot(dzab, wat_ref[...], preferred_element_type=F32)
                                 + jnp.dot(dzxb, wxt_ref[...], preferred_element_type=F32))
            return carry

        lax.fori_loop(0, n_chunks, chunk_c, 0)

        def chunk_d(ci, acc):
            r0 = pl.multiple_of(ci * R, R)
            dd = _rows_with_next(dxc_scr, r0, R, ci, n_chunks, S)
            xx = _rows_with_prev(xr_ref, r0, R, ci)
            dxc = dd[:R, :]
            out = dxc * cw_ref[K - 1:K, :]
            for j in range(1, K):
                out = out + _advanced(dd, j, R) * cw_ref[K - 1 - j:K - j, :]
            dxy_ref[0, pl.ds(r0, R), :] = out.astype(BF16)
            new = [acc[k] + jnp.sum(dxc * _delayed(xx, K - 1 - k, R), axis=0, keepdims=True) for k in range(K)]
            new.append(acc[K] + jnp.sum(dxc, axis=0, keepdims=True))
            return tuple(new)

        acc = lax.fori_loop(0, n_chunks, chunk_d, (z1,) * (K + 1))

        @pl.when(b == 0)
        def _():
            for k in range(K):
                dcw_ref[k:k + 1, :] = acc[k]
            dcb_ref[...] = acc[K]
            dba_ref[...] = dba
            dbx_ref[...] = dbx
            dlam_ref[...] = dlam

        @pl.when(b > 0)
        def _():
            for k in range(K):
                dcw_ref[k:k + 1, :] += acc[k]
            dcb_ref[...] += acc[K]
            dba_ref[...] += dba
            dbx_ref[...] += dbx
            dlam_ref[...] += dlam

    vec = pl.BlockSpec((1, Dh), lambda h, b: (0, h))
    wsp = pl.BlockSpec((None, Dh, Dh), lambda h, b: (h, 0, 0))
    seq = pl.BlockSpec((S, Dh), lambda h, b: (b, h))
    ck = pl.BlockSpec((K, Dh), lambda h, b: (0, h))
    row_out = jax.ShapeDtypeStruct((1, W), F32)
    outs, carried = _carried_call(
        body, name=name, grid=(H, batch),
        in_specs=[pl.BlockSpec((S, Dh), lambda h, b: (b, ox + h)), pl.BlockSpec((S, Dh), lambda h, b: (b, oy + h)),
                  seq, seq, ck, vec, wsp, wsp, wsp, wsp, vec, vec, vec, ANY],
        out_specs=[pl.BlockSpec((2, S, Dh), lambda h, b: (0, b, h)), wsp, wsp, ck, vec, vec, vec, vec],
        out_shape=[jax.ShapeDtypeStruct(dproj.shape, dproj.dtype),
                   jax.ShapeDtypeStruct((H, Dh, Dh), F32), jax.ShapeDtypeStruct((H, Dh, Dh), F32),
                   jax.ShapeDtypeStruct((K, W), F32), row_out, row_out, row_out, row_out],
        scratch_shapes=[pltpu.VMEM((S, Dh), F32)] * 6, aliases={13: 0},
        args=[proj, proj, hseq, dyb, cw, cb, wa, wx, wat, wxt, ba, bx, lam, dproj],
        semantics=("parallel", "arbitrary"), carry=carry)
    return outs if carry is None else (outs, carried)


def _adamw(w, g, m, v, *, name, pass_grad=False, carry=None):
    R, C = w.shape
    tr, tc = _blk(R, 256), _blk(C, 1024)

    def body(w_ref, g_ref, m_ref, v_ref, d_ref, nm_ref, nv_ref, *g_out):
        gv = g_ref[...]
        if pass_grad:
            g_out[0][...] = gv
        nm = ADAM_B1 * m_ref[...] + (1.0 - ADAM_B1) * gv
        nv = ADAM_B2 * v_ref[...] + (1.0 - ADAM_B2) * (gv * gv)
        m_hat = nm / (1.0 - ADAM_B1 ** ADAM_STEP)
        v_hat = nv / (1.0 - ADAM_B2 ** ADAM_STEP)
        d_ref[...] = -ADAM_LR * (m_hat / (jnp.sqrt(v_hat) + ADAM_EPS) + ADAM_WD * w_ref[...])
        nm_ref[...] = nm
        nv_ref[...] = nv

    blk = pl.BlockSpec((tr, tc), lambda i, j: (i, j))
    out = jax.ShapeDtypeStruct((R, C), F32)
    n_out = 4 if pass_grad else 3
    outs, carried = _carried_call(
        body, name=name, grid=(R // tr, C // tc), in_specs=[blk] * 4, out_specs=[blk] * n_out,
        out_shape=[out] * n_out, scratch_shapes=[], args=[w, g, m, v], semantics=("parallel", "parallel"),
        carry=carry)
    return outs if carry is None else (outs, carried)


def _to_slab(a, pos, dtype, *, name, b=None):
    R, C = a.shape
    tr, tc = _blk(R, 512), _blk(C, 1024)

    def body(p_ref, *refs):
        v = refs[0][...]
        if b is not None:
            v = v + refs[1][...]
        refs[-1][...] = v.astype(dtype)

    blk = pl.BlockSpec((tr, tc), lambda i, j, p_ref: (i, j))
    return pl.pallas_call(
        body, name=name,
        grid_spec=pltpu.PrefetchScalarGridSpec(
            num_scalar_prefetch=1, grid=(R // tr, C // tc), in_specs=[blk] * (1 if b is None else 2),
            out_specs=pl.BlockSpec((None, tr, tc), lambda i, j, p_ref: (p_ref[0], i, j))),
        out_shape=jax.ShapeDtypeStruct((N_CHIPS, R, C), dtype),
        compiler_params=_cp("parallel", "parallel"),
    )(pos, a, *([] if b is None else [b]))


def _sum_chips(q, *, name):
    _, R, C = q.shape
    tr = _blk(R, 1024)

    def body(q_ref, o_ref):
        o_ref[...] = ((q_ref[0] + q_ref[1]) + q_ref[2]) + q_ref[3]

    return pl.pallas_call(body, name=name, grid=(R // tr,),
                          in_specs=[pl.BlockSpec((N_CHIPS, tr, C), lambda i: (0, i, 0))],
                          out_specs=pl.BlockSpec((tr, C), lambda i: (i, 0)),
                          out_shape=jax.ShapeDtypeStruct((R, C), q.dtype), compiler_params=_cp("parallel"))(q)


def _pair_add_halves(g, rb, cpos, *, name):
    n, R, C = g.shape
    hr = R // 2
    tr, tc = _blk(hr, 512), _blk(C, 1024)
    nrb = hr // tr

    def body(c_ref, g_ref, r_ref, o_ref):
        o_ref[...] = (g_ref[...].astype(F32) + r_ref[...].astype(F32)).astype(o_ref.dtype)

    return pl.pallas_call(
        body, name=name,
        grid_spec=pltpu.PrefetchScalarGridSpec(
            num_scalar_prefetch=1, grid=(n, nrb, C // tc),
            in_specs=[pl.BlockSpec((None, tr, tc), lambda s, i, j, c_ref: (s, c_ref[0] * nrb + i, j)),
                      pl.BlockSpec((None, tr, tc), lambda s, i, j, c_ref: (s, i, j))],
            out_specs=pl.BlockSpec((None, tr, tc), lambda s, i, j, c_ref: (s, i, j))),
        out_shape=jax.ShapeDtypeStruct((n, hr, C), g.dtype),
        compiler_params=_cp("parallel", "parallel", "parallel"),
    )(cpos, g, rb)


def _chip_final_add(p, q, pos, *, name):
    _, hr, C = p.shape
    tr, tc = _blk(hr, 512), _blk(C, 1024)
    nrb = hr // tr

    def body(k_ref, p_ref, q_ref, o_ref):
        o_ref[...] = ((p_ref[...].astype(F32) + q_ref[0].astype(F32)) + q_ref[1].astype(F32)) + q_ref[2].astype(F32)

    return pl.pallas_call(
        body, name=name,
        grid_spec=pltpu.PrefetchScalarGridSpec(
            num_scalar_prefetch=1, grid=(nrb, C // tc),
            in_specs=[pl.BlockSpec((None, tr, tc), lambda i, j, k_ref: (k_ref[0], i, j)),
                      pl.BlockSpec((N_CHIPS - 1, tr, tc), lambda i, j, k_ref: (0, i, j))],
            out_specs=pl.BlockSpec((tr, tc), lambda i, j, k_ref: (k_ref[1] * nrb + i, j))),
        out_shape=jax.ShapeDtypeStruct((2 * hr, C), F32),
        compiler_params=_cp("parallel", "parallel"),
    )(pos, p, q)


def _pair_swap(v, *, name):
    def body(v_ref, o_ref, send_sem, recv_sem):
        x, y, c = _mesh_pos()
        cp = pltpu.make_async_remote_copy(src_ref=v_ref, dst_ref=o_ref, send_sem=send_sem, recv_sem=recv_sem,
                                          device_id=(x, y, 1 - c), device_id_type=MESH)
        cp.start()
        cp.wait()

    return pl.pallas_call(
        body, name=name, in_specs=[ANY], out_specs=ANY, out_shape=jax.ShapeDtypeStruct(v.shape, v.dtype),
        scratch_shapes=[pltpu.SemaphoreType.DMA, pltpu.SemaphoreType.DMA],
    )(v)


def _pair_sum(g, cpos, *, tag):
    rb, = _comm_call(_Carry(_PairSwap(g)), name=f"rs_pair_swap_{tag}")
    return _pair_add_halves(g, rb, cpos, name=f"rs_pair_add_{tag}")


def _all_reduce(v, pos, *, tag):
    other = _pair_swap(v, name=f"ar_pair_swap_{tag}")
    slabs = _to_slab(v, pos, F32, b=other, name=f"ar_pair_add_{tag}")
    slabs, = _comm_call(_Carry(_GatherSlabs(slabs)), name=f"ar_allgather_{tag}")
    return _sum_chips(slabs, name=f"ar_sum_{tag}")


def _pack(arrays, unit, total_unit=None):
    parts, n = [], 0
    for a in arrays:
        flat = a.reshape(-1)
        pad = (-flat.shape[0]) % unit
        parts.append(jnp.pad(flat, (0, pad)) if pad else flat)
        n += flat.shape[0] + pad
    if total_unit and n % total_unit:
        parts.append(jnp.zeros((-n) % total_unit, arrays[0].dtype))
    return jnp.concatenate(parts).reshape(-1, LANES)


def _unpack(packed, shapes, unit):
    lead = packed.shape[:-2]
    flat = packed.reshape(lead + (-1,))
    out, pos = [], 0
    for shp in shapes:
        n = math.prod(shp)
        out.append(flat[..., pos:pos + n].reshape(lead + tuple(shp)))
        pos += n + (-n) % unit
    return out


def kernel(x, g_mix, w_in, sg_ln_g, sg_ln_b, sg_w, sg_b, lru_conv_w, lru_conv_b, lru_wa, lru_ba, lru_wx, lru_bx, lru_lam, p_sg, p_lru, w_out, g_ffn, w_up, ffn_conv_w, ffn_conv_b, w_down, g_final, loss_target, m_g_mix, m_w_in, m_sg_ln_g, m_sg_ln_b, m_sg_w, m_sg_b, m_lru_conv_w, m_lru_conv_b, m_lru_wa, m_lru_ba, m_lru_wx, m_lru_bx, m_lru_lam, m_p_sg, m_p_lru, m_w_out, m_g_ffn, m_w_up, m_ffn_conv_w, m_ffn_conv_b, m_w_down, m_g_final, v_g_mix, v_w_in, v_sg_ln_g, v_sg_ln_b, v_sg_w, v_sg_b, v_lru_conv_w, v_lru_conv_b, v_lru_wa, v_lru_ba, v_lru_wx, v_lru_bx, v_lru_lam, v_p_sg, v_p_lru, v_w_out, v_g_ffn, v_w_up, v_ffn_conv_w, v_ffn_conv_b, v_w_down, v_g_final):
    params = dict(g_mix=g_mix, w_in=w_in, sg_ln_g=sg_ln_g, sg_ln_b=sg_ln_b, sg_w=sg_w, sg_b=sg_b,
                  lru_conv_w=lru_conv_w, lru_conv_b=lru_conv_b, lru_wa=lru_wa, lru_ba=lru_ba, lru_wx=lru_wx,
                  lru_bx=lru_bx, lru_lam=lru_lam, p_sg=p_sg, p_lru=p_lru, w_out=w_out, g_ffn=g_ffn, w_up=w_up,
                  ffn_conv_w=ffn_conv_w, ffn_conv_b=ffn_conv_b, w_down=w_down, g_final=g_final)
    mom1 = dict(g_mix=m_g_mix, w_in=m_w_in, sg_ln_g=m_sg_ln_g, sg_ln_b=m_sg_ln_b, sg_w=m_sg_w, sg_b=m_sg_b,
                lru_conv_w=m_lru_conv_w, lru_conv_b=m_lru_conv_b, lru_wa=m_lru_wa, lru_ba=m_lru_ba,
                lru_wx=m_lru_wx, lru_bx=m_lru_bx, lru_lam=m_lru_lam, p_sg=m_p_sg, p_lru=m_p_lru, w_out=m_w_out,
                g_ffn=m_g_ffn, w_up=m_w_up, ffn_conv_w=m_ffn_conv_w, ffn_conv_b=m_ffn_conv_b, w_down=m_w_down,
                g_final=m_g_final)
    mom2 = dict(g_mix=v_g_mix, w_in=v_w_in, sg_ln_g=v_sg_ln_g, sg_ln_b=v_sg_ln_b, sg_w=v_sg_w, sg_b=v_sg_b,
                lru_conv_w=v_lru_conv_w, lru_conv_b=v_lru_conv_b, lru_wa=v_lru_wa, lru_ba=v_lru_ba,
                lru_wx=v_lru_wx, lru_bx=v_lru_bx, lru_lam=v_lru_lam, p_sg=v_p_sg, p_lru=v_p_lru, w_out=v_w_out,
                g_ffn=v_g_ffn, w_up=v_w_up, ffn_conv_w=v_ffn_conv_w, ffn_conv_b=v_ffn_conv_b, w_down=v_w_down,
                g_final=v_g_final)
    names = list(params)
    big = ["w_in", "p_sg", "p_lru", "w_out", "w_up", "w_down"]
    col_sharded = {"w_in", "p_sg", "w_up"}
    small = [n for n in names if n not in big]

    batch, S, D = x.shape
    T = batch * S
    W_sg = sg_ln_g.shape[-1]
    H, _, Dh = lru_wa.shape[1:]
    W_lru = H * Dh
    K_lru = lru_conv_w.shape[1]
    K_ffn = ffn_conv_w.shape[1]
    F2 = ffn_conv_b.shape[-1]
    off_lru = 2 * W_sg
    off_gate = off_lru + 2 * W_lru

    cx, cy, cc = _mesh_pos()
    chip = 2 * cx + cy
    cpos = jnp.reshape(cc, (1,)).astype(jnp.int32)
    pos = jnp.stack([chip, cc]).astype(jnp.int32)

    xf = x.reshape(T, D)
    tgt = loss_target.reshape(T, D)

    wb = {n: _to_slab(params[n][0], pos, BF16, name=f"cast_{n}") for n in big}
    rows = {n: wb[n].shape[1] for n in big}
    sharded_small = ["lru_conv_w", "ffn_conv_w", "lru_wa", "lru_wx"]
    unit_g = 2 * BF16_ROWS * LANES
    pack_g = 256 * LANES
    sm_shapes = [params[n][0].shape for n in sharded_small]
    sm = _to_slab(_pack([params[n][0] for n in sharded_small], unit_g, pack_g), pos, F32, name="slab_small")

    parts = 32

    def ici(n, lo=0, hi=parts):
        return _GatherIci(wb[n], rows[n] * lo // parts, rows[n] * hi // parts)

    def fwd(n, lo=0, hi=parts):
        return _GatherFwd(wb[n], rows[n] * lo // parts, rows[n] * hi // parts)

    quarters = [_GatherRows(wb["w_in"], rows["w_in"] * i // 4, rows["w_in"] * (i + 1) // 4) for i in range(4)]
    w_in_g, sm = _comm_call(_Carry(*quarters, _GatherRows(sm, 0, sm.shape[1])), name="gather_first")
    cwl_s, cwf_s, wa_s, wx_s = _unpack(sm, sm_shapes, unit_g)
    lru_cw = jnp.transpose(cwl_s, (1, 0, 2)).reshape(K_lru, W_lru)
    ffn_cw = jnp.transpose(cwf_s, (1, 0, 2)).reshape(K_ffn, F2)
    wa_full = jnp.transpose(wa_s, (1, 0, 2, 3)).reshape(H, Dh, Dh)
    wx_full = jnp.transpose(wx_s, (1, 0, 2, 3)).reshape(H, Dh, Dh)
    wa_b, wx_b = wa_full.astype(BF16), wx_full.astype(BF16)
    wat_b, wxt_b = jnp.swapaxes(wa_b, 1, 2), jnp.swapaxes(wx_b, 1, 2)

    wm, wmt = _sg_mask(sg_w[0], name="sg_mask")
    bt = sg_b[0].T

    h1 = _rms_fwd(xf, g_mix, name="rms1_fwd")
    proj, (wb["p_sg"], wb["p_lru"], wb["w_out"], wb["w_up"]) = _mm_nn(
        h1, w_in_g, out_dtype=F32, name="mm_proj",
        carry=_Carry(ici("p_sg"), ici("p_lru"), ici("w_out"), ici("w_up", 0, 8)))
    y_a, (wb["w_up"],) = _sg_fwd(proj, sg_ln_g, sg_ln_b, wm, bt, name="sg_fwd", carry=_Carry(ici("w_up", 8, 9)))
    y_b, hseq, (p_sg_g, p_lru_g, w_out_g, wb["w_up"]) = _lru_fwd(
        proj, lru_cw, lru_conv_b, wa_b, wx_b, lru_ba, lru_bx, lru_lam, batch=batch, off_x=off_lru, name="lru_fwd",
        carry=_Carry(fwd("p_sg"), fwd("p_lru"), fwd("w_out"), fwd("w_up", 0, 8), fwd("w_up", 8, 9),
                     ici("w_up", 9, 18)))
    p_lru_g = p_lru_g.reshape(-1, D)
    w_out_g = w_out_g.reshape(-1, D)
    pa, (wb["w_up"],) = _mm_nn(y_a, p_sg_g, out_dtype=F32, name="mm_pa",
                               carry=_Carry(fwd("w_up", 9, 18), ici("w_up", 18, 20)))
    pb, (wb["w_up"],) = _mm_nn(y_b, p_lru_g, out_dtype=F32, name="mm_pb",
                               carry=_Carry(fwd("w_up", 18, 20), ici("w_up", 20, 25)))
    merged, (wb["w_up"],) = _merge_fwd(proj, pa, pb, off_a=off_gate, name="merge_fwd",
                                       carry=_Carry(fwd("w_up", 20, 25), ici("w_up", 25, 28)))
    x1, (wb["w_up"],) = _mm_nn(merged, w_out_g, out_dtype=F32, res=xf, name="mm_out",
                               carry=_Carry(fwd("w_up", 25, 28), ici("w_up", 28, 32)))
    h2, (w_up_g,) = _rms_fwd(x1, g_ffn, name="rms2_fwd", carry=_Carry(fwd("w_up", 28, 32)))
    up0, (wb["w_down"],) = _mm_nn(h2, w_up_g, out_dtype=F32, name="mm_up", carry=_Carry(ici("w_down")))
    act, (w_down_g,) = _ffn_act_fwd(up0, ffn_cw, ffn_conv_b, batch=batch, name="ffn_act_fwd",
                                    carry=_Carry(fwd("w_down")))
    w_down_g = w_down_g.reshape(-1, D)
    x2 = _mm_nn(act, w_down_g, out_dtype=F32, res=x1, name="mm_down")
    lvec, dx2, dx2_b, dg_final = _loss_head(x2, tgt, g_final.reshape(1, D), name="loss_head")
    loss = lax.psum(jnp.sum(lvec) * (0.5 / D), ("x", "y", "c"))

    assert 2 * W_sg == W_lru == D, "d proj travels in equal column pieces"
    ps, qs = {}, {}

    def pair_add(g, rb, tag):
        return _pair_add_halves(g, rb, cpos, name=f"rs_pair_add_{tag}")

    g = _mm_tn(act, dx2_b, out_dtype=BF16, name="mm_dw_down").reshape(N_CHIPS, -1, D)
    dact, (rb,) = _mm_nt(dx2_b, w_down_g, out_dtype=F32, name="mm_dact", carry=_Carry(_PairSwap(g)))
    ps["w_down"] = pair_add(g, rb, "w_down")
    dup0, d_ffn_cw, d_ffn_cb = _ffn_act_bwd(up0, ffn_cw, ffn_conv_b, dact, batch=batch, name="ffn_act_bwd")
    g, (qs["w_down"],) = _mm_tn(h2, dup0, out_dtype=BF16, col_shards=N_CHIPS, name="mm_dw_up",
                                carry=_Carry(_ChipExchange(ps["w_down"])))
    sums = {}

    def final_half(n):
        return _ShareHalves(_chip_final_add(ps[n], qs[n], pos, name=f"rs_final_add_{n}"))

    dh2, (rb, sums["w_down"]) = _mm_nt(dup0, w_up_g, out_dtype=F32, name="mm_dh2",
                                       carry=_Carry(_PairSwap(g), final_half("w_down")))
    ps["w_up"] = pair_add(g, rb, "w_up")

    def up_piece(q, lo, hi):
        return _ChipExchange(ps["w_up"], q, lo, hi, 8)

    dx1, dx1_b, dg_ffn = _rms_bwd(x1, g_ffn, dh2, dx2, name="rms2_bwd")
    g, (q_up,) = _mm_tn(merged, dx1_b, out_dtype=BF16, name="mm_dw_out", carry=_Carry(up_piece(None, 0, 1)))
    g = g.reshape(N_CHIPS, -1, D)
    dmerged, (rb, q_up) = _mm_nt(dx1_b, w_out_g, out_dtype=F32, name="mm_dmerged",
                                 carry=_Carry(_PairSwap(g), up_piece(q_up, 1, 2)))
    ps["w_out"] = pair_add(g, rb, "w_out")
    dproj, dpa, dpb = _merge_bwd(proj, pa, pb, dmerged, off_a=off_gate, name="merge_bwd")
    g_sg = _mm_tn(y_a, dpa, out_dtype=BF16, col_shards=N_CHIPS, name="mm_dp_sg")
    g_lru, (q_up,) = _mm_tn(y_b, dpb, out_dtype=BF16, name="mm_dp_lru", carry=_Carry(up_piece(q_up, 2, 3)))
    g_lru = g_lru.reshape(N_CHIPS, -1, D)
    dya, (rb,) = _mm_nt(dpa, p_sg_g, out_dtype=F32, name="mm_dya", carry=_Carry(_PairSwap(g_sg)))
    ps["p_sg"] = pair_add(g_sg, rb, "p_sg")
    dyb, (rb, q_up) = _mm_nt(dpb, p_lru_g, out_dtype=F32, name="mm_dyb",
                             carry=_Carry(_PairSwap(g_lru), up_piece(q_up, 3, 4)))
    ps["p_lru"] = pair_add(g_lru, rb, "p_lru")
    (dproj, d_wa, d_wx, d_lru_cw, d_lru_cb, d_ba, d_bx, d_lam), (qs["w_up"],) = _lru_bwd(
        proj, hseq, dyb, dproj, lru_cw, lru_conv_b, wa_b, wx_b, wat_b, wxt_b, lru_ba, lru_bx, lru_lam,
        batch=batch, off_x=off_lru, name="lru_bwd", carry=_Carry(up_piece(q_up, 4, 8)))
    g = jnp.stack([d_wa, d_wx]).reshape(2, H, N_CHIPS, Dh // N_CHIPS, Dh)
    g = jnp.transpose(g, (2, 0, 1, 3, 4)).reshape(N_CHIPS, -1, Dh).astype(BF16)
    ps["gates"] = _pair_sum(g, cpos, tag="gates")
    dproj, d_wm, d_bt, d_lg, d_lb = _sg_bwd(proj, dya, dproj, sg_ln_g, sg_ln_b, wm, wmt, bt, name="sg_bwd")
    late = ["w_out", "p_sg", "p_lru", "gates"]
    g, carried = _mm_tn(h1, dproj, out_dtype=BF16, col_shards=N_CHIPS, name="mm_dw_in", b_shift=DPROJ_SHIFT,
                        carry=_Carry(*[_ChipExchange(ps[n]) for n in late], final_half("w_up")))
    qs.update(zip(late, carried[:-1]))
    sums["w_up"] = carried[-1]
    ps["w_in"] = _pair_sum(g, cpos, tag="w_in")
    dh1, carried = _mm_nt(dproj, w_in_g, out_dtype=F32, name="mm_dh1", a_shift=DPROJ_SHIFT,
                          carry=_Carry(_ChipExchange(ps["w_in"]), *[final_half(n) for n in late]))
    qs["w_in"] = carried[0]
    sums.update(zip(late, carried[1:]))
    dx, _, dg_mix = _rms_bwd(xf, g_mix, dh1, dx1, name="rms1_bwd")
    sums["w_in"], = _comm_call(_Carry(final_half("w_in")), name="rs_share")

    grads = dict(sums)
    d_gates = grads.pop("gates").reshape(2, H, Dh // N_CHIPS, Dh)
    grads["lru_wa"], grads["lru_wx"] = d_gates[0].reshape(lru_wa.shape), d_gates[1].reshape(lru_wx.shape)
    small_full = dict(g_mix=dg_mix, sg_ln_g=d_lg, sg_ln_b=d_lb, sg_w=d_wm, sg_b=d_bt.T, lru_conv_w=d_lru_cw,
                      lru_conv_b=d_lru_cb, lru_ba=d_ba, lru_bx=d_bx, lru_lam=d_lam,
                      g_ffn=dg_ffn, ffn_conv_w=d_ffn_cw, ffn_conv_b=d_ffn_cb, g_final=dg_final)
    reduced = list(small_full)
    unit_s = SUBLANES * LANES
    pack_s = 512 * LANES
    red = _all_reduce(_pack([small_full[n] for n in reduced], unit_s, pack_s), pos, tag="small")
    red = dict(zip(reduced, _unpack(red, [small_full[n].shape for n in reduced], unit_s)))
    cs_lru = W_lru // N_CHIPS
    cs_ffn = F2 // N_CHIPS
    red["lru_conv_w"] = lax.dynamic_slice_in_dim(red["lru_conv_w"], chip * cs_lru, cs_lru, axis=1)
    red["ffn_conv_w"] = lax.dynamic_slice_in_dim(red["ffn_conv_w"], chip * cs_ffn, cs_ffn, axis=1)
    for n in reduced:
        grads[n] = red[n].reshape(params[n].shape)

    delta, new_m, new_v = {}, {}, {}

    for n in big:
        shp = params[n].shape
        two_d = (-1, shp[-1])
        d, nm, nv, gr = _adamw(params[n].reshape(two_d), grads[n], mom1[n].reshape(two_d), mom2[n].reshape(two_d),
                               name=f"adamw_{n}", pass_grad=True)
        delta[n], new_m[n], new_v[n], grads[n] = d.reshape(shp), nm.reshape(shp), nv.reshape(shp), gr.reshape(shp)
    packs = [_pack([src[n] for n in small], unit_s, pack_s) for src in (params, grads, mom1, mom2)]
    outs = _adamw(*packs, name="adamw_small")
    shapes = [params[n].shape for n in small]
    for dst, packed in zip((delta, new_m, new_v), outs):
        dst.update(dict(zip(small, _unpack(packed, shapes, unit_s))))

    return (loss, dx.reshape(x.shape), *[grads[n] for n in names], *[delta[n] for n in names],
            *[new_m[n] for n in names], *[new_v[n] for n in names])
```
